```python
import math
import jax
import jax.numpy as jnp
from jax import lax
import numpy as np

D_MODEL = 1024
BATCH = 2
SEQ = 8192
DEPTH = 4

CHUNK = 64
N_META = 16
QBLOCK = 128
FRONT_PAD = QBLOCK - N_META
N_MIXERS = 3
N_DN_LAYERS = (DEPTH + 2) // 3
N_DA_LAYERS = (DEPTH + 1) // 3
N_LRU_LAYERS = DEPTH // 3
EPS = 1e-6

DN_HEADS = 8
DN_HEAD_K = 128
DN_HEAD_V = 128
DN_QK = DN_HEADS * DN_HEAD_K
DN_V = DN_HEADS * DN_HEAD_V
DN_CONV = 4
DN_CONV_CH = 2 * DN_QK + DN_V
DN_IN = DN_CONV_CH + DN_V + 2 * DN_HEADS

DA_HEADS = 8
DA_HEAD = D_MODEL // (2 * DA_HEADS)
DA_QK = DA_HEADS * 2 * DA_HEAD
DA_V = DA_HEADS * 2 * DA_HEAD
DA_IN = 2 * DA_QK + DA_V
N_BUCKETS = 32
MAX_DISTANCE = 128
NEG_INF = -1e30

LRU_WIDTH = D_MODEL
LRU_BLOCKS = 4
LRU_BLOCK = LRU_WIDTH // LRU_BLOCKS
LRU_CONV = 4
LRU_C = 8.0

D_FF = 4 * D_MODEL

kernel_name = "hybrid_deltanet_diffattn_rglru_trunk"


def rmsnorm(x, w):
    xf = x.astype(jnp.float32)
    y = xf * lax.rsqrt(jnp.mean(xf * xf, axis=-1, keepdims=True) + EPS)
    return (y * w.astype(jnp.float32)).astype(x.dtype)


def l2norm(x):
    xf = x.astype(jnp.float32)
    return (xf * lax.rsqrt(jnp.sum(xf * xf, axis=-1, keepdims=True) + EPS)).astype(x.dtype)


def causal_depthwise_conv(x, w):
    width, ch = w.shape
    return lax.conv_general_dilated(
        x, w[:, None, :].astype(x.dtype), window_strides=(1,),
        padding=[(width - 1, 0)], dimension_numbers=("NWC", "WIO", "NWC"),
        feature_group_count=ch)


def t5_bucket(rel):
    nb = N_BUCKETS // 2
    ret = jnp.where(rel > 0, nb, 0)
    n = jnp.abs(rel)
    max_exact = nb // 2
    nf = jnp.maximum(n, 1).astype(jnp.float32)
    large = max_exact + (jnp.log(nf / max_exact) / math.log(MAX_DISTANCE / max_exact)
                         * (nb - max_exact)).astype(jnp.int32)
    large = jnp.minimum(large, nb - 1)
    return ret + jnp.where(n < max_exact, n, large)


def chunk_gated_delta_rule(q, k, v, beta, g):
    f32 = jnp.float32
    out_dtype = v.dtype
    b, L, H, dk = q.shape
    dv = v.shape[-1]
    nc = L // CHUNK

    def chunks(t):
        t = t.astype(f32).reshape(b, nc, CHUNK, H, -1)
        return t.transpose(1, 0, 3, 2, 4)

    qc, kc, vc = chunks(q), chunks(k), chunks(v)
    bc = chunks(beta[..., None])[..., 0]
    gc = jnp.cumsum(chunks(g[..., None])[..., 0], axis=-1)
    idx = jnp.arange(CHUNK)
    incl = idx[:, None] >= idx[None, :]
    strict = idx[:, None] > idx[None, :]
    decay = jnp.exp(jnp.where(incl, gc[..., :, None] - gc[..., None, :], -jnp.inf))
    kb = kc * bc[..., None]
    a_strict = jnp.where(strict, jnp.einsum("nbhid,nbhjd->nbhij", kb, kc) * decay, 0.0)
    m = a_strict + jnp.eye(CHUNK, dtype=f32)
    u = lax.linalg.triangular_solve(m, vc * bc[..., None], left_side=True,
                                    lower=True, unit_diagonal=True)
    w = lax.linalg.triangular_solve(m, kb * jnp.exp(gc)[..., None], left_side=True,
                                    lower=True, unit_diagonal=True)

    def step(state, xs):
        qi, ki, ui, wi, gi, di = xs
        attn = jnp.einsum("bhid,bhjd->bhij", qi, ki) * di
        v_new = ui - jnp.einsum("bhid,bhde->bhie", wi, state)
        o = (jnp.einsum("bhid,bhde->bhie", qi * jnp.exp(gi)[..., None], state)
             + jnp.einsum("bhij,bhje->bhie", attn, v_new))
        g_last = gi[..., -1]
        state = (state * jnp.exp(g_last)[..., None, None]
                 + jnp.einsum("bhid,bhie->bhde",
                              ki * jnp.exp(g_last[..., None] - gi)[..., None], v_new))
        return state, o

    s0 = jnp.zeros((b, H, dk, dv), f32)
    _, o = lax.scan(step, s0, (qc, kc, u, w, gc, decay))
    o = o.transpose(1, 0, 3, 2, 4).reshape(b, L, H, dv)
    return o.astype(out_dtype)


def gated_deltanet(u, valid, w_in, conv_w, a_log, dt_bias, norm_w, w_out):
    f32 = jnp.float32
    b, L, _ = u.shape
    proj = u @ w_in
    qkv = jax.nn.silu(causal_depthwise_conv(proj[..., :DN_CONV_CH], conv_w))
    z = proj[..., DN_CONV_CH:DN_CONV_CH + DN_V].reshape(b, L, DN_HEADS, DN_HEAD_V)
    beta_logit = proj[..., DN_CONV_CH + DN_V:DN_CONV_CH + DN_V + DN_HEADS]
    a = proj[..., DN_CONV_CH + DN_V + DN_HEADS:]
    q = l2norm(qkv[..., :DN_QK].reshape(b, L, DN_HEADS, DN_HEAD_K)) * (DN_HEAD_K ** -0.5)
    k = l2norm(qkv[..., DN_QK:2 * DN_QK].reshape(b, L, DN_HEADS, DN_HEAD_K))
    v = qkv[..., 2 * DN_QK:].reshape(b, L, DN_HEADS, DN_HEAD_V)
    m = valid[None, :, None].astype(f32)
    beta = jax.nn.sigmoid(beta_logit.astype(f32))
    g = -jnp.exp(a_log.astype(f32)) * jax.nn.softplus(a.astype(f32) + dt_bias.astype(f32)) * m
    k = k * m[..., None].astype(k.dtype)
    v = v * m[..., None].astype(v.dtype)
    o = chunk_gated_delta_rule(q, k, v, beta, g)
    o = rmsnorm(o, norm_w) * jax.nn.silu(z)
    return o.reshape(b, L, DN_V) @ w_out


def diff_attention(u, valid, w_in, lam_q1, lam_k1, lam_q2, lam_k2, subln_w, w_out,
                   rel_bias, lambda_init):
    f32 = jnp.float32
    b, L, _ = u.shape
    proj = u @ w_in
    q = proj[..., :DA_QK].reshape(b, L, DA_HEADS, 2, DA_HEAD)
    k = proj[..., DA_QK:2 * DA_QK].reshape(b, L, DA_HEADS, 2, DA_HEAD)
    v = proj[..., 2 * DA_QK:].reshape(b, L, DA_HEADS, 2 * DA_HEAD)
    lam = (jnp.exp(jnp.sum(lam_q1.astype(f32) * lam_k1.astype(f32)))
           - jnp.exp(jnp.sum(lam_q2.astype(f32) * lam_k2.astype(f32))) + lambda_init)
    scale = DA_HEAD ** -0.5
    pos = jnp.arange(L)
    kchunk = pos // CHUNK
    nq = L // QBLOCK
    qb = q.reshape(b, nq, QBLOCK, DA_HEADS, 2, DA_HEAD).transpose(1, 0, 2, 3, 4, 5)
    table = rel_bias.astype(f32)

    def block(args):
        qi, bi = args
        qpos = bi * QBLOCK + jnp.arange(QBLOCK)
        s = jnp.einsum("bqhcd,bkhcd->bhcqk", qi, k).astype(f32) * scale
        bias = table[t5_bucket(pos[None, :] - qpos[:, None])]
        s = s + jnp.moveaxis(bias, -1, 0)[None, :, None]
        allowed = (kchunk[None, :] <= (qpos // CHUNK)[:, None]) & valid[None, :]
        s = jnp.where(allowed, s, NEG_INF)
        p = jax.nn.softmax(s, axis=-1)
        attn = p[:, :, 0] - lam * p[:, :, 1]
        return jnp.einsum("bhqk,bkhe->bqhe", attn.astype(v.dtype), v)

    o = lax.map(block, (qb, jnp.arange(nq)))
    o = o.transpose(1, 0, 2, 3, 4).reshape(b, L, DA_HEADS, 2 * DA_HEAD)
    o = rmsnorm(o, subln_w) * (1.0 - lambda_init)
    return o.reshape(b, L, DA_V) @ w_out


def rglru_block(u, valid, w_in, conv_w, conv_b, w_rgate, b_rgate, w_igate, b_igate,
                lam, w_out):
    f32 = jnp.float32
    b, L, _ = u.shape
    proj = u @ w_in
    gate = jax.nn.gelu(proj[..., :LRU_WIDTH], approximate=True)
    xr = (causal_depthwise_conv(proj[..., LRU_WIDTH:], conv_w) + conv_b) \
        * valid[None, :, None].astype(u.dtype)
    xb = xr.reshape(b, L, LRU_BLOCKS, LRU_BLOCK)
    r = jax.nn.sigmoid((jnp.einsum("blgi,gio->blgo", xb, w_rgate).reshape(b, L, LRU_WIDTH)
                        + b_rgate).astype(f32))
    i = jax.nn.sigmoid((jnp.einsum("blgi,gio->blgo", xb, w_igate).reshape(b, L, LRU_WIDTH)
                        + b_igate).astype(f32))
    log_a = -LRU_C * r * jax.nn.softplus(-lam.astype(f32))
    a = jnp.exp(log_a)
    inp = jnp.sqrt(-jnp.expm1(2.0 * log_a)) * (i * xr.astype(f32))

    def combine(c1, c2):
        a1, b1 = c1
        a2, b2 = c2
        return a1 * a2, a2 * b1 + b2

    _, hs = lax.associative_scan(combine, (a, inp), axis=1)
    y = hs.astype(u.dtype) * gate
    return y @ w_out


def sq_relu_mlp(u, w1, w2):
    return jnp.square(jax.nn.relu(u @ w1)) @ w2


def setup_inputs(seed: int = 0) -> dict:
    key = jax.random.key(seed)
    ks = list(jax.random.split(key, 32))
    f32 = jnp.float32

    def nrm(i, shape, scale):
        return scale * jax.random.normal(ks[i], shape, f32)

    def uni(i, shape, lo, hi):
        return jax.random.uniform(ks[i], shape, f32, lo, hi)

    dt = jnp.exp(uni(9, (N_DN_LAYERS, DN_HEADS), math.log(1e-3), math.log(1e-1)))
    s = uni(26, (N_LRU_LAYERS, LRU_WIDTH), 0.9, 0.999) ** (1.0 / LRU_C)
    return {
        "x": nrm(0, (BATCH, SEQ, D_MODEL), 1.0),
        "meta_tokens": nrm(1, (N_META, D_MODEL), 1.0),
        "rel_bias": nrm(2, (N_BUCKETS, DA_HEADS), 0.5),
        "norm_mix_w": 1.0 + nrm(3, (DEPTH, D_MODEL), 0.1),
        "norm_mlp_w": 1.0 + nrm(4, (DEPTH, D_MODEL), 0.1),
        "final_norm_w": 1.0 + nrm(5, (D_MODEL,), 0.1),
        "dn_w_in": nrm(6, (N_DN_LAYERS, D_MODEL, DN_IN), D_MODEL ** -0.5),
        "dn_conv_w": nrm(7, (N_DN_LAYERS, DN_CONV, DN_CONV_CH), DN_CONV ** -0.5),
        "dn_a_log": jnp.log(uni(8, (N_DN_LAYERS, DN_HEADS), 1.0, 16.0)),
        "dn_dt_bias": dt + jnp.log(-jnp.expm1(-dt)),
        "dn_norm_w": 1.0 + nrm(10, (N_DN_LAYERS, DN_HEAD_V), 0.1),
        "dn_w_out": nrm(11, (N_DN_LAYERS, DN_V, D_MODEL), DN_V ** -0.5),
        "da_w_in": nrm(12, (N_DA_LAYERS, D_MODEL, DA_IN), D_MODEL ** -0.5),
        "da_lam_q1": nrm(13, (N_DA_LAYERS, DA_HEAD), 0.1),
        "da_lam_k1": nrm(14, (N_DA_LAYERS, DA_HEAD), 0.1),
        "da_lam_q2": nrm(15, (N_DA_LAYERS, DA_HEAD), 0.1),
        "da_lam_k2": nrm(16, (N_DA_LAYERS, DA_HEAD), 0.1),
        "da_subln_w": 1.0 + nrm(17, (N_DA_LAYERS, 2 * DA_HEAD), 0.1),
        "da_w_out": nrm(18, (N_DA_LAYERS, DA_V, D_MODEL), DA_V ** -0.5),
        "lru_w_in": nrm(19, (N_LRU_LAYERS, D_MODEL, 2 * LRU_WIDTH), D_MODEL ** -0.5),
        "lru_conv_w": nrm(20, (N_LRU_LAYERS, LRU_CONV, LRU_WIDTH), LRU_CONV ** -0.5),
        "lru_conv_b": nrm(21, (N_LRU_LAYERS, LRU_WIDTH), 0.01),
        "lru_w_rgate": nrm(22, (N_LRU_LAYERS, LRU_BLOCKS, LRU_BLOCK, LRU_BLOCK), LRU_BLOCK ** -0.5),
        "lru_b_rgate": nrm(23, (N_LRU_LAYERS, LRU_WIDTH), 0.01),
        "lru_w_igate": nrm(24, (N_LRU_LAYERS, LRU_BLOCKS, LRU_BLOCK, LRU_BLOCK), LRU_BLOCK ** -0.5),
        "lru_b_igate": nrm(25, (N_LRU_LAYERS, LRU_WIDTH), 0.01),
        "lru_lambda": jnp.log(s) - jnp.log1p(-s),
        "lru_w_out": nrm(27, (N_LRU_LAYERS, LRU_WIDTH, D_MODEL), LRU_WIDTH ** -0.5),
        "mlp_w1": nrm(28, (DEPTH, D_MODEL, D_FF), D_MODEL ** -0.5),
        "mlp_w2": nrm(29, (DEPTH, D_FF, D_MODEL), D_FF ** -0.5),
    }


def reference(x, meta_tokens, rel_bias, norm_mix_w, norm_mlp_w, final_norm_w,
              dn_w_in, dn_conv_w, dn_a_log, dn_dt_bias, dn_norm_w, dn_w_out,
              da_w_in, da_lam_q1, da_lam_k1, da_lam_q2, da_lam_k2, da_subln_w, da_w_out,
              lru_w_in, lru_conv_w, lru_conv_b, lru_w_rgate, lru_b_rgate, lru_w_igate,
              lru_b_igate, lru_lambda, lru_w_out, mlp_w1, mlp_w2):
    b = x.shape[0]
    h = jnp.concatenate([
        jnp.zeros((b, FRONT_PAD, D_MODEL), x.dtype),
        jnp.broadcast_to(meta_tokens[None].astype(x.dtype), (b, N_META, D_MODEL)),
        x,
    ], axis=1)
    L = h.shape[1]
    valid = jnp.arange(L) >= FRONT_PAD
    keep = valid[None, :, None].astype(h.dtype)

    for layer in range(DEPTH):
        kind = layer % N_MIXERS
        slot = layer // N_MIXERS
        u = rmsnorm(h, norm_mix_w[layer])
        if kind == 0:
            y = gated_deltanet(u, valid, dn_w_in[slot], dn_conv_w[slot], dn_a_log[slot],
                               dn_dt_bias[slot], dn_norm_w[slot], dn_w_out[slot])
        elif kind == 1:
            lambda_init = 0.8 - 0.6 * math.exp(-0.3 * layer)
            y = diff_attention(u, valid, da_w_in[slot], da_lam_q1[slot], da_lam_k1[slot],
                               da_lam_q2[slot], da_lam_k2[slot], da_subln_w[slot],
                               da_w_out[slot], rel_bias, lambda_init)
        else:
            y = rglru_block(u, valid, lru_w_in[slot], lru_conv_w[slot], lru_conv_b[slot],
                            lru_w_rgate[slot], lru_b_rgate[slot], lru_w_igate[slot],
                            lru_b_igate[slot], lru_lambda[slot], lru_w_out[slot])
        h = h + keep * y
        u = rmsnorm(h, norm_mlp_w[layer])
        h = h + keep * sq_relu_mlp(u, mlp_w1[layer], mlp_w2[layer])

    h = rmsnorm(h, final_norm_w)
    return h[:, FRONT_PAD + N_META:]
```

```python
import functools
import math

import jax
import jax.numpy as jnp
from jax import lax
from jax.experimental import pallas as pl
from jax.experimental.pallas import tpu as pltpu

F32 = jnp.float32
BF16 = jnp.bfloat16

D_MODEL = 1024
N_META = 16
QBLOCK = 128
FRONT_PAD = QBLOCK - N_META
N_MIXERS = 3
EPS = 1e-6
CHUNK = 64
CHUNK_SHIFT = 6

DN_HEADS = 8
DN_HEAD = 128
DN_CONV = 4
DN_QKV = 3 * DN_HEADS * DN_HEAD
DN_MAIN = DN_QKV + DN_HEADS * DN_HEAD
DN_SMALL = 128
DN_CHUNK = 128
INV_BASE_LOG = 4

DA_HEADS = 8
DA_HEAD = 64
N_BUCKETS = 32
MAX_DISTANCE = 128
NEG_INF = -1e30

LRU_WIDTH = 1024
LRU_BLOCKS = 4
LRU_BLOCK = LRU_WIDTH // LRU_BLOCKS
LRU_CONV = 4
LRU_C = 8.0

D_FF = 4 * D_MODEL

V7X_VMEM_LIMIT_BYTES = 56 * 1024 * 1024


def _params(*semantics):
    return pltpu.CompilerParams(dimension_semantics=semantics,
                                vmem_limit_bytes=V7X_VMEM_LIMIT_BYTES)


def _pick(n, candidates):
    for c in candidates:
        if n % c == 0:
            return c
    raise ValueError(f"no tile for {n} in {candidates}")


def _dot(a, b):
    return jnp.dot(a, b, preferred_element_type=F32)


def _dot_nt(a, b):
    return lax.dot_general(a, b, (((1,), (1,)), ((), ())), preferred_element_type=F32)


def _split2(a):
    hi = a.astype(BF16)
    lo = (a - hi.astype(F32)).astype(BF16)
    return hi, lo


def _dot3(a, b):
    ah, al = _split2(a)
    bh, bl = _split2(b)
    return _dot(ah, bh) + (_dot(ah, bl) + _dot(al, bh))


def _rmsnorm_rows(x, w):
    return x * lax.rsqrt(jnp.mean(x * x, axis=-1, keepdims=True) + EPS) * w


def _keep_rows(tile_index, rows):
    pos = tile_index * rows + lax.broadcasted_iota(jnp.int32, (rows, 1), 0)
    return pos >= FRONT_PAD


def _norm_matmul_kernel(h_ref, nw_ref, w_ref, o_ref, u_ref):
    @pl.when(pl.program_id(2) == 0)
    def _():
        u_ref[...] = _rmsnorm_rows(h_ref[0], nw_ref[...]).astype(BF16)

    o_ref[0] = _dot(u_ref[...], w_ref[...]).astype(o_ref.dtype)


def norm_matmul(h, nw, w, out_dtype):
    b, L, d = h.shape
    n = w.shape[1]
    tm = _pick(L, (1040, 640, 320, 128))
    tn = _pick(n, (1408, 1024, 512, 128))
    return pl.pallas_call(
        _norm_matmul_kernel,
        grid=(b, L // tm, n // tn),
        in_specs=[
            pl.BlockSpec((1, tm, d), lambda bi, i, j: (bi, i, 0)),
            pl.BlockSpec((1, d), lambda bi, i, j: (0, 0)),
            pl.BlockSpec((d, tn), lambda bi, i, j: (0, j)),
        ],
        out_specs=pl.BlockSpec((1, tm, tn), lambda bi, i, j: (bi, i, j)),
        out_shape=jax.ShapeDtypeStruct((b, L, n), out_dtype),
        scratch_shapes=[pltpu.VMEM((tm, d), BF16)],
        compiler_params=_params("arbitrary", "arbitrary", "arbitrary"),
        name="norm_matmul",
    )(h, nw.reshape(1, d), w)


def _out_proj_kernel(y_ref, w_ref, h_ref, o_ref):
    tm = y_ref.shape[1]
    y = _dot(y_ref[0].astype(BF16), w_ref[...])
    keep = _keep_rows(pl.program_id(1), tm)
    o_ref[0] = h_ref[0] + jnp.where(keep, y, 0.0)


def out_proj_residual(y, w, h):
    b, L, d = h.shape
    k = y.shape[-1]
    tm = _pick(L, (1040, 640, 320, 128))
    return pl.pallas_call(
        _out_proj_kernel,
        grid=(b, L // tm),
        in_specs=[
            pl.BlockSpec((1, tm, k), lambda bi, i: (bi, i, 0)),
            pl.BlockSpec((k, d), lambda bi, i: (0, 0)),
            pl.BlockSpec((1, tm, d), lambda bi, i: (bi, i, 0)),
        ],
        out_specs=pl.BlockSpec((1, tm, d), lambda bi, i: (bi, i, 0)),
        out_shape=jax.ShapeDtypeStruct((b, L, d), F32),
        compiler_params=_params("arbitrary", "arbitrary"),
        name="out_proj_residual",
    )(y, w, h)


def _mlp_kernel(h_ref, nw_ref, w1_ref, w2_ref, fw_ref, o_ref, u_ref, acc_ref, *, final_norm):
    f = pl.program_id(2)
    tm = h_ref.shape[1]

    @pl.when(f == 0)
    def _():
        u_ref[...] = _rmsnorm_rows(h_ref[0], nw_ref[...]).astype(BF16)
        acc_ref[...] = jnp.zeros_like(acc_ref)

    a = _dot(u_ref[...], w1_ref[...])
    a = jnp.square(jnp.maximum(a, 0.0)).astype(BF16)
    acc_ref[...] += _dot(a, w2_ref[...])

    @pl.when(f == pl.num_programs(2) - 1)
    def _():
        keep = _keep_rows(pl.program_id(1), tm)
        hn = h_ref[0] + jnp.where(keep, acc_ref[...], 0.0)
        if final_norm:
            hn = _rmsnorm_rows(hn, fw_ref[...])
        o_ref[0] = hn


def mlp_residual(h, nw, w1, w2, final_w=None):
    b, L, d = h.shape
    ff = w1.shape[1]
    tm = _pick(L, (1040, 640, 320, 128))
    tf = _pick(ff, (512, 128))
    final_norm = final_w is not None
    fw = (final_w if final_norm else nw).reshape(1, d)
    return pl.pallas_call(
        functools.partial(_mlp_kernel, final_norm=final_norm),
        grid=(b, L // tm, ff // tf),
        in_specs=[
            pl.BlockSpec((1, tm, d), lambda bi, i, f: (bi, i, 0)),
            pl.BlockSpec((1, d), lambda bi, i, f: (0, 0)),
            pl.BlockSpec((d, tf), lambda bi, i, f: (0, f)),
            pl.BlockSpec((tf, d), lambda bi, i, f: (f, 0)),
            pl.BlockSpec((1, d), lambda bi, i, f: (0, 0)),
        ],
        out_specs=pl.BlockSpec((1, tm, d), lambda bi, i, f: (bi, i, 0)),
        out_shape=jax.ShapeDtypeStruct((b, L, d), F32),
        scratch_shapes=[pltpu.VMEM((tm, d), BF16), pltpu.VMEM((tm, d), F32)],
        compiler_params=_params("arbitrary", "arbitrary", "arbitrary"),
        name="mlp_residual",
    )(h, nw.reshape(1, d), w1, w2, fw)


def _causal_conv_rows(xs_ref, cur, halo, first_tile, w, width):
    tl = cur.shape[0]
    xs_ref[0:8, :] = jnp.where(first_tile, 0.0, halo)
    xs_ref[8:, :] = cur
    acc = None
    for j in range(width):
        off = 8 - (width - 1) + j
        term = xs_ref[off:off + tl, :] * w[j:j + 1, :]
        acc = term if acc is None else acc + term
    return acc


def _dn_prep_kernel(x_ref, halo_ref, cw_ref, o_ref, xs_ref):
    i = pl.program_id(1)
    tl = x_ref.shape[1]
    y = _causal_conv_rows(xs_ref, x_ref[0], halo_ref[0], i == 0, cw_ref[...], DN_CONV)
    y = y * jax.nn.sigmoid(y)
    keep = _keep_rows(i, tl)
    for g in range(3 * DN_HEADS):
        cols = slice(g * DN_HEAD, (g + 1) * DN_HEAD)
        yg = y[:, cols]
        if g < 2 * DN_HEADS:
            yg = yg * lax.rsqrt(jnp.sum(yg * yg, axis=-1, keepdims=True) + EPS)
        if g < DN_HEADS:
            yg = yg * (DN_HEAD ** -0.5)
        else:
            yg = jnp.where(keep, yg, 0.0)
        o_ref[0, :, cols] = yg


def dn_prep(proj, conv_w):
    b, L, _ = proj.shape
    tl = _pick(L, (320, 128))
    return pl.pallas_call(
        _dn_prep_kernel,
        grid=(b, L // tl),
        in_specs=[
            pl.BlockSpec((1, tl, DN_QKV), lambda bi, i: (bi, i, 0)),
            pl.BlockSpec((1, 8, DN_QKV), lambda bi, i: (bi, jnp.maximum(i * (tl // 8) - 1, 0), 0)),
            pl.BlockSpec((DN_CONV, DN_QKV), lambda bi, i: (0, 0)),
        ],
        out_specs=pl.BlockSpec((1, tl, DN_QKV), lambda bi, i: (bi, i, 0)),
        out_shape=jax.ShapeDtypeStruct((b, L, DN_QKV), F32),
        scratch_shapes=[pltpu.VMEM((tl + 8, DN_QKV), F32)],
        compiler_params=_params("arbitrary", "arbitrary"),
        name="dn_prep",
    )(proj, proj, conv_w)


def _unit_lower_inverse(a):
    n = a.shape[0]
    row = lax.broadcasted_iota(jnp.int32, (n, n), 0)
    col = lax.broadcasted_iota(jnp.int32, (n, n), 1)

    def same_block(log_size):
        return lax.shift_right_logical(row, log_size) == lax.shift_right_logical(col, log_size)

    log_size = INV_BASE_LOG
    ad = jnp.where(same_block(log_size), a, 0.0)
    t = (row == col).astype(F32) - ad
    bk = ad
    for _ in range(log_size - 1):
        bk = _dot3(bk, bk)
        t = t + _dot3(t, bk)
    while (1 << log_size) < n:
        off = jnp.where(same_block(log_size + 1) & ~same_block(log_size), a, 0.0)
        t = t - _dot3(_dot3(t, off), t)
        log_size += 1
    return t


def _dn_chunk_kernel(q_ref, k_ref, v_ref, z_ref, s_ref, alog_ref, dtb_ref, nw_ref,
                     o_ref, state_ref):
    c = pl.program_id(1)
    n = DN_CHUNK

    @pl.when(c == 0)
    def _():
        state_ref[...] = jnp.zeros_like(state_ref)

    keep = _keep_rows(c, n)
    small = s_ref[0]
    beta = jax.nn.sigmoid(small)
    sp = jnp.logaddexp(small + dtb_ref[...], 0.0)
    g = jnp.where(keep, -jnp.exp(alog_ref[...]) * sp, 0.0)

    row = lax.broadcasted_iota(jnp.int32, (n, n), 0)
    col = lax.broadcasted_iota(jnp.int32, (n, n), 1)
    incl = row >= col
    strict = row > col
    tri = incl.astype(BF16)
    g_hi = g.astype(BF16)
    r1 = g - g_hi.astype(F32)
    g_mid = r1.astype(BF16)
    g_lo = (r1 - g_mid.astype(F32)).astype(BF16)
    gc = _dot(tri, g_hi) + (_dot(tri, g_mid) + _dot(tri, g_lo))
    gc_t = gc.T

    for h in range(DN_HEADS):
        cols = slice(h * DN_HEAD, (h + 1) * DN_HEAD)
        qh = q_ref[0, :, cols]
        kh = k_ref[0, :, cols]
        vh = v_ref[0, :, cols]
        bcol = beta[:, h:h + 1]
        gcol = gc[:, DN_HEADS + h:DN_HEADS + h + 1]
        grow = gc_t[DN_HEADS + h:DN_HEADS + h + 1, :]
        decay = jnp.exp(jnp.where(incl, gcol - grow, -jnp.inf))
        kb = kh * bcol
        kh16 = kh.astype(BF16)
        a = jnp.where(strict, _dot_nt(kb.astype(BF16), kh16) * decay, 0.0)
        t = _unit_lower_inverse(a)
        egc = jnp.exp(gcol)
        rhs = jnp.concatenate([vh * bcol, kb * egc], axis=1)
        uw = _dot3(t, rhs)
        u = uw[:, :DN_HEAD]
        w = uw[:, DN_HEAD:]
        s = state_ref[h]
        s16 = s.astype(BF16)
        v_new = u - _dot(w.astype(BF16), s16)
        v16 = v_new.astype(BF16)
        attn = _dot_nt(qh.astype(BF16), kh16) * decay
        o = _dot((qh * egc).astype(BF16), s16) + _dot(attn.astype(BF16), v16)
        g_last = gcol[n - 1:n, :]
        kdec = kh * jnp.exp(g_last - gcol)
        state_ref[h] = s * jnp.exp(g_last) + _dot(kdec.T.astype(BF16), v16)
        zh = z_ref[0, :, cols]
        o = _rmsnorm_rows(o, nw_ref[...]) * (zh * jax.nn.sigmoid(zh))
        o_ref[0, :, cols] = o.astype(o_ref.dtype)


def dn_chunk(qkv, proj, a_log_row, dt_bias_row, norm_w):
    b, L, _ = qkv.shape
    n = DN_CHUNK
    hd = DN_HEADS * DN_HEAD
    return pl.pallas_call(
        _dn_chunk_kernel,
        grid=(b, L // n),
        in_specs=[
            pl.BlockSpec((1, n, hd), lambda bi, c: (bi, c, 0)),
            pl.BlockSpec((1, n, hd), lambda bi, c: (bi, c, 1)),
            pl.BlockSpec((1, n, hd), lambda bi, c: (bi, c, 2)),
            pl.BlockSpec((1, n, hd), lambda bi, c: (bi, c, 3)),
            pl.BlockSpec((1, n, DN_SMALL), lambda bi, c: (bi, c, DN_MAIN // DN_SMALL)),
            pl.BlockSpec((1, DN_SMALL), lambda bi, c: (0, 0)),
            pl.BlockSpec((1, DN_SMALL), lambda bi, c: (0, 0)),
            pl.BlockSpec((1, DN_HEAD), lambda bi, c: (0, 0)),
        ],
        out_specs=pl.BlockSpec((1, n, hd), lambda bi, c: (bi, c, 0)),
        out_shape=jax.ShapeDtypeStruct((b, L, hd), BF16),
        scratch_shapes=[pltpu.VMEM((DN_HEADS, DN_HEAD, DN_HEAD), F32)],
        compiler_params=_params("arbitrary", "arbitrary"),
        name="dn_chunk",
    )(qkv, qkv, qkv, proj, proj, a_log_row, dt_bias_row, norm_w.reshape(1, DN_HEAD))


def gated_deltanet_layer(h, nw, w_in, conv_w, a_log, dt_bias, norm_w, w_out):
    d = h.shape[-1]
    w_pad = jnp.zeros((d, DN_SMALL - 2 * DN_HEADS), w_in.dtype)
    w_all = jnp.concatenate([w_in, w_pad], axis=1).astype(BF16)
    lane_pad = jnp.zeros((DN_SMALL - 2 * DN_HEADS,), F32)
    head_pad = jnp.zeros((DN_HEADS,), F32)
    a_log_row = jnp.concatenate([head_pad, a_log.astype(F32), lane_pad]).reshape(1, DN_SMALL)
    dt_bias_row = jnp.concatenate([head_pad, dt_bias.astype(F32), lane_pad]).reshape(1, DN_SMALL)
    proj = norm_matmul(h, nw, w_all, F32)
    qkv = dn_prep(proj, conv_w)
    o = dn_chunk(qkv, proj, a_log_row, dt_bias_row, norm_w)
    return out_proj_residual(o, w_out.astype(BF16), h)


def _t5_bucket(rel):
    nb = N_BUCKETS // 2
    ret = jnp.where(rel > 0, nb, 0)
    n = jnp.abs(rel)
    max_exact = nb // 2
    nf = jnp.maximum(n, 1).astype(F32)
    large = max_exact + (jnp.log(nf / max_exact) / math.log(MAX_DISTANCE / max_exact)
                         * (nb - max_exact)).astype(jnp.int32)
    large = jnp.minimum(large, nb - 1)
    return ret + jnp.where(n < max_exact, n, large)


def _bias_tile_kernel(tab_ref, o_ref):
    h = pl.program_id(0)
    which = pl.program_id(1)
    row = lax.broadcasted_iota(jnp.int32, (QBLOCK, QBLOCK), 0)
    col = lax.broadcasted_iota(jnp.int32, (QBLOCK, QBLOCK), 1)
    bucket = _t5_bucket(col - row - QBLOCK * which)
    acc = jnp.zeros((QBLOCK, QBLOCK), F32)
    for bkt in range(N_BUCKETS):
        acc = jnp.where(bucket == bkt, tab_ref[bkt, h], acc)
    o_ref[0, 0] = acc - tab_ref[N_BUCKETS // 2 - 1, h]


def bias_tiles(rel_bias):
    return pl.pallas_call(
        _bias_tile_kernel,
        grid=(DA_HEADS, 2),
        in_specs=[pl.BlockSpec(memory_space=pltpu.SMEM)],
        out_specs=pl.BlockSpec((1, 1, QBLOCK, QBLOCK), lambda h, w: (h, w, 0, 0)),
        out_shape=jax.ShapeDtypeStruct((DA_HEADS, 2, QBLOCK, QBLOCK), F32),
        compiler_params=_params("arbitrary", "arbitrary"),
        name="bias_tiles",
    )(rel_bias.astype(F32))


def _da_kernel(q_ref, k_ref, v_ref, bias_ref, lamv_ref, subw_ref, o_ref,
               m_ref, l_ref, acc_ref, *, tq, lambda_init):
    qi = pl.program_id(2)
    nsub = tq // QBLOCK
    q = q_ref[0]
    lane = lax.broadcasted_iota(jnp.int32, (1, 2 * DA_HEAD), 1)
    qc = [jnp.where(lane < DA_HEAD, q, jnp.zeros_like(q)),
          jnp.where(lane >= DA_HEAD, q, jnp.zeros_like(q))]
    scale = DA_HEAD ** -0.5

    m_ref[...] = jnp.full(m_ref.shape, NEG_INF, F32)
    l_ref[...] = jnp.zeros_like(l_ref)
    acc_ref[...] = jnp.zeros_like(acc_ref)

    def near_bias(which_block):
        rows = []
        for r in range(nsub):
            tiles = []
            for cc in range(nsub):
                dist = r - cc + nsub * which_block
                if dist == 0:
                    tiles.append(bias_ref[0, 0])
                elif dist == 1:
                    tiles.append(bias_ref[0, 1])
                else:
                    tiles.append(jnp.zeros((QBLOCK, QBLOCK), F32))
            rows.append(jnp.concatenate(tiles, axis=1))
        return jnp.concatenate(rows, axis=0)

    def step(kb, diag=False, sub=False, first=False):
        start = pl.multiple_of(kb * tq, tq)
        k = k_ref[0, pl.ds(start, tq), :]
        v = v_ref[0, pl.ds(start, tq), :]
        extra = None
        if diag:
            extra = near_bias(0)
        elif sub:
            extra = near_bias(1)
        mask = None
        if diag:
            r = lax.broadcasted_iota(jnp.int32, (tq, tq), 0)
            cidx = lax.broadcasted_iota(jnp.int32, (tq, tq), 1)
            mask = (lax.shift_right_logical(cidx, CHUNK_SHIFT)
                    <= lax.shift_right_logical(r, CHUNK_SHIFT))
        if first:
            valid = lax.broadcasted_iota(jnp.int32, (1, tq), 1) >= FRONT_PAD
            mask = valid if mask is None else (mask & valid)
        for c in range(2):
            s = _dot_nt(qc[c], k) * scale
            if extra is not None:
                s = s + extra
            if mask is not None:
                s = jnp.where(mask, s, NEG_INF)
            m_old = m_ref[c]
            m_new = jnp.maximum(m_old, jnp.max(s, axis=-1, keepdims=True))
            alpha = jnp.exp(m_old - m_new)
            p = jnp.exp(s - m_new)
            l_ref[c] = alpha * l_ref[c] + jnp.sum(p, axis=-1, keepdims=True)
            acc_ref[c] = alpha * acc_ref[c] + _dot(p.astype(BF16), v)
            m_ref[c] = m_new

    @pl.when(qi == 0)
    def _():
        step(0, diag=True, first=True)

    @pl.when(qi == 1)
    def _():
        step(0, sub=True, first=True)
        step(1, diag=True)

    @pl.when(qi >= 2)
    def _():
        step(0, first=True)

        def body(kb, carry):
            step(kb)
            return carry

        lax.fori_loop(1, qi - 1, body, 0)
        step(qi - 1, sub=True)
        step(qi, diag=True)

    lamv = lamv_ref[...]
    lam = (jnp.exp(jnp.sum(lamv[0:1] * lamv[1:2], axis=-1, keepdims=True))
           - jnp.exp(jnp.sum(lamv[2:3] * lamv[3:4], axis=-1, keepdims=True)) + lambda_init)
    o = acc_ref[0] / l_ref[0] - lam * (acc_ref[1] / l_ref[1])
    o = _rmsnorm_rows(o, subw_ref[...]) * (1.0 - lambda_init)
    o_ref[0] = o.astype(o_ref.dtype)


def diff_attention_core(proj, bias, lamv, subln_w, lambda_init):
    b, L, _ = proj.shape
    tq = _pick(L, (640, 128))
    hw = 2 * DA_HEAD
    return pl.pallas_call(
        functools.partial(_da_kernel, tq=tq, lambda_init=lambda_init),
        grid=(b, DA_HEADS, L // tq),
        in_specs=[
            pl.BlockSpec((1, tq, hw), lambda bi, h, i: (bi, i, h)),
            pl.BlockSpec((1, L, hw), lambda bi, h, i: (bi, 0, DA_HEADS + h)),
            pl.BlockSpec((1, L, hw), lambda bi, h, i: (bi, 0, 2 * DA_HEADS + h)),
            pl.BlockSpec((1, 2, QBLOCK, QBLOCK), lambda bi, h, i: (h, 0, 0, 0)),
            pl.BlockSpec((4, DA_HEAD), lambda bi, h, i: (0, 0)),
            pl.BlockSpec((1, hw), lambda bi, h, i: (0, 0)),
        ],
        out_specs=pl.BlockSpec((1, tq, hw), lambda bi, h, i: (bi, i, h)),
        out_shape=jax.ShapeDtypeStruct((b, L, DA_HEADS * hw), BF16),
        scratch_shapes=[pltpu.VMEM((2, tq, 1), F32), pltpu.VMEM((2, tq, 1), F32),
                        pltpu.VMEM((2, tq, hw), F32)],
        compiler_params=_params("arbitrary", "arbitrary", "arbitrary"),
        name="diff_attention",
    )(proj, proj, proj, bias, lamv, subln_w.reshape(1, hw))


def diff_attention_layer(h, nw, w_in, lam_q1, lam_k1, lam_q2, lam_k2, subln_w, w_out,
                         rel_bias, lambda_init):
    proj = norm_matmul(h, nw, w_in.astype(BF16), BF16)
    bias = bias_tiles(rel_bias)
    lamv = jnp.stack([lam_q1, lam_k1, lam_q2, lam_k2]).astype(F32)
    o = diff_attention_core(proj, bias, lamv, subln_w, lambda_init)
    return out_proj_residual(o, w_out.astype(BF16), h)


def _lru_kernel(gate_ref, x_ref, halo_ref, cw_ref, cb_ref, wr_ref, br_ref, wi_ref, bi_ref,
                lam_ref, o_ref, xs_ref, a_ref, b_ref, h_ref):
    i = pl.program_id(1)
    tl = x_ref.shape[1]

    @pl.when(i == 0)
    def _():
        h_ref[...] = jnp.zeros_like(h_ref)

    xr = _causal_conv_rows(xs_ref, x_ref[0], halo_ref[0], i == 0, cw_ref[...], LRU_CONV)
    xr = jnp.where(_keep_rows(i, tl), xr + cb_ref[...], 0.0)
    neg_sp = -LRU_C * jnp.logaddexp(-lam_ref[...], 0.0)
    sub = jnp.bitwise_and(lax.broadcasted_iota(jnp.int32, (tl, 1), 0), 7)
    for g in range(LRU_BLOCKS):
        cols = slice(g * LRU_BLOCK, (g + 1) * LRU_BLOCK)
        xg = xr[:, cols]
        x16 = xg.astype(BF16)
        r = jax.nn.sigmoid(_dot(x16, wr_ref[g]) + br_ref[:, cols])
        ig = jax.nn.sigmoid(_dot(x16, wi_ref[g]) + bi_ref[:, cols])
        log_a = r * neg_sp[:, cols]
        a = jnp.exp(log_a)
        inp = jnp.sqrt(jnp.maximum(-jnp.tanh(log_a) * (a * a + 1.0), 0.0)) * (ig * xg)
        for s in (1, 2, 4):
            a_sh = pltpu.roll(a, s, 0)
            b_sh = pltpu.roll(inp, s, 0)
            use = sub >= s
            inp = jnp.where(use, a * b_sh + inp, inp)
            a = jnp.where(use, a * a_sh, a)
        a_ref[:, cols] = a
        b_ref[:, cols] = inp

    def body(t, hprev):
        rows = pl.ds(pl.multiple_of(t * 8, 8), 8)
        hs = b_ref[rows, :] + a_ref[rows, :] * hprev
        b_ref[rows, :] = hs
        return hs[7:8, :]

    h_ref[...] = lax.fori_loop(0, tl // 8, body, h_ref[...])
    gate = gate_ref[0]
    gelu = 0.5 * gate * (1.0 + jnp.tanh(math.sqrt(2.0 / math.pi)
                                        * (gate + 0.044715 * (gate * gate * gate))))
    o_ref[0] = (b_ref[...] * gelu).astype(o_ref.dtype)


def lru_core(proj, conv_w, conv_b, w_r, b_r, w_i, b_i, lam):
    b, L, _ = proj.shape
    tl = _pick(L, (640, 320, 128))
    wd = LRU_WIDTH
    row = lambda a: a.astype(F32).reshape(1, wd)
    return pl.pallas_call(
        _lru_kernel,
        grid=(b, L // tl),
        in_specs=[
            pl.BlockSpec((1, tl, wd), lambda bi, i: (bi, i, 0)),
            pl.BlockSpec((1, tl, wd), lambda bi, i: (bi, i, 1)),
            pl.BlockSpec((1, 8, wd), lambda bi, i: (bi, jnp.maximum(i * (tl // 8) - 1, 0), 1)),
            pl.BlockSpec((LRU_CONV, wd), lambda bi, i: (0, 0)),
            pl.BlockSpec((1, wd), lambda bi, i: (0, 0)),
            pl.BlockSpec((LRU_BLOCKS, LRU_BLOCK, LRU_BLOCK), lambda bi, i: (0, 0, 0)),
            pl.BlockSpec((1, wd), lambda bi, i: (0, 0)),
            pl.BlockSpec((LRU_BLOCKS, LRU_BLOCK, LRU_BLOCK), lambda bi, i: (0, 0, 0)),
            pl.BlockSpec((1, wd), lambda bi, i: (0, 0)),
            pl.BlockSpec((1, wd), lambda bi, i: (0, 0)),
        ],
        out_specs=pl.BlockSpec((1, tl, wd), lambda bi, i: (bi, i, 0)),
        out_shape=jax.ShapeDtypeStruct((b, L, wd), BF16),
        scratch_shapes=[pltpu.VMEM((tl + 8, wd), F32), pltpu.VMEM((tl, wd), F32),
                        pltpu.VMEM((tl, wd), F32), pltpu.VMEM((1, wd), F32)],
        compiler_params=_params("arbitrary", "arbitrary"),
        name="rglru",
    )(proj, proj, proj, conv_w, row(conv_b), w_r.astype(BF16), row(b_r), w_i.astype(BF16),
      row(b_i), row(lam))


def rglru_layer(h, nw, w_in, conv_w, conv_b, w_r, b_r, w_i, b_i, lam, w_out):
    proj = norm_matmul(h, nw, w_in.astype(BF16), F32)
    y = lru_core(proj, conv_w, conv_b, w_r, b_r, w_i, b_i, lam)
    return out_proj_residual(y, w_out.astype(BF16), h)


def kernel(x, meta_tokens, rel_bias, norm_mix_w, norm_mlp_w, final_norm_w, dn_w_in, dn_conv_w, dn_a_log, dn_dt_bias, dn_norm_w, dn_w_out, da_w_in, da_lam_q1, da_lam_k1, da_lam_q2, da_lam_k2, da_subln_w, da_w_out, lru_w_in, lru_conv_w, lru_conv_b, lru_w_rgate, lru_b_rgate, lru_w_igate, lru_b_igate, lru_lambda, lru_w_out, mlp_w1, mlp_w2):
    b = x.shape[0]
    depth = norm_mix_w.shape[0]
    h = jnp.concatenate([
        jnp.zeros((b, FRONT_PAD, D_MODEL), x.dtype),
        jnp.broadcast_to(meta_tokens[None].astype(x.dtype), (b, N_META, D_MODEL)),
        x,
    ], axis=1)
    for layer in range(depth):
        kind = layer % N_MIXERS
        slot = layer // N_MIXERS
        if kind == 0:
            h = gated_deltanet_layer(h, norm_mix_w[layer], dn_w_in[slot], dn_conv_w[slot],
                                     dn_a_log[slot], dn_dt_bias[slot], dn_norm_w[slot],
                                     dn_w_out[slot])
        elif kind == 1:
            lambda_init = 0.8 - 0.6 * math.exp(-0.3 * layer)
            h = diff_attention_layer(h, norm_mix_w[layer], da_w_in[slot], da_lam_q1[slot],
                                     da_lam_k1[slot], da_lam_q2[slot], da_lam_k2[slot],
                                     da_subln_w[slot], da_w_out[slot], rel_bias, lambda_init)
        else:
            h = rglru_layer(h, norm_mix_w[layer], lru_w_in[slot], lru_conv_w[slot],
                            lru_conv_b[slot], lru_w_rgate[slot], lru_b_rgate[slot],
                            lru_w_igate[slot], lru_b_igate[slot], lru_lambda[slot],
                            lru_w_out[slot])
        final_w = final_norm_w if layer == depth - 1 else None
        h = mlp_residual(h, norm_mlp_w[layer], mlp_w1[layer].astype(BF16),
                         mlp_w2[layer].astype(BF16), final_w)
    return h[:, FRONT_PAD + N_META:]
```

```python
import functools
import math

import jax
import jax.numpy as jnp
from jax import lax
from jax.experimental import pallas as pl
from jax.experimental.pallas import tpu as pltpu

F32 = jnp.float32
BF16 = jnp.bfloat16

D_MODEL = 1024
N_META = 16
QBLOCK = 128
FRONT_PAD = QBLOCK - N_META
N_MIXERS = 3
EPS = 1e-6
CHUNK = 64
CHUNK_SHIFT = 6

DN_HEADS = 8
DN_HEAD = 128
DN_CONV = 4
DN_QKV = 3 * DN_HEADS * DN_HEAD
DN_MAIN = DN_QKV + DN_HEADS * DN_HEAD
DN_SMALL = 128
DN_CHUNK = 128
INV_BASE_LOG = 4

DA_HEADS = 8
DA_HEAD = 64
N_BUCKETS = 32
MAX_DISTANCE = 128
NEG_INF = -1e30
M_SLAB = 16

LRU_WIDTH = 1024
LRU_BLOCKS = 4
LRU_BLOCK = LRU_WIDTH // LRU_BLOCKS
LRU_CONV = 4
LRU_C = 8.0

D_FF = 4 * D_MODEL

V7X_VMEM_LIMIT_BYTES = 56 * 1024 * 1024


def _params(*semantics):
    return pltpu.CompilerParams(dimension_semantics=semantics,
                                vmem_limit_bytes=V7X_VMEM_LIMIT_BYTES)


def _pick(n, candidates):
    for c in candidates:
        if n % c == 0:
            return c
    raise ValueError(f"no tile for {n} in {candidates}")


def _dot(a, b):
    return jnp.dot(a, b, preferred_element_type=F32)


def _dot_nt(a, b):
    return lax.dot_general(a, b, (((1,), (1,)), ((), ())), preferred_element_type=F32)


def _split2(a):
    hi = a.astype(BF16)
    lo = (a - hi.astype(F32)).astype(BF16)
    return hi, lo


def _dot3(a, b):
    ah, al = _split2(a)
    bh, bl = _split2(b)
    return _dot(ah, bh) + (_dot(ah, bl) + _dot(al, bh))


def _rmsnorm_rows(x, w):
    return x * lax.rsqrt(jnp.mean(x * x, axis=-1, keepdims=True) + EPS) * w


def _keep_rows(tile_index, rows):
    pos = tile_index * rows + lax.broadcasted_iota(jnp.int32, (rows, 1), 0)
    return pos >= FRONT_PAD


def _norm_matmul_kernel(h_ref, nw_ref, w_ref, o_ref, u_ref):
    @pl.when(pl.program_id(2) == 0)
    def _():
        u_ref[...] = _rmsnorm_rows(h_ref[0], nw_ref[...]).astype(BF16)

    o_ref[0] = _dot(u_ref[...], w_ref[...]).astype(o_ref.dtype)


def norm_matmul(h, nw, w, out_dtype):
    b, L, d = h.shape
    n = w.shape[1]
    tm = _pick(L, (1040, 640, 320, 128))
    tn = _pick(n, (1408, 1024, 512, 128))
    return pl.pallas_call(
        _norm_matmul_kernel,
        grid=(b, L // tm, n // tn),
        in_specs=[
            pl.BlockSpec((1, tm, d), lambda bi, i, j: (bi, i, 0)),
            pl.BlockSpec((1, d), lambda bi, i, j: (0, 0)),
            pl.BlockSpec((d, tn), lambda bi, i, j: (0, j)),
        ],
        out_specs=pl.BlockSpec((1, tm, tn), lambda bi, i, j: (bi, i, j)),
        out_shape=jax.ShapeDtypeStruct((b, L, n), out_dtype),
        scratch_shapes=[pltpu.VMEM((tm, d), BF16)],
        compiler_params=_params("arbitrary", "arbitrary", "arbitrary"),
        name="norm_matmul",
    )(h, nw.reshape(1, d), w)


def _out_proj_kernel(y_ref, w_ref, h_ref, o_ref):
    tm = y_ref.shape[1]
    y = _dot(y_ref[0].astype(BF16), w_ref[...])
    keep = _keep_rows(pl.program_id(1), tm)
    o_ref[0] = h_ref[0] + jnp.where(keep, y, 0.0)


def out_proj_residual(y, w, h):
    b, L, d = h.shape
    k = y.shape[-1]
    tm = _pick(L, (1040, 640, 320, 128))
    return pl.pallas_call(
        _out_proj_kernel,
        grid=(b, L // tm),
        in_specs=[
            pl.BlockSpec((1, tm, k), lambda bi, i: (bi, i, 0)),
            pl.BlockSpec((k, d), lambda bi, i: (0, 0)),
            pl.BlockSpec((1, tm, d), lambda bi, i: (bi, i, 0)),
        ],
        out_specs=pl.BlockSpec((1, tm, d), lambda bi, i: (bi, i, 0)),
        out_shape=jax.ShapeDtypeStruct((b, L, d), F32),
        compiler_params=_params("arbitrary", "arbitrary"),
        name="out_proj_residual",
    )(y, w, h)


def _mlp_kernel(h_ref, nw_ref, w1_ref, w2_ref, fw_ref, o_ref, u_ref, acc_ref, *, final_norm):
    f = pl.program_id(2)
    tm = h_ref.shape[1]

    @pl.when(f == 0)
    def _():
        u_ref[...] = _rmsnorm_rows(h_ref[0], nw_ref[...]).astype(BF16)
        acc_ref[...] = jnp.zeros_like(acc_ref)

    a = _dot(u_ref[...], w1_ref[...])
    a = jnp.square(jnp.maximum(a, 0.0)).astype(BF16)
    acc_ref[...] += _dot(a, w2_ref[...])

    @pl.when(f == pl.num_programs(2) - 1)
    def _():
        keep = _keep_rows(pl.program_id(1), tm)
        hn = h_ref[0] + jnp.where(keep, acc_ref[...], 0.0)
        if final_norm:
            hn = _rmsnorm_rows(hn, fw_ref[...])
        o_ref[0] = hn


def mlp_residual(h, nw, w1, w2, final_w=None):
    b, L, d = h.shape
    ff = w1.shape[1]
    tm = _pick(L, (1040, 640, 320, 128))
    tf = _pick(ff, (512, 128))
    final_norm = final_w is not None
    fw = (final_w if final_norm else nw).reshape(1, d)
    return pl.pallas_call(
        functools.partial(_mlp_kernel, final_norm=final_norm),
        grid=(b, L // tm, ff // tf),
        in_specs=[
            pl.BlockSpec((1, tm, d), lambda bi, i, f: (bi, i, 0)),
            pl.BlockSpec((1, d), lambda bi, i, f: (0, 0)),
            pl.BlockSpec((d, tf), lambda bi, i, f: (0, f)),
            pl.BlockSpec((tf, d), lambda bi, i, f: (f, 0)),
            pl.BlockSpec((1, d), lambda bi, i, f: (0, 0)),
        ],
        out_specs=pl.BlockSpec((1, tm, d), lambda bi, i, f: (bi, i, 0)),
        out_shape=jax.ShapeDtypeStruct((b, L, d), F32),
        scratch_shapes=[pltpu.VMEM((tm, d), BF16), pltpu.VMEM((tm, d), F32)],
        compiler_params=_params("arbitrary", "arbitrary", "arbitrary"),
        name="mlp_residual",
    )(h, nw.reshape(1, d), w1, w2, fw)


def _causal_conv_rows(xs_ref, cur, halo, first_tile, w, width):
    tl = cur.shape[0]
    xs_ref[0:8, :] = jnp.where(first_tile, 0.0, halo)
    xs_ref[8:, :] = cur
    acc = None
    for j in range(width):
        off = 8 - (width - 1) + j
        term = xs_ref[off:off + tl, :] * w[j:j + 1, :]
        acc = term if acc is None else acc + term
    return acc


def _dn_prep_kernel(x_ref, halo_ref, cw_ref, o_ref, xs_ref):
    i = pl.program_id(1)
    tl = x_ref.shape[1]
    y = _causal_conv_rows(xs_ref, x_ref[0], halo_ref[0], i == 0, cw_ref[...], DN_CONV)
    y = y * jax.nn.sigmoid(y)
    keep = _keep_rows(i, tl)
    for g in range(3 * DN_HEADS):
        cols = slice(g * DN_HEAD, (g + 1) * DN_HEAD)
        yg = y[:, cols]
        if g < 2 * DN_HEADS:
            yg = yg * lax.rsqrt(jnp.sum(yg * yg, axis=-1, keepdims=True) + EPS)
        if g < DN_HEADS:
            yg = yg * (DN_HEAD ** -0.5)
        else:
            yg = jnp.where(keep, yg, 0.0)
        o_ref[0, :, cols] = yg


def dn_prep(proj, conv_w):
    b, L, _ = proj.shape
    tl = _pick(L, (320, 128))
    return pl.pallas_call(
        _dn_prep_kernel,
        grid=(b, L // tl),
        in_specs=[
            pl.BlockSpec((1, tl, DN_QKV), lambda bi, i: (bi, i, 0)),
            pl.BlockSpec((1, 8, DN_QKV), lambda bi, i: (bi, jnp.maximum(i * (tl // 8) - 1, 0), 0)),
            pl.BlockSpec((DN_CONV, DN_QKV), lambda bi, i: (0, 0)),
        ],
        out_specs=pl.BlockSpec((1, tl, DN_QKV), lambda bi, i: (bi, i, 0)),
        out_shape=jax.ShapeDtypeStruct((b, L, DN_QKV), F32),
        scratch_shapes=[pltpu.VMEM((tl + 8, DN_QKV), F32)],
        compiler_params=_params("arbitrary", "arbitrary"),
        name="dn_prep",
    )(proj, proj, conv_w)


def _approx_unit_lower_inverses(a_list):
    n = a_list[0].shape[0]
    row = lax.broadcasted_iota(jnp.int32, (n, n), 0)
    col = lax.broadcasted_iota(jnp.int32, (n, n), 1)

    def same_block(log_size):
        return lax.shift_right_logical(row, log_size) == lax.shift_right_logical(col, log_size)

    log_size = INV_BASE_LOG
    in_diag = same_block(log_size)
    eye = (row == col).astype(F32)
    ad = [jnp.where(in_diag, a, 0.0) for a in a_list]
    t = [eye - x for x in ad]
    bk = [x.astype(BF16) for x in ad]
    for _ in range(log_size - 1):
        bk = [_dot(x, x).astype(BF16) for x in bk]
        t = [ti + _dot(ti.astype(BF16), x) for ti, x in zip(t, bk)]
    while (1 << log_size) < n:
        sel = same_block(log_size + 1) & jnp.logical_not(same_block(log_size))
        off = [jnp.where(sel, a, 0.0).astype(BF16) for a in a_list]
        t16 = [ti.astype(BF16) for ti in t]
        left = [_dot(ti, o).astype(BF16) for ti, o in zip(t16, off)]
        t = [ti - _dot(x, ti16) for ti, x, ti16 in zip(t, left, t16)]
        log_size += 1
    return t


def _dn_chunk_kernel(q_ref, k_ref, v_ref, z_ref, s_ref, alog_ref, dtb_ref, nw_ref,
                     o_ref, state_ref):
    c = pl.program_id(0)
    n = DN_CHUNK
    nb = q_ref.shape[0]
    streams = [(b, h) for b in range(nb) for h in range(DN_HEADS)]

    @pl.when(c == 0)
    def _():
        state_ref[...] = jnp.zeros_like(state_ref)

    keep = _keep_rows(c, n)
    row = lax.broadcasted_iota(jnp.int32, (n, n), 0)
    col = lax.broadcasted_iota(jnp.int32, (n, n), 1)
    incl = row >= col
    strict = row > col
    tri = incl.astype(BF16)

    beta, gc, gc_t = [], [], []
    for b in range(nb):
        small = s_ref[b]
        beta.append(jax.nn.sigmoid(small))
        sp = jnp.logaddexp(small + dtb_ref[...], 0.0)
        g = jnp.where(keep, -jnp.exp(alog_ref[...]) * sp, 0.0)
        g_hi = g.astype(BF16)
        r1 = g - g_hi.astype(F32)
        g_mid = r1.astype(BF16)
        g_lo = (r1 - g_mid.astype(F32)).astype(BF16)
        gcb = _dot(tri, g_hi) + (_dot(tri, g_mid) + _dot(tri, g_lo))
        gc.append(gcb)
        gc_t.append(gcb.T)

    def cols(h):
        return slice(h * DN_HEAD, (h + 1) * DN_HEAD)

    bcol = [beta[b][:, h:h + 1] for b, h in streams]
    gcol = [gc[b][:, DN_HEADS + h:DN_HEADS + h + 1] for b, h in streams]
    grow = [gc_t[b][DN_HEADS + h:DN_HEADS + h + 1, :] for b, h in streams]
    decay = [jnp.exp(jnp.where(incl, gi - gj, -jnp.inf)) for gi, gj in zip(gcol, grow)]
    k = [k_ref[b, :, cols(h)] for b, h in streams]
    k16 = [x.astype(BF16) for x in k]
    kb = [x * bc for x, bc in zip(k, bcol)]
    a = [jnp.where(strict, _dot_nt(x.astype(BF16), y) * dc, 0.0)
         for x, y, dc in zip(kb, k16, decay)]
    t16 = [x.astype(BF16) for x in _approx_unit_lower_inverses(a)]
    a_split = [_split2(x) for x in a]

    egc = [jnp.exp(x) for x in gcol]
    s = [state_ref[i] for i in range(len(streams))]
    s16 = [x.astype(BF16) for x in s]
    rhs = [v_ref[b, :, cols(h)] * bc - _dot((kbi * e).astype(BF16), si)
           for (b, h), bc, kbi, e, si in zip(streams, bcol, kb, egc, s16)]
    x0 = [_dot(ti, r.astype(BF16)) for ti, r in zip(t16, rhs)]
    resid = []
    for (ah, al), x, r in zip(a_split, x0, rhs):
        xh, xl = _split2(x)
        resid.append(r - x - (_dot(ah, xh) + (_dot(ah, xl) + _dot(al, xh))))
    v_new = [x + _dot(ti, r.astype(BF16)) for x, ti, r in zip(x0, t16, resid)]
    v16 = [x.astype(BF16) for x in v_new]

    q = [q_ref[b, :, cols(h)] for b, h in streams]
    attn = [(_dot_nt(x.astype(BF16), y) * dc).astype(BF16) for x, y, dc in zip(q, k16, decay)]
    o = [_dot((x * e).astype(BF16), si) + _dot(at, vi)
         for x, e, si, at, vi in zip(q, egc, s16, attn, v16)]
    g_last = [x[n - 1:n, :] for x in gcol]
    kdec = [(x * jnp.exp(gl - gi)).T.astype(BF16) for x, gl, gi in zip(k, g_last, gcol)]
    for i, (si, gl, kd, vi) in enumerate(zip(s, g_last, kdec, v16)):
        state_ref[i] = si * jnp.exp(gl) + _dot(kd, vi)
    for (b, h), oi in zip(streams, o):
        zh = z_ref[b, :, cols(h)]
        y = _rmsnorm_rows(oi, nw_ref[...]) * (zh * jax.nn.sigmoid(zh))
        o_ref[b, :, cols(h)] = y.astype(o_ref.dtype)


def dn_chunk(qkv, proj, a_log_row, dt_bias_row, norm_w):
    b, L, _ = qkv.shape
    n = DN_CHUNK
    hd = DN_HEADS * DN_HEAD
    return pl.pallas_call(
        _dn_chunk_kernel,
        grid=(L // n,),
        in_specs=[
            pl.BlockSpec((b, n, hd), lambda c: (0, c, 0)),
            pl.BlockSpec((b, n, hd), lambda c: (0, c, 1)),
            pl.BlockSpec((b, n, hd), lambda c: (0, c, 2)),
            pl.BlockSpec((b, n, hd), lambda c: (0, c, 3)),
            pl.BlockSpec((b, n, DN_SMALL), lambda c: (0, c, DN_MAIN // DN_SMALL)),
            pl.BlockSpec((1, DN_SMALL), lambda c: (0, 0)),
            pl.BlockSpec((1, DN_SMALL), lambda c: (0, 0)),
            pl.BlockSpec((1, DN_HEAD), lambda c: (0, 0)),
        ],
        out_specs=pl.BlockSpec((b, n, hd), lambda c: (0, c, 0)),
        out_shape=jax.ShapeDtypeStruct((b, L, hd), BF16),
        scratch_shapes=[pltpu.VMEM((b * DN_HEADS, DN_HEAD, DN_HEAD), F32)],
        compiler_params=_params("arbitrary"),
        name="dn_chunk",
    )(qkv, qkv, qkv, proj, proj, a_log_row, dt_bias_row, norm_w.reshape(1, DN_HEAD))


def gated_deltanet_layer(h, nw, w_in, conv_w, a_log, dt_bias, norm_w, w_out):
    d = h.shape[-1]
    w_pad = jnp.zeros((d, DN_SMALL - 2 * DN_HEADS), w_in.dtype)
    w_all = jnp.concatenate([w_in, w_pad], axis=1).astype(BF16)
    lane_pad = jnp.zeros((DN_SMALL - 2 * DN_HEADS,), F32)
    head_pad = jnp.zeros((DN_HEADS,), F32)
    a_log_row = jnp.concatenate([head_pad, a_log.astype(F32), lane_pad]).reshape(1, DN_SMALL)
    dt_bias_row = jnp.concatenate([head_pad, dt_bias.astype(F32), lane_pad]).reshape(1, DN_SMALL)
    proj = norm_matmul(h, nw, w_all, F32)
    qkv = dn_prep(proj, conv_w)
    o = dn_chunk(qkv, proj, a_log_row, dt_bias_row, norm_w)
    return out_proj_residual(o, w_out.astype(BF16), h)


def _t5_bucket(rel):
    nb = N_BUCKETS // 2
    ret = jnp.where(rel > 0, nb, 0)
    n = jnp.abs(rel)
    max_exact = nb // 2
    nf = jnp.maximum(n, 1).astype(F32)
    large = max_exact + (jnp.log(nf / max_exact) / math.log(MAX_DISTANCE / max_exact)
                         * (nb - max_exact)).astype(jnp.int32)
    large = jnp.minimum(large, nb - 1)
    return ret + jnp.where(n < max_exact, n, large)


def _bias_tile_kernel(tab_ref, o_ref):
    h = pl.program_id(0)
    which = pl.program_id(1)
    key = lax.broadcasted_iota(jnp.int32, (QBLOCK, QBLOCK), 0)
    query = lax.broadcasted_iota(jnp.int32, (QBLOCK, QBLOCK), 1)
    bucket = _t5_bucket(key - query - QBLOCK * which)
    acc = jnp.zeros((QBLOCK, QBLOCK), F32)
    for bkt in range(N_BUCKETS):
        acc = jnp.where(bucket == bkt, tab_ref[bkt, h], acc)
    o_ref[0, 0] = acc - tab_ref[N_BUCKETS // 2 - 1, h]


def bias_tiles(rel_bias):
    return pl.pallas_call(
        _bias_tile_kernel,
        grid=(DA_HEADS, 2),
        in_specs=[pl.BlockSpec(memory_space=pltpu.SMEM)],
        out_specs=pl.BlockSpec((1, 1, QBLOCK, QBLOCK), lambda h, w: (h, w, 0, 0)),
        out_shape=jax.ShapeDtypeStruct((DA_HEADS, 2, QBLOCK, QBLOCK), F32),
        compiler_params=_params("arbitrary", "arbitrary"),
        name="bias_tiles",
    )(rel_bias.astype(F32))


def _da_kernel(q_ref, k_ref, v_ref, bias_ref, lamv_ref, subw_ref, o_ref,
               kx_ref, vt_ref, qt_ref, m_ref, l_ref, acc_ref, *, tq, lambda_init):
    qi = pl.program_id(2)
    nsub = tq // QBLOCK
    hw = 2 * DA_HEAD
    n_blocks = k_ref.shape[1] // tq
    slab_rows = (slice(DA_HEAD, DA_HEAD + M_SLAB), slice(0, M_SLAB))

    @pl.when(qi == 0)
    def _():
        lane = lax.broadcasted_iota(jnp.int32, (1, hw), 1)

        def prep(t, carry):
            rows = pl.ds(pl.multiple_of(t * tq, tq), tq)
            k = k_ref[0, rows, :]
            one = jnp.ones_like(k)
            kx_ref[0, rows, :] = jnp.where(lane < DA_HEAD, k, one)
            kx_ref[1, rows, :] = jnp.where(lane >= DA_HEAD, k, one)
            vt_ref[t] = v_ref[0, rows, :].astype(F32).T.astype(BF16)
            return carry

        lax.fori_loop(0, n_blocks, prep, 0)

    feat = lax.broadcasted_iota(jnp.int32, (hw, 1), 0)
    q_t = q_ref[0].astype(F32).T * (DA_HEAD ** -0.5)
    qt_ref[0] = jnp.where(feat < DA_HEAD, q_t, 0.0).astype(BF16)
    qt_ref[1] = jnp.where(feat >= DA_HEAD, q_t, 0.0).astype(BF16)
    m_ref[...] = jnp.full(m_ref.shape, NEG_INF, F32)
    l_ref[...] = jnp.zeros_like(l_ref)
    acc_ref[...] = jnp.zeros_like(acc_ref)

    def near_bias(which_block):
        rows = []
        for kk in range(nsub):
            tiles = []
            for qq in range(nsub):
                dist = qq - kk + nsub * which_block
                if dist == 0:
                    tiles.append(bias_ref[0, 0])
                elif dist == 1:
                    tiles.append(bias_ref[0, 1])
                else:
                    tiles.append(jnp.zeros((QBLOCK, QBLOCK), F32))
            rows.append(jnp.concatenate(tiles, axis=1))
        return jnp.concatenate(rows, axis=0)

    def step(kb, diag=False, sub=False, first=False):
        start = pl.multiple_of(kb * tq, tq)
        vt = vt_ref[kb]
        extra = None
        if diag:
            extra = near_bias(0)
        elif sub:
            extra = near_bias(1)
        mask = None
        if diag:
            key = lax.broadcasted_iota(jnp.int32, (tq, tq), 0)
            query = lax.broadcasted_iota(jnp.int32, (tq, tq), 1)
            mask = (lax.shift_right_logical(key, CHUNK_SHIFT)
                    <= lax.shift_right_logical(query, CHUNK_SHIFT))
        if first:
            valid = lax.broadcasted_iota(jnp.int32, (tq, 1), 0) >= FRONT_PAD
            mask = valid if mask is None else (mask & valid)
        part = lax.broadcasted_iota(jnp.int32, (M_SLAB, 1), 0)
        for c in range(2):
            kx = kx_ref[c, pl.ds(start, tq), :]
            qt_ref[c, slab_rows[c], :] = jnp.zeros((M_SLAB, tq), BF16)
            s = _dot(kx, qt_ref[c])
            if extra is not None:
                s = s + extra
            if mask is not None:
                s = jnp.where(mask, s, NEG_INF)
            m_old = m_ref[c]
            m_new = jnp.maximum(m_old, jnp.max(s, axis=0, keepdims=True))
            neg = -m_new
            hi = neg.astype(BF16).astype(F32)
            mid = (neg - hi).astype(BF16).astype(F32)
            lo = (neg - hi) - mid
            slab = jnp.where(part == 0, hi, jnp.where(part == 1, mid,
                                                      jnp.where(part == 2, lo, 0.0)))
            qt_ref[c, slab_rows[c], :] = slab.astype(BF16)
            x = _dot(kx, qt_ref[c])
            if extra is not None:
                x = x + extra
            if mask is not None:
                x = jnp.where(mask, x, NEG_INF - m_new)
            p = jnp.exp(x)
            alpha = jnp.exp(m_old - m_new)
            l_ref[c] = alpha * l_ref[c] + jnp.sum(p, axis=0, keepdims=True)
            acc_ref[c] = alpha * acc_ref[c] + _dot(vt, p.astype(BF16))
            m_ref[c] = m_new

    @pl.when(qi == 0)
    def _():
        step(0, diag=True, first=True)

    @pl.when(qi == 1)
    def _():
        step(0, sub=True, first=True)
        step(1, diag=True)

    @pl.when(qi >= 2)
    def _():
        step(0, first=True)

        def body(kb, carry):
            step(kb)
            return carry

        lax.fori_loop(1, qi - 1, body, 0)
        step(qi - 1, sub=True)
        step(qi, diag=True)

    lamv = lamv_ref[...]
    lam = (jnp.exp(jnp.sum(lamv[0:1] * lamv[1:2], axis=-1, keepdims=True))
           - jnp.exp(jnp.sum(lamv[2:3] * lamv[3:4], axis=-1, keepdims=True)) + lambda_init)
    o_t = acc_ref[0] * (1.0 / l_ref[0]) - lam * (acc_ref[1] * (1.0 / l_ref[1]))
    o = _rmsnorm_rows(o_t.T, subw_ref[...]) * (1.0 - lambda_init)
    o_ref[0] = o.astype(o_ref.dtype)


def diff_attention_core(proj, bias, lamv, subln_w, lambda_init):
    b, L, _ = proj.shape
    tq = _pick(L, (640, 128))
    hw = 2 * DA_HEAD
    return pl.pallas_call(
        functools.partial(_da_kernel, tq=tq, lambda_init=lambda_init),
        grid=(b, DA_HEADS, L // tq),
        in_specs=[
            pl.BlockSpec((1, tq, hw), lambda bi, h, i: (bi, i, h)),
            pl.BlockSpec((1, L, hw), lambda bi, h, i: (bi, 0, DA_HEADS + h)),
            pl.BlockSpec((1, L, hw), lambda bi, h, i: (bi, 0, 2 * DA_HEADS + h)),
            pl.BlockSpec((1, 2, QBLOCK, QBLOCK), lambda bi, h, i: (h, 0, 0, 0)),
            pl.BlockSpec((4, DA_HEAD), lambda bi, h, i: (0, 0)),
            pl.BlockSpec((1, hw), lambda bi, h, i: (0, 0)),
        ],
        out_specs=pl.BlockSpec((1, tq, hw), lambda bi, h, i: (bi, i, h)),
        out_shape=jax.ShapeDtypeStruct((b, L, DA_HEADS * hw), BF16),
        scratch_shapes=[pltpu.VMEM((2, L, hw), BF16), pltpu.VMEM((L // tq, hw, tq), BF16),
                        pltpu.VMEM((2, hw, tq), BF16), pltpu.VMEM((2, 1, tq), F32),
                        pltpu.VMEM((2, 1, tq), F32), pltpu.VMEM((2, hw, tq), F32)],
        compiler_params=_params("arbitrary", "arbitrary", "arbitrary"),
        name="diff_attention",
    )(proj, proj, proj, bias, lamv, subln_w.reshape(1, hw))


def diff_attention_layer(h, nw, w_in, lam_q1, lam_k1, lam_q2, lam_k2, subln_w, w_out,
                         rel_bias, lambda_init):
    proj = norm_matmul(h, nw, w_in.astype(BF16), BF16)
    bias = bias_tiles(rel_bias)
    lamv = jnp.stack([lam_q1, lam_k1, lam_q2, lam_k2]).astype(F32)
    o = diff_attention_core(proj, bias, lamv, subln_w, lambda_init)
    return out_proj_residual(o, w_out.astype(BF16), h)


def _lru_kernel(gate_ref, x_ref, halo_ref, cw_ref, cb_ref, wr_ref, br_ref, wi_ref, bi_ref,
                lam_ref, o_ref, xs_ref, a_ref, b_ref, h_ref):
    i = pl.program_id(1)
    tl = x_ref.shape[1]

    @pl.when(i == 0)
    def _():
        h_ref[...] = jnp.zeros_like(h_ref)

    xr = _causal_conv_rows(xs_ref, x_ref[0], halo_ref[0], i == 0, cw_ref[...], LRU_CONV)
    xr = jnp.where(_keep_rows(i, tl), xr + cb_ref[...], 0.0)
    neg_sp = -LRU_C * jnp.logaddexp(-lam_ref[...], 0.0)
    sub = jnp.bitwise_and(lax.broadcasted_iota(jnp.int32, (tl, 1), 0), 7)
    for g in range(LRU_BLOCKS):
        cols = slice(g * LRU_BLOCK, (g + 1) * LRU_BLOCK)
        xg = xr[:, cols]
        x16 = xg.astype(BF16)
        r = jax.nn.sigmoid(_dot(x16, wr_ref[g]) + br_ref[:, cols])
        ig = jax.nn.sigmoid(_dot(x16, wi_ref[g]) + bi_ref[:, cols])
        log_a = r * neg_sp[:, cols]
        a = jnp.exp(log_a)
        inp = jnp.sqrt(jnp.maximum(-jnp.tanh(log_a) * (a * a + 1.0), 0.0)) * (ig * xg)
        for s in (1, 2, 4):
            a_sh = pltpu.roll(a, s, 0)
            b_sh = pltpu.roll(inp, s, 0)
            use = sub >= s
            inp = jnp.where(use, a * b_sh + inp, inp)
            a = jnp.where(use, a * a_sh, a)
        a_ref[:, cols] = a
        b_ref[:, cols] = inp

    def body(t, hprev):
        rows = pl.ds(pl.multiple_of(t * 8, 8), 8)
        hs = b_ref[rows, :] + a_ref[rows, :] * hprev
        b_ref[rows, :] = hs
        return hs[7:8, :]

    h_ref[...] = lax.fori_loop(0, tl // 8, body, h_ref[...])
    gate = gate_ref[0]
    gelu = 0.5 * gate * (1.0 + jnp.tanh(math.sqrt(2.0 / math.pi)
                                        * (gate + 0.044715 * (gate * gate * gate))))
    o_ref[0] = (b_ref[...] * gelu).astype(o_ref.dtype)


def lru_core(proj, conv_w, conv_b, w_r, b_r, w_i, b_i, lam):
    b, L, _ = proj.shape
    tl = _pick(L, (640, 320, 128))
    wd = LRU_WIDTH
    row = lambda a: a.astype(F32).reshape(1, wd)
    return pl.pallas_call(
        _lru_kernel,
        grid=(b, L // tl),
        in_specs=[
            pl.BlockSpec((1, tl, wd), lambda bi, i: (bi, i, 0)),
            pl.BlockSpec((1, tl, wd), lambda bi, i: (bi, i, 1)),
            pl.BlockSpec((1, 8, wd), lambda bi, i: (bi, jnp.maximum(i * (tl // 8) - 1, 0), 1)),
            pl.BlockSpec((LRU_CONV, wd), lambda bi, i: (0, 0)),
            pl.BlockSpec((1, wd), lambda bi, i: (0, 0)),
            pl.BlockSpec((LRU_BLOCKS, LRU_BLOCK, LRU_BLOCK), lambda bi, i: (0, 0, 0)),
            pl.BlockSpec((1, wd), lambda bi, i: (0, 0)),
            pl.BlockSpec((LRU_BLOCKS, LRU_BLOCK, LRU_BLOCK), lambda bi, i: (0, 0, 0)),
            pl.BlockSpec((1, wd), lambda bi, i: (0, 0)),
            pl.BlockSpec((1, wd), lambda bi, i: (0, 0)),
        ],
        out_specs=pl.BlockSpec((1, tl, wd), lambda bi, i: (bi, i, 0)),
        out_shape=jax.ShapeDtypeStruct((b, L, wd), BF16),
        scratch_shapes=[pltpu.VMEM((tl + 8, wd), F32), pltpu.VMEM((tl, wd), F32),
                        pltpu.VMEM((tl, wd), F32), pltpu.VMEM((1, wd), F32)],
        compiler_params=_params("arbitrary", "arbitrary"),
        name="rglru",
    )(proj, proj, proj, conv_w, row(conv_b), w_r.astype(BF16), row(b_r), w_i.astype(BF16),
      row(b_i), row(lam))


def rglru_layer(h, nw, w_in, conv_w, conv_b, w_r, b_r, w_i, b_i, lam, w_out):
    proj = norm_matmul(h, nw, w_in.astype(BF16), F32)
    y = lru_core(proj, conv_w, conv_b, w_r, b_r, w_i, b_i, lam)
    return out_proj_residual(y, w_out.astype(BF16), h)


def kernel(x, meta_tokens, rel_bias, norm_mix_w, norm_mlp_w, final_norm_w, dn_w_in, dn_conv_w, dn_a_log, dn_dt_bias, dn_norm_w, dn_w_out, da_w_in, da_lam_q1, da_lam_k1, da_lam_q2, da_lam_k2, da_subln_w, da_w_out, lru_w_in, lru_conv_w, lru_conv_b, lru_w_rgate, lru_b_rgate, lru_w_igate, lru_b_igate, lru_lambda, lru_w_out, mlp_w1, mlp_w2):
    b = x.shape[0]
    depth = norm_mix_w.shape[0]
    h = jnp.concatenate([
        jnp.zeros((b, FRONT_PAD, D_MODEL), x.dtype),
        jnp.broadcast_to(meta_tokens[None].astype(x.dtype), (b, N_META, D_MODEL)),
        x,
    ], axis=1)
    for layer in range(depth):
        kind = layer % N_MIXERS
        slot = layer // N_MIXERS
        if kind == 0:
            h = gated_deltanet_layer(h, norm_mix_w[layer], dn_w_in[slot], dn_conv_w[slot],
                                     dn_a_log[slot], dn_dt_bias[slot], dn_norm_w[slot],
                                     dn_w_out[slot])
        elif kind == 1:
            lambda_init = 0.8 - 0.6 * math.exp(-0.3 * layer)
            h = diff_attention_layer(h, norm_mix_w[layer], da_w_in[slot], da_lam_q1[slot],
                                     da_lam_k1[slot], da_lam_q2[slot], da_lam_k2[slot],
                                     da_subln_w[slot], da_w_out[slot], rel_bias, lambda_init)
        else:
            h = rglru_layer(h, norm_mix_w[layer], lru_w_in[slot], lru_conv_w[slot],
                            lru_conv_b[slot], lru_w_rgate[slot], lru_b_rgate[slot],
                            lru_w_igate[slot], lru_b_igate[slot], lru_lambda[slot],
                            lru_w_out[slot])
        final_w = final_norm_w if layer == depth - 1 else None
        h = mlp_residual(h, norm_mlp_w[layer], mlp_w1[layer].astype(BF16),
                         mlp_w2[layer].astype(BF16), final_w)
    return h[:, FRONT_PAD + N_META:]
```

```python
import functools
import math

import jax
import jax.numpy as jnp
from jax import lax
from jax.experimental import pallas as pl
from jax.experimental.pallas import tpu as pltpu

F32 = jnp.float32
BF16 = jnp.bfloat16

D_MODEL = 1024
N_META = 16
QBLOCK = 128
FRONT_PAD = QBLOCK - N_META
N_MIXERS = 3
EPS = 1e-6
CHUNK = 64
CHUNK_SHIFT = 6

DN_HEADS = 8
DN_HEAD = 128
DN_CONV = 4
DN_QKV = 3 * DN_HEADS * DN_HEAD
DN_MAIN = DN_QKV + DN_HEADS * DN_HEAD
DN_SMALL = 128
DN_CHUNK = 128
INV_BASE_LOG = 4

DA_HEADS = 8
DA_HEAD = 64
N_BUCKETS = 32
MAX_DISTANCE = 128
NEG_INF = -1e30
LOG2E = math.log2(math.e)
DA_Q_SCALE = DA_HEAD ** -0.5 * LOG2E

LRU_WIDTH = 1024
LRU_BLOCKS = 4
LRU_BLOCK = LRU_WIDTH // LRU_BLOCKS
LRU_CONV = 4
LRU_C = 8.0

D_FF = 4 * D_MODEL

V7X_VMEM_LIMIT_BYTES = 56 * 1024 * 1024


def _params(*semantics):
    return pltpu.CompilerParams(dimension_semantics=semantics,
                                vmem_limit_bytes=V7X_VMEM_LIMIT_BYTES)


def _pick(n, candidates):
    for c in candidates:
        if n % c == 0:
            return c
    raise ValueError(f"no tile for {n} in {candidates}")


def _dot(a, b):
    return jnp.dot(a, b, preferred_element_type=F32)


def _dot_nt(a, b):
    return lax.dot_general(a, b, (((1,), (1,)), ((), ())), preferred_element_type=F32)


def _split2(a):
    hi = a.astype(BF16)
    lo = (a - hi.astype(F32)).astype(BF16)
    return hi, lo


def _dot3(a, b):
    ah, al = _split2(a)
    bh, bl = _split2(b)
    return _dot(ah, bh) + (_dot(ah, bl) + _dot(al, bh))


def _rmsnorm_rows(x, w):
    return x * lax.rsqrt(jnp.mean(x * x, axis=-1, keepdims=True) + EPS) * w


def _keep_rows(tile_index, rows):
    pos = tile_index * rows + lax.broadcasted_iota(jnp.int32, (rows, 1), 0)
    return pos >= FRONT_PAD


def _norm_matmul_kernel(h_ref, nw_ref, w_ref, *rest, scaled):
    cs_ref, o_ref, u_ref = rest if scaled else (None,) + rest

    @pl.when(pl.program_id(2) == 0)
    def _():
        u_ref[...] = _rmsnorm_rows(h_ref[0], nw_ref[...]).astype(BF16)

    y = _dot(u_ref[...], w_ref[...])
    if scaled:
        y = y * cs_ref[...]
    o_ref[0] = y.astype(o_ref.dtype)


def norm_matmul(h, nw, w, out_dtype, col_scale=None):
    b, L, d = h.shape
    n = w.shape[1]
    tm = _pick(L, (1040, 640, 320, 128))
    tn = _pick(n, (1408, 1024, 512, 128))
    scaled = col_scale is not None
    in_specs = [
        pl.BlockSpec((1, tm, d), lambda bi, i, j: (bi, i, 0)),
        pl.BlockSpec((1, d), lambda bi, i, j: (0, 0)),
        pl.BlockSpec((d, tn), lambda bi, i, j: (0, j)),
    ]
    args = [h, nw.reshape(1, d), w]
    if scaled:
        in_specs.append(pl.BlockSpec((1, tn), lambda bi, i, j: (0, j)))
        args.append(col_scale.astype(F32).reshape(1, n))
    return pl.pallas_call(
        functools.partial(_norm_matmul_kernel, scaled=scaled),
        grid=(b, L // tm, n // tn),
        in_specs=in_specs,
        out_specs=pl.BlockSpec((1, tm, tn), lambda bi, i, j: (bi, i, j)),
        out_shape=jax.ShapeDtypeStruct((b, L, n), out_dtype),
        scratch_shapes=[pltpu.VMEM((tm, d), BF16)],
        compiler_params=_params("arbitrary", "arbitrary", "arbitrary"),
        name="norm_matmul",
    )(*args)


def _out_proj_kernel(y_ref, w_ref, h_ref, o_ref):
    tm = y_ref.shape[1]
    y = _dot(y_ref[0].astype(BF16), w_ref[...])
    keep = _keep_rows(pl.program_id(1), tm)
    o_ref[0] = h_ref[0] + jnp.where(keep, y, 0.0)


def out_proj_residual(y, w, h):
    b, L, d = h.shape
    k = y.shape[-1]
    tm = _pick(L, (1040, 640, 320, 128))
    return pl.pallas_call(
        _out_proj_kernel,
        grid=(b, L // tm),
        in_specs=[
            pl.BlockSpec((1, tm, k), lambda bi, i: (bi, i, 0)),
            pl.BlockSpec((k, d), lambda bi, i: (0, 0)),
            pl.BlockSpec((1, tm, d), lambda bi, i: (bi, i, 0)),
        ],
        out_specs=pl.BlockSpec((1, tm, d), lambda bi, i: (bi, i, 0)),
        out_shape=jax.ShapeDtypeStruct((b, L, d), F32),
        compiler_params=_params("arbitrary", "arbitrary"),
        name="out_proj_residual",
    )(y, w, h)


def _mlp_kernel(h_ref, nw_ref, w1_ref, w2_ref, fw_ref, o_ref, u_ref, acc_ref, *, final_norm):
    f = pl.program_id(2)
    tm = h_ref.shape[1]

    @pl.when(f == 0)
    def _():
        u_ref[...] = _rmsnorm_rows(h_ref[0], nw_ref[...]).astype(BF16)
        acc_ref[...] = jnp.zeros_like(acc_ref)

    a = _dot(u_ref[...], w1_ref[...])
    a = jnp.square(jnp.maximum(a, 0.0)).astype(BF16)
    acc_ref[...] += _dot(a, w2_ref[...])

    @pl.when(f == pl.num_programs(2) - 1)
    def _():
        keep = _keep_rows(pl.program_id(1), tm)
        hn = h_ref[0] + jnp.where(keep, acc_ref[...], 0.0)
        if final_norm:
            hn = _rmsnorm_rows(hn, fw_ref[...])
        o_ref[0] = hn


def mlp_residual(h, nw, w1, w2, final_w=None):
    b, L, d = h.shape
    ff = w1.shape[1]
    tm = _pick(L, (1040, 640, 320, 128))
    tf = _pick(ff, (512, 128))
    final_norm = final_w is not None
    fw = (final_w if final_norm else nw).reshape(1, d)
    return pl.pallas_call(
        functools.partial(_mlp_kernel, final_norm=final_norm),
        grid=(b, L // tm, ff // tf),
        in_specs=[
            pl.BlockSpec((1, tm, d), lambda bi, i, f: (bi, i, 0)),
            pl.BlockSpec((1, d), lambda bi, i, f: (0, 0)),
            pl.BlockSpec((d, tf), lambda bi, i, f: (0, f)),
            pl.BlockSpec((tf, d), lambda bi, i, f: (f, 0)),
            pl.BlockSpec((1, d), lambda bi, i, f: (0, 0)),
        ],
        out_specs=pl.BlockSpec((1, tm, d), lambda bi, i, f: (bi, i, 0)),
        out_shape=jax.ShapeDtypeStruct((b, L, d), F32),
        scratch_shapes=[pltpu.VMEM((tm, d), BF16), pltpu.VMEM((tm, d), F32)],
        compiler_params=_params("arbitrary", "arbitrary", "arbitrary"),
        name="mlp_residual",
    )(h, nw.reshape(1, d), w1, w2, fw)


def _causal_conv_rows(xs_ref, cur, halo, first_tile, w, width):
    tl = cur.shape[0]
    xs_ref[0:8, :] = jnp.where(first_tile, 0.0, halo)
    xs_ref[8:, :] = cur
    acc = None
    for j in range(width):
        off = 8 - (width - 1) + j
        term = xs_ref[off:off + tl, :] * w[j:j + 1, :]
        acc = term if acc is None else acc + term
    return acc


def _dn_prep_kernel(x_ref, halo_ref, cw_ref, o_ref, xs_ref):
    i = pl.program_id(1)
    tl = x_ref.shape[1]
    y = _causal_conv_rows(xs_ref, x_ref[0], halo_ref[0], i == 0, cw_ref[...], DN_CONV)
    y = y * jax.nn.sigmoid(y)
    keep = _keep_rows(i, tl)
    for g in range(3 * DN_HEADS):
        cols = slice(g * DN_HEAD, (g + 1) * DN_HEAD)
        yg = y[:, cols]
        if g < 2 * DN_HEADS:
            yg = yg * lax.rsqrt(jnp.sum(yg * yg, axis=-1, keepdims=True) + EPS)
        if g < DN_HEADS:
            yg = yg * (DN_HEAD ** -0.5)
        else:
            yg = jnp.where(keep, yg, 0.0)
        o_ref[0, :, cols] = yg


def dn_prep(proj, conv_w):
    b, L, _ = proj.shape
    tl = _pick(L, (320, 128))
    return pl.pallas_call(
        _dn_prep_kernel,
        grid=(b, L // tl),
        in_specs=[
            pl.BlockSpec((1, tl, DN_QKV), lambda bi, i: (bi, i, 0)),
            pl.BlockSpec((1, 8, DN_QKV), lambda bi, i: (bi, jnp.maximum(i * (tl // 8) - 1, 0), 0)),
            pl.BlockSpec((DN_CONV, DN_QKV), lambda bi, i: (0, 0)),
        ],
        out_specs=pl.BlockSpec((1, tl, DN_QKV), lambda bi, i: (bi, i, 0)),
        out_shape=jax.ShapeDtypeStruct((b, L, DN_QKV), F32),
        scratch_shapes=[pltpu.VMEM((tl + 8, DN_QKV), F32)],
        compiler_params=_params("arbitrary", "arbitrary"),
        name="dn_prep",
    )(proj, proj, conv_w)


def _approx_unit_lower_inverses(a_list):
    n = a_list[0].shape[0]
    row = lax.broadcasted_iota(jnp.int32, (n, n), 0)
    col = lax.broadcasted_iota(jnp.int32, (n, n), 1)

    def same_block(log_size):
        return lax.shift_right_logical(row, log_size) == lax.shift_right_logical(col, log_size)

    log_size = INV_BASE_LOG
    in_diag = same_block(log_size)
    eye = (row == col).astype(F32)
    ad = [jnp.where(in_diag, a, 0.0) for a in a_list]
    t = [eye - x for x in ad]
    bk = [x.astype(BF16) for x in ad]
    for _ in range(log_size - 1):
        bk = [_dot(x, x).astype(BF16) for x in bk]
        t = [ti + _dot(ti.astype(BF16), x) for ti, x in zip(t, bk)]
    while (1 << log_size) < n:
        sel = same_block(log_size + 1) & jnp.logical_not(same_block(log_size))
        off = [jnp.where(sel, a, 0.0).astype(BF16) for a in a_list]
        t16 = [ti.astype(BF16) for ti in t]
        left = [_dot(ti, o).astype(BF16) for ti, o in zip(t16, off)]
        t = [ti - _dot(x, ti16) for ti, x, ti16 in zip(t, left, t16)]
        log_size += 1
    return t


def _dn_chunk_kernel(q_ref, k_ref, v_ref, z_ref, s_ref, alog_ref, dtb_ref, nw_ref,
                     o_ref, state_ref):
    c = pl.program_id(0)
    n = DN_CHUNK
    nb = q_ref.shape[0]
    streams = [(b, h) for b in range(nb) for h in range(DN_HEADS)]

    @pl.when(c == 0)
    def _():
        state_ref[...] = jnp.zeros_like(state_ref)

    keep = _keep_rows(c, n)
    row = lax.broadcasted_iota(jnp.int32, (n, n), 0)
    col = lax.broadcasted_iota(jnp.int32, (n, n), 1)
    incl = row >= col
    strict = row > col
    tri = incl.astype(BF16)

    beta, gc, gc_t = [], [], []
    for b in range(nb):
        small = s_ref[b]
        beta.append(jax.nn.sigmoid(small))
        sp = jnp.logaddexp(small + dtb_ref[...], 0.0)
        g = jnp.where(keep, -jnp.exp(alog_ref[...]) * sp, 0.0)
        g_hi = g.astype(BF16)
        r1 = g - g_hi.astype(F32)
        g_mid = r1.astype(BF16)
        g_lo = (r1 - g_mid.astype(F32)).astype(BF16)
        gcb = _dot(tri, g_hi) + (_dot(tri, g_mid) + _dot(tri, g_lo))
        gc.append(gcb)
        gc_t.append(gcb.T)

    def cols(h):
        return slice(h * DN_HEAD, (h + 1) * DN_HEAD)

    bcol = [beta[b][:, h:h + 1] for b, h in streams]
    gcol = [gc[b][:, DN_HEADS + h:DN_HEADS + h + 1] for b, h in streams]
    grow = [gc_t[b][DN_HEADS + h:DN_HEADS + h + 1, :] for b, h in streams]
    decay = [jnp.exp(jnp.where(incl, gi - gj, -jnp.inf)) for gi, gj in zip(gcol, grow)]
    k = [k_ref[b, :, cols(h)] for b, h in streams]
    k16 = [x.astype(BF16) for x in k]
    kb = [x * bc for x, bc in zip(k, bcol)]
    a = [jnp.where(strict, _dot_nt(x.astype(BF16), y) * dc, 0.0)
         for x, y, dc in zip(kb, k16, decay)]
    t16 = [x.astype(BF16) for x in _approx_unit_lower_inverses(a)]
    a_split = [_split2(x) for x in a]

    egc = [jnp.exp(x) for x in gcol]
    s = [state_ref[i] for i in range(len(streams))]
    s16 = [x.astype(BF16) for x in s]
    rhs = [v_ref[b, :, cols(h)] * bc - _dot((kbi * e).astype(BF16), si)
           for (b, h), bc, kbi, e, si in zip(streams, bcol, kb, egc, s16)]
    x0 = [_dot(ti, r.astype(BF16)) for ti, r in zip(t16, rhs)]
    resid = []
    for (ah, al), x, r in zip(a_split, x0, rhs):
        xh, xl = _split2(x)
        resid.append(r - x - (_dot(ah, xh) + (_dot(ah, xl) + _dot(al, xh))))
    v_new = [x + _dot(ti, r.astype(BF16)) for x, ti, r in zip(x0, t16, resid)]
    v16 = [x.astype(BF16) for x in v_new]

    q = [q_ref[b, :, cols(h)] for b, h in streams]
    attn = [(_dot_nt(x.astype(BF16), y) * dc).astype(BF16) for x, y, dc in zip(q, k16, decay)]
    o = [_dot((x * e).astype(BF16), si) + _dot(at, vi)
         for x, e, si, at, vi in zip(q, egc, s16, attn, v16)]
    g_last = [x[n - 1:n, :] for x in gcol]
    kdec = [(x * jnp.exp(gl - gi)).T.astype(BF16) for x, gl, gi in zip(k, g_last, gcol)]
    for i, (si, gl, kd, vi) in enumerate(zip(s, g_last, kdec, v16)):
        state_ref[i] = si * jnp.exp(gl) + _dot(kd, vi)
    for (b, h), oi in zip(streams, o):
        zh = z_ref[b, :, cols(h)]
        y = _rmsnorm_rows(oi, nw_ref[...]) * (zh * jax.nn.sigmoid(zh))
        o_ref[b, :, cols(h)] = y.astype(o_ref.dtype)


def dn_chunk(qkv, proj, a_log_row, dt_bias_row, norm_w):
    b, L, _ = qkv.shape
    n = DN_CHUNK
    hd = DN_HEADS * DN_HEAD
    return pl.pallas_call(
        _dn_chunk_kernel,
        grid=(L // n,),
        in_specs=[
            pl.BlockSpec((b, n, hd), lambda c: (0, c, 0)),
            pl.BlockSpec((b, n, hd), lambda c: (0, c, 1)),
            pl.BlockSpec((b, n, hd), lambda c: (0, c, 2)),
            pl.BlockSpec((b, n, hd), lambda c: (0, c, 3)),
            pl.BlockSpec((b, n, DN_SMALL), lambda c: (0, c, DN_MAIN // DN_SMALL)),
            pl.BlockSpec((1, DN_SMALL), lambda c: (0, 0)),
            pl.BlockSpec((1, DN_SMALL), lambda c: (0, 0)),
            pl.BlockSpec((1, DN_HEAD), lambda c: (0, 0)),
        ],
        out_specs=pl.BlockSpec((b, n, hd), lambda c: (0, c, 0)),
        out_shape=jax.ShapeDtypeStruct((b, L, hd), BF16),
        scratch_shapes=[pltpu.VMEM((b * DN_HEADS, DN_HEAD, DN_HEAD), F32)],
        compiler_params=_params("arbitrary"),
        name="dn_chunk",
    )(qkv, qkv, qkv, proj, proj, a_log_row, dt_bias_row, norm_w.reshape(1, DN_HEAD))


def gated_deltanet_layer(h, nw, w_in, conv_w, a_log, dt_bias, norm_w, w_out):
    d = h.shape[-1]
    w_pad = jnp.zeros((d, DN_SMALL - 2 * DN_HEADS), w_in.dtype)
    w_all = jnp.concatenate([w_in, w_pad], axis=1).astype(BF16)
    lane_pad = jnp.zeros((DN_SMALL - 2 * DN_HEADS,), F32)
    head_pad = jnp.zeros((DN_HEADS,), F32)
    a_log_row = jnp.concatenate([head_pad, a_log.astype(F32), lane_pad]).reshape(1, DN_SMALL)
    dt_bias_row = jnp.concatenate([head_pad, dt_bias.astype(F32), lane_pad]).reshape(1, DN_SMALL)
    proj = norm_matmul(h, nw, w_all, F32)
    qkv = dn_prep(proj, conv_w)
    o = dn_chunk(qkv, proj, a_log_row, dt_bias_row, norm_w)
    return out_proj_residual(o, w_out.astype(BF16), h)


def _t5_bucket(rel):
    nb = N_BUCKETS // 2
    ret = jnp.where(rel > 0, nb, 0)
    n = jnp.abs(rel)
    max_exact = nb // 2
    nf = jnp.maximum(n, 1).astype(F32)
    large = max_exact + (jnp.log(nf / max_exact) / math.log(MAX_DISTANCE / max_exact)
                         * (nb - max_exact)).astype(jnp.int32)
    large = jnp.minimum(large, nb - 1)
    return ret + jnp.where(n < max_exact, n, large)


def _bias_tile_kernel(tab_ref, o_ref):
    h = pl.program_id(0)
    which = pl.program_id(1)
    key = lax.broadcasted_iota(jnp.int32, (QBLOCK, QBLOCK), 0)
    query = lax.broadcasted_iota(jnp.int32, (QBLOCK, QBLOCK), 1)
    bucket = _t5_bucket(key - query - QBLOCK * which)
    acc = jnp.zeros((QBLOCK, QBLOCK), F32)
    for bkt in range(N_BUCKETS):
        acc = jnp.where(bucket == bkt, tab_ref[bkt, h], acc)
    o_ref[0, 0] = (acc - tab_ref[N_BUCKETS // 2 - 1, h]) * LOG2E


def bias_tiles(rel_bias):
    return pl.pallas_call(
        _bias_tile_kernel,
        grid=(DA_HEADS, 2),
        in_specs=[pl.BlockSpec(memory_space=pltpu.SMEM)],
        out_specs=pl.BlockSpec((1, 1, QBLOCK, QBLOCK), lambda h, w: (h, w, 0, 0)),
        out_shape=jax.ShapeDtypeStruct((DA_HEADS, 2, QBLOCK, QBLOCK), F32),
        compiler_params=_params("arbitrary", "arbitrary"),
        name="bias_tiles",
    )(rel_bias.astype(F32))


def _da_kernel(q_ref, k_ref, v_ref, bias_ref, lamv_ref, subw_ref, o_ref,
               vt_ref, qt_ref, s_ref, p_ref, mblk_ref, m_ref, l_ref, acc_ref,
               *, tq, lambda_init):
    qi = pl.program_id(2)
    nsub = tq // QBLOCK
    hw = 2 * DA_HEAD
    n_blocks = k_ref.shape[1] // tq

    @pl.when(qi == 0)
    def _():
        def prep(t, carry):
            rows = pl.ds(pl.multiple_of(t * tq, tq), tq)
            vt_ref[t] = v_ref[0, rows, :].astype(F32).T.astype(BF16)
            return carry

        lax.fori_loop(0, n_blocks, prep, 0)

    feat = lax.broadcasted_iota(jnp.int32, (hw, 1), 0)
    q_t = q_ref[0].astype(F32).T
    qt_ref[0] = jnp.where(feat < DA_HEAD, q_t, 0.0).astype(BF16)
    qt_ref[1] = jnp.where(feat >= DA_HEAD, q_t, 0.0).astype(BF16)
    m_ref[...] = jnp.full(m_ref.shape, NEG_INF, F32)
    l_ref[...] = jnp.zeros_like(l_ref)
    acc_ref[...] = jnp.zeros_like(acc_ref)

    def sub_rows(j):
        return slice(j * QBLOCK, (j + 1) * QBLOCK)

    def scores_sub(c, kb, j, diag, sub, first):
        rows = pl.ds(pl.multiple_of(kb * tq + j * QBLOCK, QBLOCK), QBLOCK)
        s = _dot(k_ref[0, rows, :], qt_ref[c])
        if diag or sub:
            tiles = []
            for qq in range(nsub):
                dist = qq - j + (nsub if sub else 0)
                if dist == 0:
                    tiles.append(bias_ref[0, 0])
                elif dist == 1:
                    tiles.append(bias_ref[0, 1])
                else:
                    tiles.append(jnp.zeros((QBLOCK, QBLOCK), F32))
            s = s + jnp.concatenate(tiles, axis=1)
        mask = None
        if diag:
            key = j * QBLOCK + lax.broadcasted_iota(jnp.int32, (QBLOCK, tq), 0)
            query = lax.broadcasted_iota(jnp.int32, (QBLOCK, tq), 1)
            mask = (lax.shift_right_logical(key, CHUNK_SHIFT)
                    <= lax.shift_right_logical(query, CHUNK_SHIFT))
        if first and j == 0:
            valid = lax.broadcasted_iota(jnp.int32, (QBLOCK, 1), 0) >= FRONT_PAD
            mask = valid if mask is None else (mask & valid)
        if mask is not None:
            s = jnp.where(mask, s, NEG_INF)
        s_ref[c, sub_rows(j), :] = s
        return jnp.max(s.reshape(QBLOCK // 8, 8, tq), axis=0)

    def stage(kb, nxt=None, diag=False, sub=False, first=False):
        for c in range(2):
            if kb is not None:
                m_old = m_ref[c]
                m_new = jnp.maximum(m_old, mblk_ref[c])
            lsum = None
            running = None
            for j in range(nsub):
                if kb is not None:
                    p = jnp.exp2(s_ref[c, sub_rows(j), :] - m_new)
                    lj = jnp.sum(p.reshape(QBLOCK // 8, 8, tq), axis=0)
                    lsum = lj if lsum is None else lsum + lj
                    p_ref[c, sub_rows(j), :] = p.astype(BF16)
                if nxt is not None:
                    mj = scores_sub(c, nxt, j, diag, sub, first)
                    running = mj if running is None else jnp.maximum(running, mj)
            if nxt is not None:
                mblk_ref[c] = jnp.max(running, axis=0, keepdims=True)
            if kb is not None:
                alpha = jnp.exp2(m_old - m_new)
                l_ref[c] = alpha * l_ref[c] + jnp.sum(lsum, axis=0, keepdims=True)
                acc_ref[c] = alpha * acc_ref[c] + _dot(vt_ref[kb], p_ref[c])
                m_ref[c] = m_new

    @pl.when(qi == 0)
    def _():
        stage(None, 0, diag=True, first=True)
        stage(0)

    @pl.when(qi == 1)
    def _():
        stage(None, 0, sub=True, first=True)
        stage(0, 1, diag=True)
        stage(1)

    @pl.when(qi >= 2)
    def _():
        stage(None, 0, first=True)

        def body(kb, carry):
            stage(kb, kb + 1)
            return carry

        lax.fori_loop(0, qi - 2, body, 0)
        stage(qi - 2, qi - 1, sub=True)
        stage(qi - 1, qi, diag=True)
        stage(qi)

    lamv = lamv_ref[...]
    lam = (jnp.exp(jnp.sum(lamv[0:1] * lamv[1:2], axis=-1, keepdims=True))
           - jnp.exp(jnp.sum(lamv[2:3] * lamv[3:4], axis=-1, keepdims=True)) + lambda_init)
    o_t = acc_ref[0] * (1.0 / l_ref[0]) - lam * (acc_ref[1] * (1.0 / l_ref[1]))
    o = _rmsnorm_rows(o_t.T, subw_ref[...]) * (1.0 - lambda_init)
    o_ref[0] = o.astype(o_ref.dtype)


def diff_attention_core(proj, bias, lamv, subln_w, lambda_init):
    b, L, _ = proj.shape
    tq = _pick(L, (640, 128))
    hw = 2 * DA_HEAD
    return pl.pallas_call(
        functools.partial(_da_kernel, tq=tq, lambda_init=lambda_init),
        grid=(b, DA_HEADS, L // tq),
        in_specs=[
            pl.BlockSpec((1, tq, hw), lambda bi, h, i: (bi, i, h)),
            pl.BlockSpec((1, L, hw), lambda bi, h, i: (bi, 0, DA_HEADS + h)),
            pl.BlockSpec((1, L, hw), lambda bi, h, i: (bi, 0, 2 * DA_HEADS + h)),
            pl.BlockSpec((1, 2, QBLOCK, QBLOCK), lambda bi, h, i: (h, 0, 0, 0)),
            pl.BlockSpec((4, DA_HEAD), lambda bi, h, i: (0, 0)),
            pl.BlockSpec((1, hw), lambda bi, h, i: (0, 0)),
        ],
        out_specs=pl.BlockSpec((1, tq, hw), lambda bi, h, i: (bi, i, h)),
        out_shape=jax.ShapeDtypeStruct((b, L, DA_HEADS * hw), BF16),
        scratch_shapes=[pltpu.VMEM((L // tq, hw, tq), BF16), pltpu.VMEM((2, hw, tq), BF16),
                        pltpu.VMEM((2, tq, tq), F32), pltpu.VMEM((2, tq, tq), BF16),
                        pltpu.VMEM((2, 1, tq), F32), pltpu.VMEM((2, 1, tq), F32),
                        pltpu.VMEM((2, 1, tq), F32), pltpu.VMEM((2, hw, tq), F32)],
        compiler_params=_params("arbitrary", "arbitrary", "arbitrary"),
        name="diff_attention",
    )(proj, proj, proj, bias, lamv, subln_w.reshape(1, hw))


def diff_attention_layer(h, nw, w_in, lam_q1, lam_k1, lam_q2, lam_k2, subln_w, w_out,
                         rel_bias, lambda_init):
    qk = DA_HEADS * 2 * DA_HEAD
    col_scale = jnp.concatenate([jnp.full((qk,), DA_Q_SCALE, F32),
                                 jnp.ones((w_in.shape[1] - qk,), F32)])
    proj = norm_matmul(h, nw, w_in.astype(BF16), BF16, col_scale)
    bias = bias_tiles(rel_bias)
    lamv = jnp.stack([lam_q1, lam_k1, lam_q2, lam_k2]).astype(F32)
    o = diff_attention_core(proj, bias, lamv, subln_w, lambda_init)
    return out_proj_residual(o, w_out.astype(BF16), h)


def _lru_kernel(gate_ref, x_ref, halo_ref, cw_ref, cb_ref, wr_ref, br_ref, wi_ref, bi_ref,
                lam_ref, o_ref, xs_ref, a_ref, b_ref, h_ref):
    i = pl.program_id(1)
    tl = x_ref.shape[1]

    @pl.when(i == 0)
    def _():
        h_ref[...] = jnp.zeros_like(h_ref)

    xr = _causal_conv_rows(xs_ref, x_ref[0], halo_ref[0], i == 0, cw_ref[...], LRU_CONV)
    xr = jnp.where(_keep_rows(i, tl), xr + cb_ref[...], 0.0)
    neg_sp = -LRU_C * jnp.logaddexp(-lam_ref[...], 0.0)
    sub = jnp.bitwise_and(lax.broadcasted_iota(jnp.int32, (tl, 1), 0), 7)
    for g in range(LRU_BLOCKS):
        cols = slice(g * LRU_BLOCK, (g + 1) * LRU_BLOCK)
        xg = xr[:, cols]
        x16 = xg.astype(BF16)
        r = jax.nn.sigmoid(_dot(x16, wr_ref[g]) + br_ref[:, cols])
        ig = jax.nn.sigmoid(_dot(x16, wi_ref[g]) + bi_ref[:, cols])
        log_a = r * neg_sp[:, cols]
        a = jnp.exp(log_a)
        inp = jnp.sqrt(jnp.maximum(-jnp.tanh(log_a) * (a * a + 1.0), 0.0)) * (ig * xg)
        for s in (1, 2, 4):
            a_sh = pltpu.roll(a, s, 0)
            b_sh = pltpu.roll(inp, s, 0)
            use = sub >= s
            inp = jnp.where(use, a * b_sh + inp, inp)
            a = jnp.where(use, a * a_sh, a)
        a_ref[:, cols] = a
        b_ref[:, cols] = inp

    def body(t, hprev):
        rows = pl.ds(pl.multiple_of(t * 8, 8), 8)
        hs = b_ref[rows, :] + a_ref[rows, :] * hprev
        b_ref[rows, :] = hs
        return hs[7:8, :]

    h_ref[...] = lax.fori_loop(0, tl // 8, body, h_ref[...])
    gate = gate_ref[0]
    gelu = 0.5 * gate * (1.0 + jnp.tanh(math.sqrt(2.0 / math.pi)
                                        * (gate + 0.044715 * (gate * gate * gate))))
    o_ref[0] = (b_ref[...] * gelu).astype(o_ref.dtype)


def lru_core(proj, conv_w, conv_b, w_r, b_r, w_i, b_i, lam):
    b, L, _ = proj.shape
    tl = _pick(L, (640, 320, 128))
    wd = LRU_WIDTH
    row = lambda a: a.astype(F32).reshape(1, wd)
    return pl.pallas_call(
        _lru_kernel,
        grid=(b, L // tl),
        in_specs=[
            pl.BlockSpec((1, tl, wd), lambda bi, i: (bi, i, 0)),
            pl.BlockSpec((1, tl, wd), lambda bi, i: (bi, i, 1)),
            pl.BlockSpec((1, 8, wd), lambda bi, i: (bi, jnp.maximum(i * (tl // 8) - 1, 0), 1)),
            pl.BlockSpec((LRU_CONV, wd), lambda bi, i: (0, 0)),
            pl.BlockSpec((1, wd), lambda bi, i: (0, 0)),
            pl.BlockSpec((LRU_BLOCKS, LRU_BLOCK, LRU_BLOCK), lambda bi, i: (0, 0, 0)),
            pl.BlockSpec((1, wd), lambda bi, i: (0, 0)),
            pl.BlockSpec((LRU_BLOCKS, LRU_BLOCK, LRU_BLOCK), lambda bi, i: (0, 0, 0)),
            pl.BlockSpec((1, wd), lambda bi, i: (0, 0)),
            pl.BlockSpec((1, wd), lambda bi, i: (0, 0)),
        ],
        out_specs=pl.BlockSpec((1, tl, wd), lambda bi, i: (bi, i, 0)),
        out_shape=jax.ShapeDtypeStruct((b, L, wd), BF16),
        scratch_shapes=[pltpu.VMEM((tl + 8, wd), F32), pltpu.VMEM((tl, wd), F32),
                        pltpu.VMEM((tl, wd), F32), pltpu.VMEM((1, wd), F32)],
        compiler_params=_params("arbitrary", "arbitrary"),
        name="rglru",
    )(proj, proj, proj, conv_w, row(conv_b), w_r.astype(BF16), row(b_r), w_i.astype(BF16),
      row(b_i), row(lam))


def rglru_layer(h, nw, w_in, conv_w, conv_b, w_r, b_r, w_i, b_i, lam, w_out):
    proj = norm_matmul(h, nw, w_in.astype(BF16), F32)
    y = lru_core(proj, conv_w, conv_b, w_r, b_r, w_i, b_i, lam)
    return out_proj_residual(y, w_out.astype(BF16), h)


def kernel(x, meta_tokens, rel_bias, norm_mix_w, norm_mlp_w, final_norm_w, dn_w_in, dn_conv_w, dn_a_log, dn_dt_bias, dn_norm_w, dn_w_out, da_w_in, da_lam_q1, da_lam_k1, da_lam_q2, da_lam_k2, da_subln_w, da_w_out, lru_w_in, lru_conv_w, lru_conv_b, lru_w_rgate, lru_b_rgate, lru_w_igate, lru_b_igate, lru_lambda, lru_w_out, mlp_w1, mlp_w2):
    b = x.shape[0]
    depth = norm_mix_w.shape[0]
    h = jnp.concatenate([
        jnp.zeros((b, FRONT_PAD, D_MODEL), x.dtype),
        jnp.broadcast_to(meta_tokens[None].astype(x.dtype), (b, N_META, D_MODEL)),
        x,
    ], axis=1)
    for layer in range(depth):
        kind = layer % N_MIXERS
        slot = layer // N_MIXERS
        if kind == 0:
            h = gated_deltanet_layer(h, norm_mix_w[layer], dn_w_in[slot], dn_conv_w[slot],
                                     dn_a_log[slot], dn_dt_bias[slot], dn_norm_w[slot],
                                     dn_w_out[slot])
        elif kind == 1:
            lambda_init = 0.8 - 0.6 * math.exp(-0.3 * layer)
            h = diff_attention_layer(h, norm_mix_w[layer], da_w_in[slot], da_lam_q1[slot],
                                     da_lam_k1[slot], da_lam_q2[slot], da_lam_k2[slot],
                                     da_subln_w[slot], da_w_out[slot], rel_bias, lambda_init)
        else:
            h = rglru_layer(h, norm_mix_w[layer], lru_w_in[slot], lru_conv_w[slot],
                            lru_conv_b[slot], lru_w_rgate[slot], lru_b_rgate[slot],
                            lru_w_igate[slot], lru_b_igate[slot], lru_lambda[slot],
                            lru_w_out[slot])
        final_w = final_norm_w if layer == depth - 1 else None
        h = mlp_residual(h, norm_mlp_w[layer], mlp_w1[layer].astype(BF16),
                         mlp_w2[layer].astype(BF16), final_w)
    return h[:, FRONT_PAD + N_META:]
```

```python
import functools
import math

import jax
import jax.numpy as jnp
from jax import lax
from jax.experimental import pallas as pl
from jax.experimental.pallas import tpu as pltpu

F32 = jnp.float32
BF16 = jnp.bfloat16

D_MODEL = 1024
N_META = 16
QBLOCK = 128
FRONT_PAD = QBLOCK - N_META
N_MIXERS = 3
EPS = 1e-6
CHUNK = 64
CHUNK_SHIFT = 6
CONV_HALO = 8

DN_HEADS = 8
DN_HEAD = 128
DN_CONV = 4
DN_QKV = 3 * DN_HEADS * DN_HEAD
DN_MAIN = DN_QKV + DN_HEADS * DN_HEAD
DN_SMALL = 128
DN_CHUNK = 128
INV_BASE_LOG = 4

DA_HEADS = 8
DA_HEAD = 64
N_BUCKETS = 32
MAX_DISTANCE = 128
NEG_INF = -1e30
LOG2E = math.log2(math.e)
DA_Q_SCALE = DA_HEAD ** -0.5 * LOG2E

LRU_WIDTH = 1024
LRU_BLOCKS = 4
LRU_BLOCK = LRU_WIDTH // LRU_BLOCKS
LRU_CONV = 4
LRU_C = 8.0

D_FF = 4 * D_MODEL

V7X_VMEM_LIMIT_BYTES = 56 * 1024 * 1024


def _params(*semantics):
    return pltpu.CompilerParams(dimension_semantics=semantics,
                                vmem_limit_bytes=V7X_VMEM_LIMIT_BYTES)


def _pick(n, candidates):
    for c in candidates:
        if n % c == 0:
            return c
    raise ValueError(f"no tile for {n} in {candidates}")


def _dot(a, b):
    return jnp.dot(a, b, preferred_element_type=F32)


def _dot_nt(a, b):
    return lax.dot_general(a, b, (((1,), (1,)), ((), ())), preferred_element_type=F32)


def _split2(a):
    hi = a.astype(BF16)
    lo = (a - hi.astype(F32)).astype(BF16)
    return hi, lo


def _dot3(a, b):
    ah, al = _split2(a)
    bh, bl = _split2(b)
    return _dot(ah, bh) + (_dot(ah, bl) + _dot(al, bh))


def _rmsnorm_rows(x, w):
    return x * lax.rsqrt(jnp.mean(x * x, axis=-1, keepdims=True) + EPS) * w


def _keep_rows(tile_index, rows):
    pos = tile_index * rows + lax.broadcasted_iota(jnp.int32, (rows, 1), 0)
    return pos >= FRONT_PAD


def _norm_matmul_kernel(h_ref, nw_ref, w_ref, *rest, scaled):
    cs_ref, o_ref, u_ref = rest if scaled else (None,) + rest

    @pl.when(pl.program_id(2) == 0)
    def _():
        u_ref[...] = _rmsnorm_rows(h_ref[0], nw_ref[...]).astype(BF16)

    y = _dot(u_ref[...], w_ref[...])
    if scaled:
        y = y * cs_ref[...]
    o_ref[0] = y.astype(o_ref.dtype)


def norm_matmul(h, nw, w, out_dtype, col_scale=None):
    b, L, d = h.shape
    w, slot = w
    n = w.shape[2]
    tm = _pick(L, (1040, 640, 320, 128))
    tn = _pick(n, (1408, 1024, 512, 128))
    scaled = col_scale is not None
    in_specs = [
        pl.BlockSpec((1, tm, d), lambda bi, i, j: (bi, i, 0)),
        pl.BlockSpec((1, d), lambda bi, i, j: (0, 0)),
        pl.BlockSpec((None, d, tn), lambda bi, i, j: (slot, 0, j)),
    ]
    args = [h, nw.reshape(1, d), w]
    if scaled:
        in_specs.append(pl.BlockSpec((1, tn), lambda bi, i, j: (0, j)))
        args.append(col_scale.astype(F32).reshape(1, n))
    return pl.pallas_call(
        functools.partial(_norm_matmul_kernel, scaled=scaled),
        grid=(b, L // tm, n // tn),
        in_specs=in_specs,
        out_specs=pl.BlockSpec((1, tm, tn), lambda bi, i, j: (bi, i, j)),
        out_shape=jax.ShapeDtypeStruct((b, L, n), out_dtype),
        scratch_shapes=[pltpu.VMEM((tm, d), BF16)],
        compiler_params=_params("arbitrary", "arbitrary", "arbitrary"),
        name="norm_matmul",
    )(*args)


def _out_proj_kernel(y_ref, w_ref, h_ref, o_ref):
    tm = y_ref.shape[1]
    y = _dot(y_ref[0].astype(BF16), w_ref[...])
    keep = _keep_rows(pl.program_id(1), tm)
    o_ref[0] = h_ref[0] + jnp.where(keep, y, 0.0)


def out_proj_residual(y, w, h):
    b, L, d = h.shape
    w, slot = w
    k = y.shape[-1]
    tm = _pick(L, (1040, 640, 320, 128))
    return pl.pallas_call(
        _out_proj_kernel,
        grid=(b, L // tm),
        in_specs=[
            pl.BlockSpec((1, tm, k), lambda bi, i: (bi, i, 0)),
            pl.BlockSpec((None, k, d), lambda bi, i: (slot, 0, 0)),
            pl.BlockSpec((1, tm, d), lambda bi, i: (bi, i, 0)),
        ],
        out_specs=pl.BlockSpec((1, tm, d), lambda bi, i: (bi, i, 0)),
        out_shape=jax.ShapeDtypeStruct((b, L, d), F32),
        compiler_params=_params("arbitrary", "arbitrary"),
        name="out_proj_residual",
    )(y, w, h)


def _mlp_kernel(h_ref, nw_ref, w1_ref, w2_ref, fw_ref, o_ref, u_ref, acc_ref, *, final_norm):
    f = pl.program_id(2)
    tm = h_ref.shape[1]

    @pl.when(f == 0)
    def _():
        u_ref[...] = _rmsnorm_rows(h_ref[0], nw_ref[...]).astype(BF16)
        acc_ref[...] = jnp.zeros_like(acc_ref)

    a = _dot(u_ref[...], w1_ref[...])
    a = jnp.square(jnp.maximum(a, 0.0)).astype(BF16)
    acc_ref[...] += _dot(a, w2_ref[...])

    @pl.when(f == pl.num_programs(2) - 1)
    def _():
        keep = _keep_rows(pl.program_id(1), tm)
        hn = h_ref[0] + jnp.where(keep, acc_ref[...], 0.0)
        if final_norm:
            hn = _rmsnorm_rows(hn, fw_ref[...])
        o_ref[0] = hn


def mlp_residual(h, nw, w1, w2, final_w=None):
    b, L, d = h.shape
    (w1, layer), (w2, _) = w1, w2
    ff = w1.shape[2]
    tm = _pick(L, (1040, 640, 320, 128))
    tf = _pick(ff, (512, 128))
    final_norm = final_w is not None
    fw = (final_w if final_norm else nw).reshape(1, d)
    return pl.pallas_call(
        functools.partial(_mlp_kernel, final_norm=final_norm),
        grid=(b, L // tm, ff // tf),
        in_specs=[
            pl.BlockSpec((1, tm, d), lambda bi, i, f: (bi, i, 0)),
            pl.BlockSpec((1, d), lambda bi, i, f: (0, 0)),
            pl.BlockSpec((None, d, tf), lambda bi, i, f: (layer, 0, f)),
            pl.BlockSpec((None, tf, d), lambda bi, i, f: (layer, f, 0)),
            pl.BlockSpec((1, d), lambda bi, i, f: (0, 0)),
        ],
        out_specs=pl.BlockSpec((1, tm, d), lambda bi, i, f: (bi, i, 0)),
        out_shape=jax.ShapeDtypeStruct((b, L, d), F32),
        scratch_shapes=[pltpu.VMEM((tm, d), BF16), pltpu.VMEM((tm, d), F32)],
        compiler_params=_params("arbitrary", "arbitrary", "arbitrary"),
        name="mlp_residual",
    )(h, nw.reshape(1, d), w1, w2, fw)


def _causal_conv_rows(xs_ref, cur, halo, first_tile, w, width):
    tl = cur.shape[0]
    xs_ref[0:CONV_HALO, :] = jnp.where(first_tile, 0.0, halo)
    xs_ref[CONV_HALO:, :] = cur
    acc = None
    for j in range(width):
        off = CONV_HALO - (width - 1) + j
        term = xs_ref[off:off + tl, :] * w[j:j + 1, :]
        acc = term if acc is None else acc + term
    return acc


def _approx_unit_lower_inverses(a_list):
    n = a_list[0].shape[0]
    row = lax.broadcasted_iota(jnp.int32, (n, n), 0)
    col = lax.broadcasted_iota(jnp.int32, (n, n), 1)

    def same_block(log_size):
        return lax.shift_right_logical(row, log_size) == lax.shift_right_logical(col, log_size)

    log_size = INV_BASE_LOG
    in_diag = same_block(log_size)
    eye = (row == col).astype(F32)
    ad = [jnp.where(in_diag, a, 0.0) for a in a_list]
    t = [eye - x for x in ad]
    bk = [x.astype(BF16) for x in ad]
    for _ in range(log_size - 1):
        bk = [_dot(x, x).astype(BF16) for x in bk]
        t = [ti + _dot(ti.astype(BF16), x) for ti, x in zip(t, bk)]
    while (1 << log_size) < n:
        sel = same_block(log_size + 1) & jnp.logical_not(same_block(log_size))
        off = [jnp.where(sel, a, 0.0).astype(BF16) for a in a_list]
        t16 = [ti.astype(BF16) for ti in t]
        left = [_dot(ti, o).astype(BF16) for ti, o in zip(t16, off)]
        t = [ti - _dot(x, ti16) for ti, x, ti16 in zip(t, left, t16)]
        log_size += 1
    return t


def _dn_chunk_kernel(q_ref, k_ref, v_ref, z_ref, s_ref, cw_ref, alog_ref, dtb_ref, nw_ref,
                     o_ref, state_ref, xs_ref):
    c = pl.program_id(0)
    n = DN_CHUNK
    nb = q_ref.shape[0]
    hd = DN_HEADS * DN_HEAD
    streams = [(b, h) for b in range(nb) for h in range(DN_HEADS)]

    @pl.when(c == 0)
    def _():
        state_ref[...] = jnp.zeros_like(state_ref)
        xs_ref[:, 0:CONV_HALO, :] = jnp.zeros((nb, CONV_HALO, 3 * hd), F32)

    for b in range(nb):
        xs_ref[b, CONV_HALO:, 0:hd] = q_ref[b]
        xs_ref[b, CONV_HALO:, hd:2 * hd] = k_ref[b]
        xs_ref[b, CONV_HALO:, 2 * hd:] = v_ref[b]

    def conv_act(b, g):
        lanes = slice(g * DN_HEAD, (g + 1) * DN_HEAD)
        x = xs_ref[b, :, lanes]
        acc = x * cw_ref[DN_CONV - 1:DN_CONV, lanes]
        for j in range(DN_CONV - 2, -1, -1):
            x = pltpu.roll(x, 1, 0)
            acc = acc + x * cw_ref[j:j + 1, lanes]
        acc = acc[CONV_HALO:, :]
        return acc * jax.nn.sigmoid(acc)

    def unit_rows(x):
        return x * lax.rsqrt(jnp.sum(x * x, axis=-1, keepdims=True) + EPS)

    keep = _keep_rows(c, n)
    row = lax.broadcasted_iota(jnp.int32, (n, n), 0)
    col = lax.broadcasted_iota(jnp.int32, (n, n), 1)
    incl = row >= col
    strict = row > col
    tri = incl.astype(BF16)

    beta, gc, gc_t = [], [], []
    for b in range(nb):
        small = s_ref[b]
        beta.append(jax.nn.sigmoid(small))
        sp = jnp.logaddexp(small + dtb_ref[...], 0.0)
        g = jnp.where(keep, -jnp.exp(alog_ref[...]) * sp, 0.0)
        g_hi = g.astype(BF16)
        r1 = g - g_hi.astype(F32)
        g_mid = r1.astype(BF16)
        g_lo = (r1 - g_mid.astype(F32)).astype(BF16)
        gcb = _dot(tri, g_hi) + (_dot(tri, g_mid) + _dot(tri, g_lo))
        gc.append(gcb)
        gc_t.append(gcb.T)

    def cols(h):
        return slice(h * DN_HEAD, (h + 1) * DN_HEAD)

    bcol = [beta[b][:, h:h + 1] for b, h in streams]
    gcol = [gc[b][:, DN_HEADS + h:DN_HEADS + h + 1] for b, h in streams]
    grow = [gc_t[b][DN_HEADS + h:DN_HEADS + h + 1, :] for b, h in streams]
    decay = [jnp.exp(jnp.where(incl, gi - gj, -jnp.inf)) for gi, gj in zip(gcol, grow)]
    k = [jnp.where(keep, unit_rows(conv_act(b, DN_HEADS + h)), 0.0) for b, h in streams]
    k16 = [x.astype(BF16) for x in k]
    kb = [x * bc for x, bc in zip(k, bcol)]
    a = [jnp.where(strict, _dot_nt(x.astype(BF16), y) * dc, 0.0)
         for x, y, dc in zip(kb, k16, decay)]
    t16 = [x.astype(BF16) for x in _approx_unit_lower_inverses(a)]
    a_split = [_split2(x) for x in a]

    egc = [jnp.exp(x) for x in gcol]
    s = [state_ref[i] for i in range(len(streams))]
    s16 = [x.astype(BF16) for x in s]
    rhs = [jnp.where(keep, conv_act(b, 2 * DN_HEADS + h), 0.0) * bc
           - _dot((kbi * e).astype(BF16), si)
           for (b, h), bc, kbi, e, si in zip(streams, bcol, kb, egc, s16)]
    x0 = [_dot(ti, r.astype(BF16)) for ti, r in zip(t16, rhs)]
    resid = []
    for (ah, al), x, r in zip(a_split, x0, rhs):
        xh, xl = _split2(x)
        resid.append(r - x - (_dot(ah, xh) + (_dot(ah, xl) + _dot(al, xh))))
    v_new = [x + _dot(ti, r.astype(BF16)) for x, ti, r in zip(x0, t16, resid)]
    v16 = [x.astype(BF16) for x in v_new]

    q = [unit_rows(conv_act(b, h)) * (DN_HEAD ** -0.5) for b, h in streams]
    attn = [(_dot_nt(x.astype(BF16), y) * dc).astype(BF16) for x, y, dc in zip(q, k16, decay)]
    o = [_dot((x * e).astype(BF16), si) + _dot(at, vi)
         for x, e, si, at, vi in zip(q, egc, s16, attn, v16)]
    g_last = [x[n - 1:n, :] for x in gcol]
    kdec = [(x * jnp.exp(gl - gi)).T.astype(BF16) for x, gl, gi in zip(k, g_last, gcol)]
    for i, (si, gl, kd, vi) in enumerate(zip(s, g_last, kdec, v16)):
        state_ref[i] = si * jnp.exp(gl) + _dot(kd, vi)
    for (b, h), oi in zip(streams, o):
        zh = z_ref[b, :, cols(h)]
        y = _rmsnorm_rows(oi, nw_ref[...]) * (zh * jax.nn.sigmoid(zh))
        o_ref[b, :, cols(h)] = y.astype(o_ref.dtype)
    xs_ref[:, 0:CONV_HALO, :] = xs_ref[:, n:n + CONV_HALO, :]


def dn_chunk(proj, conv_w, a_log_row, dt_bias_row, norm_w):
    b, L, _ = proj.shape
    n = DN_CHUNK
    hd = DN_HEADS * DN_HEAD
    return pl.pallas_call(
        _dn_chunk_kernel,
        grid=(L // n,),
        in_specs=[
            pl.BlockSpec((b, n, hd), lambda c: (0, c, 0)),
            pl.BlockSpec((b, n, hd), lambda c: (0, c, 1)),
            pl.BlockSpec((b, n, hd), lambda c: (0, c, 2)),
            pl.BlockSpec((b, n, hd), lambda c: (0, c, 3)),
            pl.BlockSpec((b, n, DN_SMALL), lambda c: (0, c, DN_MAIN // DN_SMALL)),
            pl.BlockSpec((DN_CONV, DN_QKV), lambda c: (0, 0)),
            pl.BlockSpec((1, DN_SMALL), lambda c: (0, 0)),
            pl.BlockSpec((1, DN_SMALL), lambda c: (0, 0)),
            pl.BlockSpec((1, DN_HEAD), lambda c: (0, 0)),
        ],
        out_specs=pl.BlockSpec((b, n, hd), lambda c: (0, c, 0)),
        out_shape=jax.ShapeDtypeStruct((b, L, hd), BF16),
        scratch_shapes=[pltpu.VMEM((b * DN_HEADS, DN_HEAD, DN_HEAD), F32),
                        pltpu.VMEM((b, CONV_HALO + n, DN_QKV), F32)],
        compiler_params=_params("arbitrary"),
        name="dn_chunk",
    )(proj, proj, proj, proj, proj, conv_w, a_log_row, dt_bias_row,
      norm_w.reshape(1, DN_HEAD))


def gated_deltanet_layer(h, nw, w_all, conv_w, a_log, dt_bias, norm_w, w_out):
    lane_pad = jnp.zeros((DN_SMALL - 2 * DN_HEADS,), F32)
    head_pad = jnp.zeros((DN_HEADS,), F32)
    a_log_row = jnp.concatenate([head_pad, a_log.astype(F32), lane_pad]).reshape(1, DN_SMALL)
    dt_bias_row = jnp.concatenate([head_pad, dt_bias.astype(F32), lane_pad]).reshape(1, DN_SMALL)
    proj = norm_matmul(h, nw, w_all, F32)
    o = dn_chunk(proj, conv_w, a_log_row, dt_bias_row, norm_w)
    return out_proj_residual(o, w_out, h)


def _t5_bucket(rel):
    nb = N_BUCKETS // 2
    ret = jnp.where(rel > 0, nb, 0)
    n = jnp.abs(rel)
    max_exact = nb // 2
    nf = jnp.maximum(n, 1).astype(F32)
    large = max_exact + (jnp.log(nf / max_exact) / math.log(MAX_DISTANCE / max_exact)
                         * (nb - max_exact)).astype(jnp.int32)
    large = jnp.minimum(large, nb - 1)
    return ret + jnp.where(n < max_exact, n, large)


def _bias_tile_kernel(tab_ref, o_ref):
    h = pl.program_id(0)
    which = pl.program_id(1)
    key = lax.broadcasted_iota(jnp.int32, (QBLOCK, QBLOCK), 0)
    query = lax.broadcasted_iota(jnp.int32, (QBLOCK, QBLOCK), 1)
    bucket = _t5_bucket(key - query - QBLOCK * which)
    acc = jnp.zeros((QBLOCK, QBLOCK), F32)
    for bkt in range(N_BUCKETS):
        acc = jnp.where(bucket == bkt, tab_ref[bkt, h], acc)
    o_ref[0, 0] = (acc - tab_ref[N_BUCKETS // 2 - 1, h]) * LOG2E


def bias_tiles(rel_bias):
    return pl.pallas_call(
        _bias_tile_kernel,
        grid=(DA_HEADS, 2),
        in_specs=[pl.BlockSpec(memory_space=pltpu.SMEM)],
        out_specs=pl.BlockSpec((1, 1, QBLOCK, QBLOCK), lambda h, w: (h, w, 0, 0)),
        out_shape=jax.ShapeDtypeStruct((DA_HEADS, 2, QBLOCK, QBLOCK), F32),
        compiler_params=_params("arbitrary", "arbitrary"),
        name="bias_tiles",
    )(rel_bias.astype(F32))


def _da_kernel(q_ref, k_ref, v_ref, bias_ref, lamv_ref, subw_ref, o_ref,
               vt_ref, qt_ref, s_ref, p_ref, mblk_ref, m_ref, l_ref, acc_ref,
               *, tq, lambda_init):
    qi = pl.program_id(2)
    nsub = tq // QBLOCK
    hw = 2 * DA_HEAD
    n_blocks = k_ref.shape[1] // tq

    @pl.when(qi == 0)
    def _():
        def prep(t, carry):
            rows = pl.ds(pl.multiple_of(t * tq, tq), tq)
            vt_ref[t] = v_ref[0, rows, :].astype(F32).T.astype(BF16)
            return carry

        lax.fori_loop(0, n_blocks, prep, 0)

    feat = lax.broadcasted_iota(jnp.int32, (hw, 1), 0)
    q_t = q_ref[0].astype(F32).T
    qt_ref[:, :tq] = jnp.where(feat < DA_HEAD, q_t, 0.0).astype(BF16)
    qt_ref[:, tq:] = jnp.where(feat >= DA_HEAD, q_t, 0.0).astype(BF16)
    m_ref[...] = jnp.full(m_ref.shape, NEG_INF, F32)
    l_ref[...] = jnp.zeros_like(l_ref)
    acc_ref[...] = jnp.zeros_like(acc_ref)

    def sub_rows(j):
        return slice(j * QBLOCK, (j + 1) * QBLOCK)

    def near_terms(s_half, j, diag, sub):
        tiles = [s_half[:, sub_rows(qq)] for qq in range(nsub)]
        if sub and j == nsub - 1:
            tiles[0] = tiles[0] + bias_ref[0, 1]
        if diag:
            key = lax.broadcasted_iota(jnp.int32, (QBLOCK, QBLOCK), 0)
            query = lax.broadcasted_iota(jnp.int32, (QBLOCK, QBLOCK), 1)
            allowed = (lax.shift_right_logical(key, CHUNK_SHIFT)
                       <= lax.shift_right_logical(query, CHUNK_SHIFT))
            for qq in range(j):
                tiles[qq] = jnp.full((QBLOCK, QBLOCK), NEG_INF, F32)
            tiles[j] = jnp.where(allowed, tiles[j] + bias_ref[0, 0], NEG_INF)
            if j + 1 < nsub:
                tiles[j + 1] = tiles[j + 1] + bias_ref[0, 1]
        return jnp.concatenate(tiles, axis=1)

    def scores_sub(kb, j, diag, sub, first):
        rows = pl.ds(pl.multiple_of(kb * tq + j * QBLOCK, QBLOCK), QBLOCK)
        s = _dot(k_ref[0, rows, :], qt_ref[...])
        if diag or (sub and j == nsub - 1):
            s = jnp.concatenate([near_terms(s[:, :tq], j, diag, sub),
                                 near_terms(s[:, tq:], j, diag, sub)], axis=1)
        if first and j == 0:
            valid = lax.broadcasted_iota(jnp.int32, (QBLOCK, 1), 0) >= FRONT_PAD
            s = jnp.where(valid, s, NEG_INF)
        s_ref[sub_rows(j), :] = s
        return jnp.max(s.reshape(QBLOCK // 8, 8, 2 * tq), axis=0)

    def stage(kb, nxt=None, diag=False, sub=False, first=False):
        if kb is not None:
            m_old = m_ref[...]
            m_new = jnp.maximum(m_old, mblk_ref[...])
        lsum = None
        running = None
        for j in range(nsub):
            if kb is not None:
                p = jnp.exp2(s_ref[sub_rows(j), :] - m_new)
                lj = jnp.sum(p.reshape(QBLOCK // 8, 8, 2 * tq), axis=0)
                lsum = lj if lsum is None else lsum + lj
                p_ref[sub_rows(j), :] = p.astype(BF16)
            if nxt is not None:
                mj = scores_sub(nxt, j, diag, sub, first)
                running = mj if running is None else jnp.maximum(running, mj)
        if nxt is not None:
            mblk_ref[...] = jnp.max(running, axis=0, keepdims=True)
        if kb is not None:
            alpha = jnp.exp2(m_old - m_new)
            l_ref[...] = alpha * l_ref[...] + jnp.sum(lsum, axis=0, keepdims=True)
            acc_ref[...] = alpha * acc_ref[...] + _dot(vt_ref[kb], p_ref[...])
            m_ref[...] = m_new

    def region(pred, *args, **kwargs):
        @pl.when(pred)
        def _():
            stage(*args, **kwargs)

    region(qi == 0, None, 0, diag=True, first=True)
    region(qi < 1, 0)

    region(qi == 1, None, 0, sub=True, first=True)
    region(qi - 1 == 0, 0, 1, diag=True)
    region(qi + 1 == 2, 1)

    region(qi >= 2, None, 0, first=True)

    @pl.when(qi > 1)
    def _():
        def body(kb, carry):
            stage(kb, kb + 1)
            return carry

        lax.fori_loop(0, qi - 2, body, 0)

    region(qi >= 2, qi - 2, qi - 1, sub=True)
    region(qi > 1, qi - 1, qi, diag=True)
    region(qi - 2 >= 0, qi)

    lamv = lamv_ref[...]
    lam = (jnp.exp(jnp.sum(lamv[0:1] * lamv[1:2], axis=-1, keepdims=True))
           - jnp.exp(jnp.sum(lamv[2:3] * lamv[3:4], axis=-1, keepdims=True)) + lambda_init)
    on = acc_ref[...] * (1.0 / l_ref[...])
    o_t = on[:, :tq] - lam * on[:, tq:]
    o = _rmsnorm_rows(o_t.T, subw_ref[...]) * (1.0 - lambda_init)
    o_ref[0] = o.astype(o_ref.dtype)


def diff_attention_core(proj, bias, lamv, subln_w, lambda_init):
    b, L, _ = proj.shape
    tq = _pick(L, (640, 128))
    hw = 2 * DA_HEAD
    return pl.pallas_call(
        functools.partial(_da_kernel, tq=tq, lambda_init=lambda_init),
        grid=(b, DA_HEADS, L // tq),
        in_specs=[
            pl.BlockSpec((1, tq, hw), lambda bi, h, i: (bi, i, h)),
            pl.BlockSpec((1, L, hw), lambda bi, h, i: (bi, 0, DA_HEADS + h)),
            pl.BlockSpec((1, L, hw), lambda bi, h, i: (bi, 0, 2 * DA_HEADS + h)),
            pl.BlockSpec((1, 2, QBLOCK, QBLOCK), lambda bi, h, i: (h, 0, 0, 0)),
            pl.BlockSpec((4, DA_HEAD), lambda bi, h, i: (0, 0)),
            pl.BlockSpec((1, hw), lambda bi, h, i: (0, 0)),
        ],
        out_specs=pl.BlockSpec((1, tq, hw), lambda bi, h, i: (bi, i, h)),
        out_shape=jax.ShapeDtypeStruct((b, L, DA_HEADS * hw), BF16),
        scratch_shapes=[pltpu.VMEM((L // tq, hw, tq), BF16), pltpu.VMEM((hw, 2 * tq), BF16),
                        pltpu.VMEM((tq, 2 * tq), F32), pltpu.VMEM((tq, 2 * tq), BF16),
                        pltpu.VMEM((1, 2 * tq), F32), pltpu.VMEM((1, 2 * tq), F32),
                        pltpu.VMEM((1, 2 * tq), F32), pltpu.VMEM((hw, 2 * tq), F32)],
        compiler_params=_params("arbitrary", "arbitrary", "arbitrary"),
        name="diff_attention",
    )(proj, proj, proj, bias, lamv, subln_w.reshape(1, hw))


def diff_attention_layer(h, nw, w_in, lam_q1, lam_k1, lam_q2, lam_k2, subln_w, w_out,
                         rel_bias, lambda_init):
    qk = DA_HEADS * 2 * DA_HEAD
    col_scale = jnp.concatenate([jnp.full((qk,), DA_Q_SCALE, F32),
                                 jnp.ones((w_in[0].shape[2] - qk,), F32)])
    proj = norm_matmul(h, nw, w_in, BF16, col_scale)
    bias = bias_tiles(rel_bias)
    lamv = jnp.stack([lam_q1, lam_k1, lam_q2, lam_k2]).astype(F32)
    o = diff_attention_core(proj, bias, lamv, subln_w, lambda_init)
    return out_proj_residual(o, w_out, h)


def _lru_kernel(gate_ref, x_ref, halo_ref, cw_ref, cb_ref, wr_ref, br_ref, wi_ref, bi_ref,
                lam_ref, o_ref, xs_ref, a_ref, b_ref, h_ref):
    i = pl.program_id(1)
    tl = x_ref.shape[1]

    @pl.when(i == 0)
    def _():
        h_ref[...] = jnp.zeros_like(h_ref)

    xr = _causal_conv_rows(xs_ref, x_ref[0], halo_ref[0], i == 0, cw_ref[...], LRU_CONV)
    xr = jnp.where(_keep_rows(i, tl), xr + cb_ref[...], 0.0)
    neg_sp = -LRU_C * jnp.logaddexp(-lam_ref[...], 0.0)
    sub = jnp.bitwise_and(lax.broadcasted_iota(jnp.int32, (tl, 1), 0), 7)
    for g in range(LRU_BLOCKS):
        cols = slice(g * LRU_BLOCK, (g + 1) * LRU_BLOCK)
        xg = xr[:, cols]
        x16 = xg.astype(BF16)
        r = jax.nn.sigmoid(_dot(x16, wr_ref[g]) + br_ref[:, cols])
        ig = jax.nn.sigmoid(_dot(x16, wi_ref[g]) + bi_ref[:, cols])
        log_a = r * neg_sp[:, cols]
        a = jnp.exp(log_a)
        inp = jnp.sqrt(jnp.maximum(-jnp.tanh(log_a) * (a * a + 1.0), 0.0)) * (ig * xg)
        for s in (1, 2, 4):
            a_sh = pltpu.roll(a, s, 0)
            b_sh = pltpu.roll(inp, s, 0)
            use = sub >= s
            inp = jnp.where(use, a * b_sh + inp, inp)
            a = jnp.where(use, a * a_sh, a)
        a_ref[:, cols] = a
        b_ref[:, cols] = inp

    def body(t, hprev):
        rows = pl.ds(pl.multiple_of(t * 8, 8), 8)
        hs = b_ref[rows, :] + a_ref[rows, :] * hprev
        b_ref[rows, :] = hs
        return hs[7:8, :]

    h_ref[...] = lax.fori_loop(0, tl // 8, body, h_ref[...])
    gate = gate_ref[0]
    gelu = 0.5 * gate * (1.0 + jnp.tanh(math.sqrt(2.0 / math.pi)
                                        * (gate + 0.044715 * (gate * gate * gate))))
    o_ref[0] = (b_ref[...] * gelu).astype(o_ref.dtype)


def lru_core(proj, conv_w, conv_b, w_r, b_r, w_i, b_i, lam):
    b, L, _ = proj.shape
    tl = _pick(L, (640, 320, 128))
    wd = LRU_WIDTH
    row = lambda a: a.astype(F32).reshape(1, wd)
    return pl.pallas_call(
        _lru_kernel,
        grid=(b, L // tl),
        in_specs=[
            pl.BlockSpec((1, tl, wd), lambda bi, i: (bi, i, 0)),
            pl.BlockSpec((1, tl, wd), lambda bi, i: (bi, i, 1)),
            pl.BlockSpec((1, 8, wd), lambda bi, i: (bi, jnp.maximum(i * (tl // 8) - 1, 0), 1)),
            pl.BlockSpec((LRU_CONV, wd), lambda bi, i: (0, 0)),
            pl.BlockSpec((1, wd), lambda bi, i: (0, 0)),
            pl.BlockSpec((LRU_BLOCKS, LRU_BLOCK, LRU_BLOCK), lambda bi, i: (0, 0, 0)),
            pl.BlockSpec((1, wd), lambda bi, i: (0, 0)),
            pl.BlockSpec((LRU_BLOCKS, LRU_BLOCK, LRU_BLOCK), lambda bi, i: (0, 0, 0)),
            pl.BlockSpec((1, wd), lambda bi, i: (0, 0)),
            pl.BlockSpec((1, wd), lambda bi, i: (0, 0)),
        ],
        out_specs=pl.BlockSpec((1, tl, wd), lambda bi, i: (bi, i, 0)),
        out_shape=jax.ShapeDtypeStruct((b, L, wd), BF16),
        scratch_shapes=[pltpu.VMEM((tl + 8, wd), F32), pltpu.VMEM((tl, wd), F32),
                        pltpu.VMEM((tl, wd), F32), pltpu.VMEM((1, wd), F32)],
        compiler_params=_params("arbitrary", "arbitrary"),
        name="rglru",
    )(proj, proj, proj, conv_w, row(conv_b), w_r.astype(BF16), row(b_r), w_i.astype(BF16),
      row(b_i), row(lam))


def rglru_layer(h, nw, w_in, conv_w, conv_b, w_r, b_r, w_i, b_i, lam, w_out):
    proj = norm_matmul(h, nw, w_in, F32)
    y = lru_core(proj, conv_w, conv_b, w_r, b_r, w_i, b_i, lam)
    return out_proj_residual(y, w_out, h)


def kernel(x, meta_tokens, rel_bias, norm_mix_w, norm_mlp_w, final_norm_w, dn_w_in, dn_conv_w, dn_a_log, dn_dt_bias, dn_norm_w, dn_w_out, da_w_in, da_lam_q1, da_lam_k1, da_lam_q2, da_lam_k2, da_subln_w, da_w_out, lru_w_in, lru_conv_w, lru_conv_b, lru_w_rgate, lru_b_rgate, lru_w_igate, lru_b_igate, lru_lambda, lru_w_out, mlp_w1, mlp_w2):
    b = x.shape[0]
    depth = norm_mix_w.shape[0]
    h = jnp.concatenate([
        jnp.zeros((b, FRONT_PAD, D_MODEL), x.dtype),
        jnp.broadcast_to(meta_tokens[None].astype(x.dtype), (b, N_META, D_MODEL)),
        x,
    ], axis=1)
    dn_pad = jnp.zeros(dn_w_in.shape[:2] + (DN_SMALL - 2 * DN_HEADS,), dn_w_in.dtype)
    dn_w_all = jnp.concatenate([dn_w_in, dn_pad], axis=2).astype(BF16)
    dn_w_out, da_w_in, da_w_out, lru_w_in, lru_w_out, mlp_w1, mlp_w2 = (
        w.astype(BF16) for w in (dn_w_out, da_w_in, da_w_out, lru_w_in, lru_w_out, mlp_w1, mlp_w2))
    for layer in range(depth):
        kind = layer % N_MIXERS
        slot = layer // N_MIXERS
        if kind == 0:
            h = gated_deltanet_layer(h, norm_mix_w[layer], (dn_w_all, slot), dn_conv_w[slot],
                                     dn_a_log[slot], dn_dt_bias[slot], dn_norm_w[slot],
                                     (dn_w_out, slot))
        elif kind == 1:
            lambda_init = 0.8 - 0.6 * math.exp(-0.3 * layer)
            h = diff_attention_layer(h, norm_mix_w[layer], (da_w_in, slot), da_lam_q1[slot],
                                     da_lam_k1[slot], da_lam_q2[slot], da_lam_k2[slot],
                                     da_subln_w[slot], (da_w_out, slot), rel_bias, lambda_init)
        else:
            h = rglru_layer(h, norm_mix_w[layer], (lru_w_in, slot), lru_conv_w[slot],
                            lru_conv_b[slot], lru_w_rgate[slot], lru_b_rgate[slot],
                            lru_w_igate[slot], lru_b_igate[slot], lru_lambda[slot],
                            (lru_w_out, slot))
        final_w = final_norm_w if layer == depth - 1 else None
        h = mlp_residual(h, norm_mlp_w[layer], (mlp_w1, layer), (mlp_w2, layer), final_w)
    return h[:, FRONT_PAD + N_META:]
```

```python
import functools
import math

import jax
import jax.numpy as jnp
from jax import lax
from jax.experimental import pallas as pl
from jax.experimental.pallas import tpu as pltpu

F32 = jnp.float32
BF16 = jnp.bfloat16

D_MODEL = 1024
N_META = 16
QBLOCK = 128
FRONT_PAD = QBLOCK - N_META
N_MIXERS = 3
EPS = 1e-6
CHUNK = 64
CHUNK_SHIFT = 6
CONV_HALO = 8

DN_HEADS = 8
DN_HEAD = 128
DN_CONV = 4
DN_QKV = 3 * DN_HEADS * DN_HEAD
DN_MAIN = DN_QKV + DN_HEADS * DN_HEAD
DN_SMALL = 128
DN_CHUNK = 128
INV_BASE_LOG = 4

DA_HEADS = 8
DA_HEAD = 64
N_BUCKETS = 32
MAX_DISTANCE = 128
NEG_INF = -1e30
LOG2E = math.log2(math.e)
DA_Q_SCALE = DA_HEAD ** -0.5 * LOG2E

LRU_WIDTH = 1024
LRU_BLOCKS = 4
LRU_BLOCK = LRU_WIDTH // LRU_BLOCKS
LRU_CONV = 4
LRU_C = 8.0

D_FF = 4 * D_MODEL

V7X_VMEM_LIMIT_BYTES = 56 * 1024 * 1024


def _params(*semantics):
    return pltpu.CompilerParams(dimension_semantics=semantics,
                                vmem_limit_bytes=V7X_VMEM_LIMIT_BYTES)


def _pick(n, candidates):
    for c in candidates:
        if n % c == 0:
            return c
    raise ValueError(f"no tile for {n} in {candidates}")


def _dot(a, b):
    return jnp.dot(a, b, preferred_element_type=F32)


def _dot_nt(a, b):
    return lax.dot_general(a, b, (((1,), (1,)), ((), ())), preferred_element_type=F32)


def _split2(a):
    hi = a.astype(BF16)
    lo = (a - hi.astype(F32)).astype(BF16)
    return hi, lo


def _dot3(a, b):
    ah, al = _split2(a)
    bh, bl = _split2(b)
    return _dot(ah, bh) + (_dot(ah, bl) + _dot(al, bh))


def _rmsnorm_rows(x, w):
    return x * lax.rsqrt(jnp.mean(x * x, axis=-1, keepdims=True) + EPS) * w


def _keep_rows(tile_index, rows):
    pos = tile_index * rows + lax.broadcasted_iota(jnp.int32, (rows, 1), 0)
    return pos >= FRONT_PAD


def _norm_matmul_kernel(h_ref, nw_ref, w_ref, *rest, scaled):
    cs_ref, o_ref, u_ref = rest if scaled else (None,) + rest

    @pl.when(pl.program_id(2) == 0)
    def _():
        u_ref[...] = _rmsnorm_rows(h_ref[0], nw_ref[...]).astype(BF16)

    y = _dot(u_ref[...], w_ref[...])
    if scaled:
        y = y * cs_ref[...]
    o_ref[0] = y.astype(o_ref.dtype)


def norm_matmul(h, nw, w, out_dtype, col_scale=None):
    b, L, d = h.shape
    w, slot = w
    n = w.shape[2]
    tm = _pick(L, (1040, 640, 320, 128))
    tn = _pick(n, (1408, 1024, 512, 128))
    scaled = col_scale is not None
    in_specs = [
        pl.BlockSpec((1, tm, d), lambda bi, i, j: (bi, i, 0)),
        pl.BlockSpec((1, d), lambda bi, i, j: (0, 0)),
        pl.BlockSpec((None, d, tn), lambda bi, i, j: (slot, 0, j)),
    ]
    args = [h, nw.reshape(1, d), w]
    if scaled:
        in_specs.append(pl.BlockSpec((1, tn), lambda bi, i, j: (0, j)))
        args.append(col_scale.astype(F32).reshape(1, n))
    return pl.pallas_call(
        functools.partial(_norm_matmul_kernel, scaled=scaled),
        grid=(b, L // tm, n // tn),
        in_specs=in_specs,
        out_specs=pl.BlockSpec((1, tm, tn), lambda bi, i, j: (bi, i, j)),
        out_shape=jax.ShapeDtypeStruct((b, L, n), out_dtype),
        scratch_shapes=[pltpu.VMEM((tm, d), BF16)],
        compiler_params=_params("arbitrary", "arbitrary", "arbitrary"),
        name="norm_matmul",
    )(*args)


def _tail_kernel(y_ref, wo_ref, h_ref, nw_ref, w1_ref, w2_ref, fw_ref, o_ref,
                 u_ref, hs_ref, acc_ref, *, final_norm):
    f = pl.program_id(2)
    tm = h_ref.shape[1]

    def kept(x):
        if final_norm:
            return x
        return jnp.where(_keep_rows(pl.program_id(1), tm), x, 0.0)

    @pl.when(f == 0)
    def _():
        h1 = h_ref[0] + kept(_dot(y_ref[0], wo_ref[...]))
        hs_ref[...] = h1
        u_ref[...] = _rmsnorm_rows(h1, nw_ref[...]).astype(BF16)
        acc_ref[...] = jnp.zeros_like(acc_ref)

    a = _dot(u_ref[...], w1_ref[...])
    a = jnp.square(jnp.maximum(a, 0.0)).astype(BF16)
    acc_ref[...] += _dot(a, w2_ref[...])

    @pl.when(f == pl.num_programs(2) - 1)
    def _():
        hn = hs_ref[...] + kept(acc_ref[...])
        if final_norm:
            hn = _rmsnorm_rows(hn, fw_ref[...])
        o_ref[0] = hn


def block_tail(y, w_out, h, nw, w1, w2, final_w=None):
    b, L, d = h.shape
    (w_out, slot), (w1, layer), (w2, _) = w_out, w1, w2
    k = y.shape[-1]
    ff = w1.shape[2]
    tf = _pick(ff, (512, 128))
    final_norm = final_w is not None
    if final_norm:
        first_row = FRONT_PAD + N_META
        rows = L - first_row
        tm = _pick(rows, (1024, 512, 128))

        def row_spec(width):
            return pl.BlockSpec(
                (pl.Element(1), pl.Element(tm), pl.Element(width)),
                lambda bi, i, f: (bi, pl.multiple_of(first_row + i * tm, QBLOCK), 0))
    else:
        rows = L
        tm = _pick(L, (1040, 640, 320, 128))

        def row_spec(width):
            return pl.BlockSpec((1, tm, width), lambda bi, i, f: (bi, i, 0))
    fw = (final_w if final_norm else nw).reshape(1, d)
    return pl.pallas_call(
        functools.partial(_tail_kernel, final_norm=final_norm),
        grid=(b, rows // tm, ff // tf),
        in_specs=[
            row_spec(k),
            pl.BlockSpec((None, k, d), lambda bi, i, f: (slot, 0, 0)),
            row_spec(d),
            pl.BlockSpec((1, d), lambda bi, i, f: (0, 0)),
            pl.BlockSpec((None, d, tf), lambda bi, i, f: (layer, 0, f)),
            pl.BlockSpec((None, tf, d), lambda bi, i, f: (layer, f, 0)),
            pl.BlockSpec((1, d), lambda bi, i, f: (0, 0)),
        ],
        out_specs=pl.BlockSpec((1, tm, d), lambda bi, i, f: (bi, i, 0)),
        out_shape=jax.ShapeDtypeStruct((b, rows, d), F32),
        scratch_shapes=[pltpu.VMEM((tm, d), BF16), pltpu.VMEM((tm, d), F32),
                        pltpu.VMEM((tm, d), F32)],
        compiler_params=_params("arbitrary", "arbitrary", "arbitrary"),
        name="block_tail",
    )(y, w_out, h, nw.reshape(1, d), w1, w2, fw)


def _causal_conv_rows(xs_ref, cur, halo, first_tile, w, width):
    tl = cur.shape[0]
    xs_ref[0:CONV_HALO, :] = jnp.where(first_tile, 0.0, halo)
    xs_ref[CONV_HALO:, :] = cur
    acc = None
    for j in range(width):
        off = CONV_HALO - (width - 1) + j
        term = xs_ref[off:off + tl, :] * w[j:j + 1, :]
        acc = term if acc is None else acc + term
    return acc


def _approx_unit_lower_inverses(a_list):
    n = a_list[0].shape[0]
    row = lax.broadcasted_iota(jnp.int32, (n, n), 0)
    col = lax.broadcasted_iota(jnp.int32, (n, n), 1)

    def same_block(log_size):
        return lax.shift_right_logical(row, log_size) == lax.shift_right_logical(col, log_size)

    log_size = INV_BASE_LOG
    in_diag = same_block(log_size)
    eye = (row == col).astype(F32)
    ad = [jnp.where(in_diag, a, 0.0) for a in a_list]
    t = [eye - x for x in ad]
    bk = [x.astype(BF16) for x in ad]
    for _ in range(log_size - 1):
        bk = [_dot(x, x).astype(BF16) for x in bk]
        t = [ti + _dot(ti.astype(BF16), x) for ti, x in zip(t, bk)]
    while (1 << log_size) < n:
        sel = same_block(log_size + 1) & jnp.logical_not(same_block(log_size))
        off = [jnp.where(sel, a, 0.0).astype(BF16) for a in a_list]
        t16 = [ti.astype(BF16) for ti in t]
        left = [_dot(ti, o).astype(BF16) for ti, o in zip(t16, off)]
        t = [ti - _dot(x, ti16) for ti, x, ti16 in zip(t, left, t16)]
        log_size += 1
    return t


def _dn_chunk_kernel(q_ref, k_ref, v_ref, z_ref, s_ref, cw_ref, alog_ref, dtb_ref, nw_ref,
                     o_ref, state_ref, xs_ref):
    c = pl.program_id(0)
    n = DN_CHUNK
    nb = q_ref.shape[0]
    hd = DN_HEADS * DN_HEAD
    streams = [(b, h) for b in range(nb) for h in range(DN_HEADS)]

    @pl.when(c == 0)
    def _():
        state_ref[...] = jnp.zeros_like(state_ref)
        xs_ref[:, 0:CONV_HALO, :] = jnp.zeros((nb, CONV_HALO, 3 * hd), F32)

    for b in range(nb):
        xs_ref[b, CONV_HALO:, 0:hd] = q_ref[b]
        xs_ref[b, CONV_HALO:, hd:2 * hd] = k_ref[b]
        xs_ref[b, CONV_HALO:, 2 * hd:] = v_ref[b]

    def conv_act(b, g):
        lanes = slice(g * DN_HEAD, (g + 1) * DN_HEAD)
        x = xs_ref[b, :, lanes]
        acc = x * cw_ref[DN_CONV - 1:DN_CONV, lanes]
        for j in range(DN_CONV - 2, -1, -1):
            x = pltpu.roll(x, 1, 0)
            acc = acc + x * cw_ref[j:j + 1, lanes]
        acc = acc[CONV_HALO:, :]
        return acc * jax.nn.sigmoid(acc)

    def unit_rows(x):
        return x * lax.rsqrt(jnp.sum(x * x, axis=-1, keepdims=True) + EPS)

    keep = _keep_rows(c, n)
    row = lax.broadcasted_iota(jnp.int32, (n, n), 0)
    col = lax.broadcasted_iota(jnp.int32, (n, n), 1)
    incl = row >= col
    strict = row > col
    tri = incl.astype(BF16)

    beta, gc, gc_t = [], [], []
    for b in range(nb):
        small = s_ref[b]
        beta.append(jax.nn.sigmoid(small))
        sp = jnp.logaddexp(small + dtb_ref[...], 0.0)
        g = jnp.where(keep, -jnp.exp(alog_ref[...]) * sp, 0.0)
        g_hi = g.astype(BF16)
        r1 = g - g_hi.astype(F32)
        g_mid = r1.astype(BF16)
        g_lo = (r1 - g_mid.astype(F32)).astype(BF16)
        gcb = _dot(tri, g_hi) + (_dot(tri, g_mid) + _dot(tri, g_lo))
        gc.append(gcb)
        gc_t.append(gcb.T)

    def cols(h):
        return slice(h * DN_HEAD, (h + 1) * DN_HEAD)

    bcol = [beta[b][:, h:h + 1] for b, h in streams]
    gcol = [gc[b][:, DN_HEADS + h:DN_HEADS + h + 1] for b, h in streams]
    grow = [gc_t[b][DN_HEADS + h:DN_HEADS + h + 1, :] for b, h in streams]
    decay = [jnp.exp(jnp.where(incl, gi - gj, -jnp.inf)) for gi, gj in zip(gcol, grow)]
    k = [jnp.where(keep, unit_rows(conv_act(b, DN_HEADS + h)), 0.0) for b, h in streams]
    k16 = [x.astype(BF16) for x in k]
    kb = [x * bc for x, bc in zip(k, bcol)]
    a = [jnp.where(strict, _dot_nt(x.astype(BF16), y) * dc, 0.0)
         for x, y, dc in zip(kb, k16, decay)]
    t16 = [x.astype(BF16) for x in _approx_unit_lower_inverses(a)]
    a_split = [_split2(x) for x in a]

    egc = [jnp.exp(x) for x in gcol]
    s = [state_ref[i] for i in range(len(streams))]
    s16 = [x.astype(BF16) for x in s]
    rhs = [jnp.where(keep, conv_act(b, 2 * DN_HEADS + h), 0.0) * bc
           - _dot((kbi * e).astype(BF16), si)
           for (b, h), bc, kbi, e, si in zip(streams, bcol, kb, egc, s16)]
    x0 = [_dot(ti, r.astype(BF16)) for ti, r in zip(t16, rhs)]
    resid = []
    for (ah, al), x, r in zip(a_split, x0, rhs):
        xh, xl = _split2(x)
        resid.append(r - x - (_dot(ah, xh) + (_dot(ah, xl) + _dot(al, xh))))
    v_new = [x + _dot(ti, r.astype(BF16)) for x, ti, r in zip(x0, t16, resid)]
    v16 = [x.astype(BF16) for x in v_new]

    q = [unit_rows(conv_act(b, h)) * (DN_HEAD ** -0.5) for b, h in streams]
    attn = [(_dot_nt(x.astype(BF16), y) * dc).astype(BF16) for x, y, dc in zip(q, k16, decay)]
    o = [_dot((x * e).astype(BF16), si) + _dot(at, vi)
         for x, e, si, at, vi in zip(q, egc, s16, attn, v16)]
    g_last = [x[n - 1:n, :] for x in gcol]
    kdec = [(x * jnp.exp(gl - gi)).T.astype(BF16) for x, gl, gi in zip(k, g_last, gcol)]
    for i, (si, gl, kd, vi) in enumerate(zip(s, g_last, kdec, v16)):
        state_ref[i] = si * jnp.exp(gl) + _dot(kd, vi)
    for (b, h), oi in zip(streams, o):
        zh = z_ref[b, :, cols(h)]
        y = _rmsnorm_rows(oi, nw_ref[...]) * (zh * jax.nn.sigmoid(zh))
        o_ref[b, :, cols(h)] = y.astype(o_ref.dtype)
    xs_ref[:, 0:CONV_HALO, :] = xs_ref[:, n:n + CONV_HALO, :]


def dn_chunk(proj, conv_w, a_log_row, dt_bias_row, norm_w):
    b, L, _ = proj.shape
    n = DN_CHUNK
    hd = DN_HEADS * DN_HEAD
    return pl.pallas_call(
        _dn_chunk_kernel,
        grid=(L // n,),
        in_specs=[
            pl.BlockSpec((b, n, hd), lambda c: (0, c, 0)),
            pl.BlockSpec((b, n, hd), lambda c: (0, c, 1)),
            pl.BlockSpec((b, n, hd), lambda c: (0, c, 2)),
            pl.BlockSpec((b, n, hd), lambda c: (0, c, 3)),
            pl.BlockSpec((b, n, DN_SMALL), lambda c: (0, c, DN_MAIN // DN_SMALL)),
            pl.BlockSpec((DN_CONV, DN_QKV), lambda c: (0, 0)),
            pl.BlockSpec((1, DN_SMALL), lambda c: (0, 0)),
            pl.BlockSpec((1, DN_SMALL), lambda c: (0, 0)),
            pl.BlockSpec((1, DN_HEAD), lambda c: (0, 0)),
        ],
        out_specs=pl.BlockSpec((b, n, hd), lambda c: (0, c, 0)),
        out_shape=jax.ShapeDtypeStruct((b, L, hd), BF16),
        scratch_shapes=[pltpu.VMEM((b * DN_HEADS, DN_HEAD, DN_HEAD), F32),
                        pltpu.VMEM((b, CONV_HALO + n, DN_QKV), F32)],
        compiler_params=_params("arbitrary"),
        name="dn_chunk",
    )(proj, proj, proj, proj, proj, conv_w, a_log_row, dt_bias_row,
      norm_w.reshape(1, DN_HEAD))


def gated_deltanet_mixer(h, nw, w_all, conv_w, a_log, dt_bias, norm_w):
    lane_pad = jnp.zeros((DN_SMALL - 2 * DN_HEADS,), F32)
    head_pad = jnp.zeros((DN_HEADS,), F32)
    a_log_row = jnp.concatenate([head_pad, a_log.astype(F32), lane_pad]).reshape(1, DN_SMALL)
    dt_bias_row = jnp.concatenate([head_pad, dt_bias.astype(F32), lane_pad]).reshape(1, DN_SMALL)
    proj = norm_matmul(h, nw, w_all, F32)
    return dn_chunk(proj, conv_w, a_log_row, dt_bias_row, norm_w)


def _t5_bucket(rel):
    nb = N_BUCKETS // 2
    ret = jnp.where(rel > 0, nb, 0)
    n = jnp.abs(rel)
    max_exact = nb // 2
    nf = jnp.maximum(n, 1).astype(F32)
    large = max_exact + (jnp.log(nf / max_exact) / math.log(MAX_DISTANCE / max_exact)
                         * (nb - max_exact)).astype(jnp.int32)
    large = jnp.minimum(large, nb - 1)
    return ret + jnp.where(n < max_exact, n, large)


def _bias_tile_kernel(tab_ref, o_ref):
    h = pl.program_id(0)
    which = pl.program_id(1)
    key = lax.broadcasted_iota(jnp.int32, (QBLOCK, QBLOCK), 0)
    query = lax.broadcasted_iota(jnp.int32, (QBLOCK, QBLOCK), 1)
    bucket = _t5_bucket(key - query - QBLOCK * which)
    acc = jnp.zeros((QBLOCK, QBLOCK), F32)
    for bkt in range(N_BUCKETS):
        acc = jnp.where(bucket == bkt, tab_ref[bkt, h], acc)
    o_ref[0, 0] = (acc - tab_ref[N_BUCKETS // 2 - 1, h]) * LOG2E


def bias_tiles(rel_bias):
    return pl.pallas_call(
        _bias_tile_kernel,
        grid=(DA_HEADS, 2),
        in_specs=[pl.BlockSpec(memory_space=pltpu.SMEM)],
        out_specs=pl.BlockSpec((1, 1, QBLOCK, QBLOCK), lambda h, w: (h, w, 0, 0)),
        out_shape=jax.ShapeDtypeStruct((DA_HEADS, 2, QBLOCK, QBLOCK), F32),
        compiler_params=_params("arbitrary", "arbitrary"),
        name="bias_tiles",
    )(rel_bias.astype(F32))


def _da_kernel(q_ref, k_ref, v_ref, bias_ref, lamv_ref, subw_ref, o_ref,
               vt_ref, qt_ref, s_ref, p_ref, mblk_ref, m_ref, l_ref, acc_ref,
               *, tq, lambda_init):
    qi = pl.program_id(2)
    nsub = tq // QBLOCK
    hw = 2 * DA_HEAD
    n_blocks = k_ref.shape[1] // tq

    @pl.when(qi == 0)
    def _():
        def prep(t, carry):
            rows = pl.ds(pl.multiple_of(t * tq, tq), tq)
            vt_ref[t] = v_ref[0, rows, :].astype(F32).T.astype(BF16)
            return carry

        lax.fori_loop(0, n_blocks, prep, 0)

    feat = lax.broadcasted_iota(jnp.int32, (hw, 1), 0)
    q_t = q_ref[0].astype(F32).T
    qt_ref[:, :tq] = jnp.where(feat < DA_HEAD, q_t, 0.0).astype(BF16)
    qt_ref[:, tq:] = jnp.where(feat >= DA_HEAD, q_t, 0.0).astype(BF16)
    m_ref[...] = jnp.full(m_ref.shape, NEG_INF, F32)
    l_ref[...] = jnp.zeros_like(l_ref)
    acc_ref[...] = jnp.zeros_like(acc_ref)

    def sub_rows(j):
        return slice(j * QBLOCK, (j + 1) * QBLOCK)

    def near_terms(s_half, j, diag, sub):
        tiles = [s_half[:, sub_rows(qq)] for qq in range(nsub)]
        if sub and j == nsub - 1:
            tiles[0] = tiles[0] + bias_ref[0, 1]
        if diag:
            key = lax.broadcasted_iota(jnp.int32, (QBLOCK, QBLOCK), 0)
            query = lax.broadcasted_iota(jnp.int32, (QBLOCK, QBLOCK), 1)
            allowed = (lax.shift_right_logical(key, CHUNK_SHIFT)
                       <= lax.shift_right_logical(query, CHUNK_SHIFT))
            for qq in range(j):
                tiles[qq] = jnp.full((QBLOCK, QBLOCK), NEG_INF, F32)
            tiles[j] = jnp.where(allowed, tiles[j] + bias_ref[0, 0], NEG_INF)
            if j + 1 < nsub:
                tiles[j + 1] = tiles[j + 1] + bias_ref[0, 1]
        return jnp.concatenate(tiles, axis=1)

    def scores_part(kb, j0, nj, diag, sub, first):
        rows = pl.ds(pl.multiple_of(kb * tq + j0 * QBLOCK, QBLOCK), nj * QBLOCK)
        s = _dot(k_ref[0, rows, :], qt_ref[...])
        pieces = []
        for jj in range(nj):
            j = j0 + jj
            piece = s[sub_rows(jj), :]
            if diag or (sub and j == nsub - 1):
                piece = jnp.concatenate([near_terms(piece[:, :tq], j, diag, sub),
                                         near_terms(piece[:, tq:], j, diag, sub)], axis=1)
            if first and j == 0:
                valid = lax.broadcasted_iota(jnp.int32, (QBLOCK, 1), 0) >= FRONT_PAD
                piece = jnp.where(valid, piece, NEG_INF)
            pieces.append(piece)
        s = pieces[0] if nj == 1 else jnp.concatenate(pieces, axis=0)
        s_ref[j0 * QBLOCK:(j0 + nj) * QBLOCK, :] = s
        return jnp.max(s.reshape(nj * QBLOCK // 8, 8, 2 * tq), axis=0)

    parts_after = {min(j0 + 1, nsub - 1): (j0, min(2, nsub - j0)) for j0 in range(0, nsub, 2)}

    def stage(kb, nxt=None, diag=False, sub=False, first=False):
        if kb is not None:
            m_old = m_ref[...]
            m_new = jnp.maximum(m_old, mblk_ref[...])
        lsum = None
        running = None
        for j in range(nsub):
            if kb is not None:
                p = jnp.exp2(s_ref[sub_rows(j), :] - m_new)
                lj = jnp.sum(p.reshape(QBLOCK // 8, 8, 2 * tq), axis=0)
                lsum = lj if lsum is None else lsum + lj
                p_ref[sub_rows(j), :] = p.astype(BF16)
            if nxt is not None and j in parts_after:
                mj = scores_part(nxt, *parts_after[j], diag, sub, first)
                running = mj if running is None else jnp.maximum(running, mj)
        if nxt is not None:
            mblk_ref[...] = jnp.max(running, axis=0, keepdims=True)
        if kb is not None:
            alpha = jnp.exp2(m_old - m_new)
            l_ref[...] = alpha * l_ref[...] + jnp.sum(lsum, axis=0, keepdims=True)
            acc_ref[...] = alpha * acc_ref[...] + _dot(vt_ref[kb], p_ref[...])
            m_ref[...] = m_new

    def region(pred, *args, **kwargs):
        @pl.when(pred)
        def _():
            stage(*args, **kwargs)

    region(qi == 0, None, 0, diag=True, first=True)
    region(qi < 1, 0)

    region(qi == 1, None, 0, sub=True, first=True)
    region(qi - 1 == 0, 0, 1, diag=True)
    region(qi + 1 == 2, 1)

    region(qi >= 2, None, 0, first=True)

    @pl.when(qi > 1)
    def _():
        def body(kb, carry):
            stage(kb, kb + 1)
            return carry

        lax.fori_loop(0, qi - 2, body, 0)

    region(qi >= 2, qi - 2, qi - 1, sub=True)
    region(qi > 1, qi - 1, qi, diag=True)
    region(qi - 2 >= 0, qi)

    lamv = lamv_ref[...]
    lam = (jnp.exp(jnp.sum(lamv[0:1] * lamv[1:2], axis=-1, keepdims=True))
           - jnp.exp(jnp.sum(lamv[2:3] * lamv[3:4], axis=-1, keepdims=True)) + lambda_init)
    on = acc_ref[...] * (1.0 / l_ref[...])
    o_t = on[:, :tq] - lam * on[:, tq:]
    o = _rmsnorm_rows(o_t.T, subw_ref[...]) * (1.0 - lambda_init)
    o_ref[0] = o.astype(o_ref.dtype)


def diff_attention_core(proj, bias, lamv, subln_w, lambda_init):
    b, L, _ = proj.shape
    tq = _pick(L, (640, 128))
    hw = 2 * DA_HEAD
    return pl.pallas_call(
        functools.partial(_da_kernel, tq=tq, lambda_init=lambda_init),
        grid=(b, DA_HEADS, L // tq),
        in_specs=[
            pl.BlockSpec((1, tq, hw), lambda bi, h, i: (bi, i, h)),
            pl.BlockSpec((1, L, hw), lambda bi, h, i: (bi, 0, DA_HEADS + h)),
            pl.BlockSpec((1, L, hw), lambda bi, h, i: (bi, 0, 2 * DA_HEADS + h)),
            pl.BlockSpec((1, 2, QBLOCK, QBLOCK), lambda bi, h, i: (h, 0, 0, 0)),
            pl.BlockSpec((4, DA_HEAD), lambda bi, h, i: (0, 0)),
            pl.BlockSpec((1, hw), lambda bi, h, i: (0, 0)),
        ],
        out_specs=pl.BlockSpec((1, tq, hw), lambda bi, h, i: (bi, i, h)),
        out_shape=jax.ShapeDtypeStruct((b, L, DA_HEADS * hw), BF16),
        scratch_shapes=[pltpu.VMEM((L // tq, hw, tq), BF16), pltpu.VMEM((hw, 2 * tq), BF16),
                        pltpu.VMEM((tq, 2 * tq), F32), pltpu.VMEM((tq, 2 * tq), BF16),
                        pltpu.VMEM((1, 2 * tq), F32), pltpu.VMEM((1, 2 * tq), F32),
                        pltpu.VMEM((1, 2 * tq), F32), pltpu.VMEM((hw, 2 * tq), F32)],
        compiler_params=_params("arbitrary", "arbitrary", "arbitrary"),
        name="diff_attention",
    )(proj, proj, proj, bias, lamv, subln_w.reshape(1, hw))


def diff_attention_mixer(h, nw, w_in, lam_q1, lam_k1, lam_q2, lam_k2, subln_w,
                         rel_bias, lambda_init):
    qk = DA_HEADS * 2 * DA_HEAD
    col_scale = jnp.concatenate([jnp.full((qk,), DA_Q_SCALE, F32),
                                 jnp.ones((w_in[0].shape[2] - qk,), F32)])
    proj = norm_matmul(h, nw, w_in, BF16, col_scale)
    bias = bias_tiles(rel_bias)
    lamv = jnp.stack([lam_q1, lam_k1, lam_q2, lam_k2]).astype(F32)
    return diff_attention_core(proj, bias, lamv, subln_w, lambda_init)


def _lru_kernel(gate_ref, x_ref, halo_ref, cw_ref, cb_ref, wr_ref, br_ref, wi_ref, bi_ref,
                lam_ref, o_ref, xs_ref, a_ref, b_ref, h_ref):
    i = pl.program_id(1)
    tl = x_ref.shape[1]

    @pl.when(i == 0)
    def _():
        h_ref[...] = jnp.zeros_like(h_ref)

    xr = _causal_conv_rows(xs_ref, x_ref[0], halo_ref[0], i == 0, cw_ref[...], LRU_CONV)
    xr = jnp.where(_keep_rows(i, tl), xr + cb_ref[...], 0.0)
    neg_sp = -LRU_C * jnp.logaddexp(-lam_ref[...], 0.0)
    sub = jnp.bitwise_and(lax.broadcasted_iota(jnp.int32, (tl, 1), 0), 7)
    for g in range(LRU_BLOCKS):
        cols = slice(g * LRU_BLOCK, (g + 1) * LRU_BLOCK)
        xg = xr[:, cols]
        x16 = xg.astype(BF16)
        r = jax.nn.sigmoid(_dot(x16, wr_ref[g]) + br_ref[:, cols])
        ig = jax.nn.sigmoid(_dot(x16, wi_ref[g]) + bi_ref[:, cols])
        log_a = r * neg_sp[:, cols]
        a = jnp.exp(log_a)
        inp = jnp.sqrt(jnp.maximum(-jnp.tanh(log_a) * (a * a + 1.0), 0.0)) * (ig * xg)
        for s in (1, 2, 4):
            a_sh = pltpu.roll(a, s, 0)
            b_sh = pltpu.roll(inp, s, 0)
            use = sub >= s
            inp = jnp.where(use, a * b_sh + inp, inp)
            a = jnp.where(use, a * a_sh, a)
        a_ref[:, cols] = a
        b_ref[:, cols] = inp

    def body(t, hprev):
        rows = pl.ds(pl.multiple_of(t * 8, 8), 8)
        hs = b_ref[rows, :] + a_ref[rows, :] * hprev
        b_ref[rows, :] = hs
        return hs[7:8, :]

    h_ref[...] = lax.fori_loop(0, tl // 8, body, h_ref[...])
    gate = gate_ref[0]
    gelu = 0.5 * gate * (1.0 + jnp.tanh(math.sqrt(2.0 / math.pi)
                                        * (gate + 0.044715 * (gate * gate * gate))))
    o_ref[0] = (b_ref[...] * gelu).astype(o_ref.dtype)


def lru_core(proj, conv_w, conv_b, w_r, b_r, w_i, b_i, lam):
    b, L, _ = proj.shape
    tl = _pick(L, (640, 320, 128))
    wd = LRU_WIDTH
    row = lambda a: a.astype(F32).reshape(1, wd)
    return pl.pallas_call(
        _lru_kernel,
        grid=(b, L // tl),
        in_specs=[
            pl.BlockSpec((1, tl, wd), lambda bi, i: (bi, i, 0)),
            pl.BlockSpec((1, tl, wd), lambda bi, i: (bi, i, 1)),
            pl.BlockSpec((1, 8, wd), lambda bi, i: (bi, jnp.maximum(i * (tl // 8) - 1, 0), 1)),
            pl.BlockSpec((LRU_CONV, wd), lambda bi, i: (0, 0)),
            pl.BlockSpec((1, wd), lambda bi, i: (0, 0)),
            pl.BlockSpec((LRU_BLOCKS, LRU_BLOCK, LRU_BLOCK), lambda bi, i: (0, 0, 0)),
            pl.BlockSpec((1, wd), lambda bi, i: (0, 0)),
            pl.BlockSpec((LRU_BLOCKS, LRU_BLOCK, LRU_BLOCK), lambda bi, i: (0, 0, 0)),
            pl.BlockSpec((1, wd), lambda bi, i: (0, 0)),
            pl.BlockSpec((1, wd), lambda bi, i: (0, 0)),
        ],
        out_specs=pl.BlockSpec((1, tl, wd), lambda bi, i: (bi, i, 0)),
        out_shape=jax.ShapeDtypeStruct((b, L, wd), BF16),
        scratch_shapes=[pltpu.VMEM((tl + 8, wd), F32), pltpu.VMEM((tl, wd), F32),
                        pltpu.VMEM((tl, wd), F32), pltpu.VMEM((1, wd), F32)],
        compiler_params=_params("arbitrary", "arbitrary"),
        name="rglru",
    )(proj, proj, proj, conv_w, row(conv_b), w_r.astype(BF16), row(b_r), w_i.astype(BF16),
      row(b_i), row(lam))


def rglru_mixer(h, nw, w_in, conv_w, conv_b, w_r, b_r, w_i, b_i, lam):
    proj = norm_matmul(h, nw, w_in, F32)
    return lru_core(proj, conv_w, conv_b, w_r, b_r, w_i, b_i, lam)


def kernel(x, meta_tokens, rel_bias, norm_mix_w, norm_mlp_w, final_norm_w, dn_w_in, dn_conv_w, dn_a_log, dn_dt_bias, dn_norm_w, dn_w_out, da_w_in, da_lam_q1, da_lam_k1, da_lam_q2, da_lam_k2, da_subln_w, da_w_out, lru_w_in, lru_conv_w, lru_conv_b, lru_w_rgate, lru_b_rgate, lru_w_igate, lru_b_igate, lru_lambda, lru_w_out, mlp_w1, mlp_w2):
    b = x.shape[0]
    depth = norm_mix_w.shape[0]
    h = jnp.concatenate([
        jnp.zeros((b, FRONT_PAD, D_MODEL), x.dtype),
        jnp.broadcast_to(meta_tokens[None].astype(x.dtype), (b, N_META, D_MODEL)),
        x,
    ], axis=1)
    dn_pad = jnp.zeros(dn_w_in.shape[:2] + (DN_SMALL - 2 * DN_HEADS,), dn_w_in.dtype)
    dn_w_all = jnp.concatenate([dn_w_in, dn_pad], axis=2).astype(BF16)
    dn_w_out, da_w_in, da_w_out, lru_w_in, lru_w_out, mlp_w1, mlp_w2 = (
        w.astype(BF16) for w in (dn_w_out, da_w_in, da_w_out, lru_w_in, lru_w_out, mlp_w1, mlp_w2))
    for layer in range(depth):
        kind = layer % N_MIXERS
        slot = layer // N_MIXERS
        if kind == 0:
            y = gated_deltanet_mixer(h, norm_mix_w[layer], (dn_w_all, slot), dn_conv_w[slot],
                                     dn_a_log[slot], dn_dt_bias[slot], dn_norm_w[slot])
            w_out = dn_w_out
        elif kind == 1:
            lambda_init = 0.8 - 0.6 * math.exp(-0.3 * layer)
            y = diff_attention_mixer(h, norm_mix_w[layer], (da_w_in, slot), da_lam_q1[slot],
                                     da_lam_k1[slot], da_lam_q2[slot], da_lam_k2[slot],
                                     da_subln_w[slot], rel_bias, lambda_init)
            w_out = da_w_out
        else:
            y = rglru_mixer(h, norm_mix_w[layer], (lru_w_in, slot), lru_conv_w[slot],
                            lru_conv_b[slot], lru_w_rgate[slot], lru_b_rgate[slot],
                            lru_w_igate[slot], lru_b_igate[slot], lru_lambda[slot])
            w_out = lru_w_out
        final_w = final_norm_w if layer == depth - 1 else None
        h = block_tail(y, (w_out, slot), h, norm_mlp_w[layer], (mlp_w1, layer), (mlp_w2, layer),
                       final_w)
    return h
```

```python
import functools
import math

import jax
import jax.numpy as jnp
from jax import lax
from jax.experimental import pallas as pl
from jax.experimental.pallas import tpu as pltpu

F32 = jnp.float32
BF16 = jnp.bfloat16

D_MODEL = 1024
N_META = 16
QBLOCK = 128
FRONT_PAD = QBLOCK - N_META
N_MIXERS = 3
EPS = 1e-6
CHUNK = 64
CHUNK_SHIFT = 6
CONV_HALO = 8
BF16_ROWS = 16

DN_HEADS = 8
DN_HEAD = 128
DN_CONV = 4
DN_QKV = 3 * DN_HEADS * DN_HEAD
DN_MAIN = DN_QKV + DN_HEADS * DN_HEAD
DN_SMALL = 128
DN_CHUNK = 128
INV_BASE_LOG = 4

DA_HEADS = 8
DA_HEAD = 64
N_BUCKETS = 32
MAX_DISTANCE = 128
NEG_INF = -1e30
LOG2E = math.log2(math.e)
DA_Q_SCALE = DA_HEAD ** -0.5 * LOG2E

LRU_WIDTH = 1024
LRU_BLOCKS = 4
LRU_BLOCK = LRU_WIDTH // LRU_BLOCKS
LRU_CONV = 4
LRU_C = 8.0

D_FF = 4 * D_MODEL

V7X_VMEM_LIMIT_BYTES = 56 * 1024 * 1024


def _params(*semantics):
    return pltpu.CompilerParams(dimension_semantics=semantics,
                                vmem_limit_bytes=V7X_VMEM_LIMIT_BYTES)


def _pick(n, candidates):
    for c in candidates:
        if n % c == 0:
            return c
    raise ValueError(f"no tile for {n} in {candidates}")


def _dot(a, b):
    return jnp.dot(a, b, preferred_element_type=F32)


def _dot_nt(a, b):
    return lax.dot_general(a, b, (((1,), (1,)), ((), ())), preferred_element_type=F32)


def _split2(a):
    hi = a.astype(BF16)
    lo = (a - hi.astype(F32)).astype(BF16)
    return hi, lo


def _dot3(a, b):
    ah, al = _split2(a)
    bh, bl = _split2(b)
    return _dot(ah, bh) + (_dot(ah, bl) + _dot(al, bh))


def _rmsnorm_rows(x, w):
    return x * lax.rsqrt(jnp.mean(x * x, axis=-1, keepdims=True) + EPS) * w


def _keep_rows(tile_index, rows):
    pos = tile_index * rows + lax.broadcasted_iota(jnp.int32, (rows, 1), 0)
    return pos >= FRONT_PAD


def _norm_matmul_kernel(h_ref, nw_ref, w_ref, *rest, scaled, tail):
    rest = list(rest)
    cs_ref = rest.pop(0) if scaled else None
    o_ref = rest.pop(0)
    t_ref = rest.pop(0) if tail else None
    u_ref = rest.pop(0)
    j = pl.program_id(2)

    @pl.when(j == 0)
    def _():
        u_ref[...] = _rmsnorm_rows(h_ref[0], nw_ref[...]).astype(BF16)

    y = _dot(u_ref[...], w_ref[...])
    if scaled:
        y = y * cs_ref[...]
    o_ref[0] = y.astype(o_ref.dtype)
    if tail:
        @pl.when(j == pl.num_programs(2) - 1)
        def _():
            t_ref[0] = y[:, y.shape[1] - tail:]


def norm_matmul(h, nw, w, out_dtype, col_scale=None, tail=0):
    b, L, d = h.shape
    w, slot = w
    n = w.shape[2]
    tm = _pick(L, (1040, 640, 320, 128))
    tn = _pick(n, (1408, 1024, 512, 128))
    scaled = col_scale is not None
    in_specs = [
        pl.BlockSpec((1, tm, d), lambda bi, i, j: (bi, i, 0)),
        pl.BlockSpec((1, d), lambda bi, i, j: (0, 0)),
        pl.BlockSpec((None, d, tn), lambda bi, i, j: (slot, 0, j)),
    ]
    args = [h, nw.reshape(1, d), w]
    if scaled:
        in_specs.append(pl.BlockSpec((1, tn), lambda bi, i, j: (0, j)))
        args.append(col_scale.astype(F32).reshape(1, n))
    out_specs = pl.BlockSpec((1, tm, tn), lambda bi, i, j: (bi, i, j))
    out_shape = jax.ShapeDtypeStruct((b, L, n), out_dtype)
    if tail:
        out_specs = [out_specs, pl.BlockSpec((1, tm, tail), lambda bi, i, j: (bi, i, 0))]
        out_shape = [out_shape, jax.ShapeDtypeStruct((b, L, tail), F32)]
    return pl.pallas_call(
        functools.partial(_norm_matmul_kernel, scaled=scaled, tail=tail),
        grid=(b, L // tm, n // tn),
        in_specs=in_specs,
        out_specs=out_specs,
        out_shape=out_shape,
        scratch_shapes=[pltpu.VMEM((tm, d), BF16)],
        compiler_params=_params("arbitrary", "arbitrary", "arbitrary"),
        name="norm_matmul",
    )(*args)


def _tail_kernel(y_ref, wo_ref, h_ref, nw_ref, w1_ref, w2_ref, fw_ref, o_ref,
                 u_ref, hs_ref, acc_ref, *, final_norm, chunks):
    f = pl.program_id(2)
    tm = h_ref.shape[1]

    def kept(x):
        if final_norm:
            return x
        return jnp.where(_keep_rows(pl.program_id(1), tm), x, 0.0)

    @pl.when(f == 0)
    def _():
        keep_all = None if final_norm else _keep_rows(pl.program_id(1), tm)
        step = tm // chunks
        for c in range(chunks):
            rows = slice(c * step, (c + 1) * step)
            mix = _dot(y_ref[0, rows, :], wo_ref[...])
            if keep_all is not None:
                mix = jnp.where(keep_all[rows], mix, 0.0)
            h1 = h_ref[0, rows, :] + mix
            hs_ref[rows, :] = h1
            u_ref[rows, :] = _rmsnorm_rows(h1, nw_ref[...]).astype(BF16)
        acc_ref[...] = jnp.zeros_like(acc_ref)

    a = _dot(u_ref[...], w1_ref[...])
    a = jnp.square(jnp.maximum(a, 0.0)).astype(BF16)
    acc_ref[...] += _dot(a, w2_ref[...])

    @pl.when(f == pl.num_programs(2) - 1)
    def _():
        hn = hs_ref[...] + kept(acc_ref[...])
        if final_norm:
            hn = _rmsnorm_rows(hn, fw_ref[...])
        o_ref[0] = hn


def block_tail(y, w_out, h, nw, w1, w2, final_w=None):
    b, L, d = h.shape
    (w_out, slot), (w1, layer), (w2, _) = w_out, w1, w2
    k = y.shape[-1]
    ff = w1.shape[2]
    tf = _pick(ff, (512, 128))
    final_norm = final_w is not None
    if final_norm:
        first_row = FRONT_PAD + N_META
        rows = L - first_row
        tm = _pick(rows, (1024, 512, 128))

        def row_spec(width):
            return pl.BlockSpec(
                (pl.Element(1), pl.Element(tm), pl.Element(width)),
                lambda bi, i, f: (bi, pl.multiple_of(first_row + i * tm, QBLOCK), 0))
    else:
        rows = L
        tm = _pick(L, (1040, 640, 320, 128))

        def row_spec(width):
            return pl.BlockSpec((1, tm, width), lambda bi, i, f: (bi, i, 0))
    fw = (final_w if final_norm else nw).reshape(1, d)
    chunks = _pick(tm // BF16_ROWS, (5, 4, 2, 1))
    return pl.pallas_call(
        functools.partial(_tail_kernel, final_norm=final_norm, chunks=chunks),
        grid=(b, rows // tm, ff // tf),
        in_specs=[
            row_spec(k),
            pl.BlockSpec((None, k, d), lambda bi, i, f: (slot, 0, 0)),
            row_spec(d),
            pl.BlockSpec((1, d), lambda bi, i, f: (0, 0)),
            pl.BlockSpec((None, d, tf), lambda bi, i, f: (layer, 0, f)),
            pl.BlockSpec((None, tf, d), lambda bi, i, f: (layer, f, 0)),
            pl.BlockSpec((1, d), lambda bi, i, f: (0, 0)),
        ],
        out_specs=pl.BlockSpec((1, tm, d), lambda bi, i, f: (bi, i, 0)),
        out_shape=jax.ShapeDtypeStruct((b, rows, d), F32),
        scratch_shapes=[pltpu.VMEM((tm, d), BF16), pltpu.VMEM((tm, d), F32),
                        pltpu.VMEM((tm, d), F32)],
        compiler_params=_params("arbitrary", "arbitrary", "arbitrary"),
        name="block_tail",
    )(y, w_out, h, nw.reshape(1, d), w1, w2, fw)


def _causal_conv_rows(xs_ref, cur, halo, first_tile, w, width):
    tl = cur.shape[0]
    xs_ref[0:CONV_HALO, :] = jnp.where(first_tile, 0.0, halo)
    xs_ref[CONV_HALO:, :] = cur
    acc = None
    for j in range(width):
        off = CONV_HALO - (width - 1) + j
        term = xs_ref[off:off + tl, :] * w[j:j + 1, :]
        acc = term if acc is None else acc + term
    return acc


def _approx_unit_lower_inverses(a_list):
    n = a_list[0].shape[0]
    row = lax.broadcasted_iota(jnp.int32, (n, n), 0)
    col = lax.broadcasted_iota(jnp.int32, (n, n), 1)

    def same_block(log_size):
        return lax.shift_right_logical(row, log_size) == lax.shift_right_logical(col, log_size)

    log_size = INV_BASE_LOG
    in_diag = same_block(log_size)
    eye = (row == col).astype(F32)
    ad = [jnp.where(in_diag, a, 0.0) for a in a_list]
    t = [eye - x for x in ad]
    bk = [x.astype(BF16) for x in ad]
    for _ in range(log_size - 1):
        bk = [_dot(x, x).astype(BF16) for x in bk]
        t = [ti + _dot(ti.astype(BF16), x) for ti, x in zip(t, bk)]
    while (1 << log_size) < n:
        sel = same_block(log_size + 1) & jnp.logical_not(same_block(log_size))
        off = [jnp.where(sel, a, 0.0).astype(BF16) for a in a_list]
        t16 = [ti.astype(BF16) for ti in t]
        left = [_dot(ti, o).astype(BF16) for ti, o in zip(t16, off)]
        t = [ti - _dot(x, ti16) for ti, x, ti16 in zip(t, left, t16)]
        log_size += 1
    return t


def _dn_chunk_kernel(q_ref, k_ref, v_ref, z_ref, s_ref, cw_ref, alog_ref, dtb_ref, nw_ref,
                     o_ref, state_ref, xs_ref):
    c = pl.program_id(0)
    n = DN_CHUNK
    nb = q_ref.shape[0]
    hd = DN_HEADS * DN_HEAD
    streams = [(b, h) for b in range(nb) for h in range(DN_HEADS)]

    @pl.when(c == 0)
    def _():
        state_ref[...] = jnp.zeros_like(state_ref)
        xs_ref[:, 0:CONV_HALO, :] = jnp.zeros((nb, CONV_HALO, 3 * hd), F32)

    for b in range(nb):
        xs_ref[b, CONV_HALO:, 0:hd] = q_ref[b].astype(F32)
        xs_ref[b, CONV_HALO:, hd:2 * hd] = k_ref[b].astype(F32)
        xs_ref[b, CONV_HALO:, 2 * hd:] = v_ref[b].astype(F32)

    def conv_act(b, g):
        lanes = slice(g * DN_HEAD, (g + 1) * DN_HEAD)
        x = xs_ref[b, :, lanes]
        acc = x * cw_ref[DN_CONV - 1:DN_CONV, lanes]
        for j in range(DN_CONV - 2, -1, -1):
            x = pltpu.roll(x, 1, 0)
            acc = acc + x * cw_ref[j:j + 1, lanes]
        acc = acc[CONV_HALO:, :]
        return acc * jax.nn.sigmoid(acc)

    def unit_rows(x):
        return x * lax.rsqrt(jnp.sum(x * x, axis=-1, keepdims=True) + EPS)

    keep = _keep_rows(c, n)
    row = lax.broadcasted_iota(jnp.int32, (n, n), 0)
    col = lax.broadcasted_iota(jnp.int32, (n, n), 1)
    incl = row >= col
    strict = row > col
    tri = incl.astype(BF16)

    beta, gc, gc_t = [], [], []
    for b in range(nb):
        small = s_ref[b]
        beta.append(jax.nn.sigmoid(small))
        sp = jnp.logaddexp(small + dtb_ref[...], 0.0)
        g = jnp.where(keep, -jnp.exp(alog_ref[...]) * sp, 0.0)
        g_hi = g.astype(BF16)
        r1 = g - g_hi.astype(F32)
        g_mid = r1.astype(BF16)
        g_lo = (r1 - g_mid.astype(F32)).astype(BF16)
        gcb = _dot(tri, g_hi) + (_dot(tri, g_mid) + _dot(tri, g_lo))
        gc.append(gcb)
        gc_t.append(gcb.T)

    def cols(h):
        return slice(h * DN_HEAD, (h + 1) * DN_HEAD)

    bcol = [beta[b][:, h:h + 1] for b, h in streams]
    gcol = [gc[b][:, DN_HEADS + h:DN_HEADS + h + 1] for b, h in streams]
    grow = [gc_t[b][DN_HEADS + h:DN_HEADS + h + 1, :] for b, h in streams]
    decay = [jnp.exp(jnp.where(incl, gi - gj, -jnp.inf)) for gi, gj in zip(gcol, grow)]
    k = [jnp.where(keep, unit_rows(conv_act(b, DN_HEADS + h)), 0.0) for b, h in streams]
    k16 = [x.astype(BF16) for x in k]
    kb = [x * bc for x, bc in zip(k, bcol)]
    a = [jnp.where(strict, _dot_nt(x.astype(BF16), y) * dc, 0.0)
         for x, y, dc in zip(kb, k16, decay)]
    t16 = [x.astype(BF16) for x in _approx_unit_lower_inverses(a)]
    a_split = [_split2(x) for x in a]

    egc = [jnp.exp(x) for x in gcol]
    s = [state_ref[i] for i in range(len(streams))]
    s16 = [x.astype(BF16) for x in s]
    rhs = [jnp.where(keep, conv_act(b, 2 * DN_HEADS + h), 0.0) * bc
           - _dot((kbi * e).astype(BF16), si)
           for (b, h), bc, kbi, e, si in zip(streams, bcol, kb, egc, s16)]
    x0 = [_dot(ti, r.astype(BF16)) for ti, r in zip(t16, rhs)]
    resid = []
    for (ah, al), x, r in zip(a_split, x0, rhs):
        xh, xl = _split2(x)
        resid.append(r - x - (_dot(ah, xh) + (_dot(ah, xl) + _dot(al, xh))))
    v_new = [x + _dot(ti, r.astype(BF16)) for x, ti, r in zip(x0, t16, resid)]
    v16 = [x.astype(BF16) for x in v_new]

    q = [unit_rows(conv_act(b, h)) * (DN_HEAD ** -0.5) for b, h in streams]
    attn = [(_dot_nt(x.astype(BF16), y) * dc).astype(BF16) for x, y, dc in zip(q, k16, decay)]
    o = [_dot((x * e).astype(BF16), si) + _dot(at, vi)
         for x, e, si, at, vi in zip(q, egc, s16, attn, v16)]
    g_last = [x[n - 1:n, :] for x in gcol]
    kdec = [(x * jnp.exp(gl - gi)).T.astype(BF16) for x, gl, gi in zip(k, g_last, gcol)]
    for i, (si, gl, kd, vi) in enumerate(zip(s, g_last, kdec, v16)):
        state_ref[i] = si * jnp.exp(gl) + _dot(kd, vi)
    for (b, h), oi in zip(streams, o):
        zh = z_ref[b, :, cols(h)].astype(F32)
        y = _rmsnorm_rows(oi, nw_ref[...]) * (zh * jax.nn.sigmoid(zh))
        o_ref[b, :, cols(h)] = y.astype(o_ref.dtype)
    xs_ref[:, 0:CONV_HALO, :] = xs_ref[:, n:n + CONV_HALO, :]


def dn_chunk(proj, small, conv_w, a_log_row, dt_bias_row, norm_w):
    b, L, _ = proj.shape
    n = DN_CHUNK
    hd = DN_HEADS * DN_HEAD
    return pl.pallas_call(
        _dn_chunk_kernel,
        grid=(L // n,),
        in_specs=[
            pl.BlockSpec((b, n, hd), lambda c: (0, c, 0)),
            pl.BlockSpec((b, n, hd), lambda c: (0, c, 1)),
            pl.BlockSpec((b, n, hd), lambda c: (0, c, 2)),
            pl.BlockSpec((b, n, hd), lambda c: (0, c, 3)),
            pl.BlockSpec((b, n, DN_SMALL), lambda c: (0, c, 0)),
            pl.BlockSpec((DN_CONV, DN_QKV), lambda c: (0, 0)),
            pl.BlockSpec((1, DN_SMALL), lambda c: (0, 0)),
            pl.BlockSpec((1, DN_SMALL), lambda c: (0, 0)),
            pl.BlockSpec((1, DN_HEAD), lambda c: (0, 0)),
        ],
        out_specs=pl.BlockSpec((b, n, hd), lambda c: (0, c, 0)),
        out_shape=jax.ShapeDtypeStruct((b, L, hd), BF16),
        scratch_shapes=[pltpu.VMEM((b * DN_HEADS, DN_HEAD, DN_HEAD), F32),
                        pltpu.VMEM((b, CONV_HALO + n, DN_QKV), F32)],
        compiler_params=_params("arbitrary"),
        name="dn_chunk",
    )(proj, proj, proj, proj, small, conv_w, a_log_row, dt_bias_row,
      norm_w.reshape(1, DN_HEAD))


def gated_deltanet_mixer(h, nw, w_all, conv_w, a_log, dt_bias, norm_w):
    lane_pad = jnp.zeros((DN_SMALL - 2 * DN_HEADS,), F32)
    head_pad = jnp.zeros((DN_HEADS,), F32)
    a_log_row = jnp.concatenate([head_pad, a_log.astype(F32), lane_pad]).reshape(1, DN_SMALL)
    dt_bias_row = jnp.concatenate([head_pad, dt_bias.astype(F32), lane_pad]).reshape(1, DN_SMALL)
    proj, small = norm_matmul(h, nw, w_all, BF16, tail=DN_SMALL)
    return dn_chunk(proj, small, conv_w, a_log_row, dt_bias_row, norm_w)


def _t5_bucket(rel):
    nb = N_BUCKETS // 2
    ret = jnp.where(rel > 0, nb, 0)
    n = jnp.abs(rel)
    max_exact = nb // 2
    nf = jnp.maximum(n, 1).astype(F32)
    large = max_exact + (jnp.log(nf / max_exact) / math.log(MAX_DISTANCE / max_exact)
                         * (nb - max_exact)).astype(jnp.int32)
    large = jnp.minimum(large, nb - 1)
    return ret + jnp.where(n < max_exact, n, large)


def _bias_tile_kernel(tab_ref, o_ref):
    h = pl.program_id(0)
    which = pl.program_id(1)
    key = lax.broadcasted_iota(jnp.int32, (QBLOCK, QBLOCK), 0)
    query = lax.broadcasted_iota(jnp.int32, (QBLOCK, QBLOCK), 1)
    bucket = _t5_bucket(key - query - QBLOCK * which)
    acc = jnp.zeros((QBLOCK, QBLOCK), F32)
    for bkt in range(N_BUCKETS):
        acc = jnp.where(bucket == bkt, tab_ref[bkt, h], acc)
    o_ref[0, 0] = (acc - tab_ref[N_BUCKETS // 2 - 1, h]) * LOG2E


def bias_tiles(rel_bias):
    return pl.pallas_call(
        _bias_tile_kernel,
        grid=(DA_HEADS, 2),
        in_specs=[pl.BlockSpec(memory_space=pltpu.SMEM)],
        out_specs=pl.BlockSpec((1, 1, QBLOCK, QBLOCK), lambda h, w: (h, w, 0, 0)),
        out_shape=jax.ShapeDtypeStruct((DA_HEADS, 2, QBLOCK, QBLOCK), F32),
        compiler_params=_params("arbitrary", "arbitrary"),
        name="bias_tiles",
    )(rel_bias.astype(F32))


def _da_kernel(q_ref, k_ref, v_ref, bias_ref, lamv_ref, subw_ref, o_ref,
               vt_ref, qt_ref, s_ref, p_ref, mblk_ref, m_ref, l_ref, acc_ref,
               *, tq, lambda_init):
    qi = pl.program_id(2)
    nsub = tq // QBLOCK
    hw = 2 * DA_HEAD
    n_blocks = k_ref.shape[1] // tq

    @pl.when(qi == 0)
    def _():
        def prep(t, carry):
            rows = pl.ds(pl.multiple_of(t * tq, tq), tq)
            vt_ref[t] = v_ref[0, rows, :].astype(F32).T.astype(BF16)
            return carry

        lax.fori_loop(0, n_blocks, prep, 0)

    feat = lax.broadcasted_iota(jnp.int32, (hw, 1), 0)
    q_t = q_ref[0].astype(F32).T
    qt_ref[:, :tq] = jnp.where(feat < DA_HEAD, q_t, 0.0).astype(BF16)
    qt_ref[:, tq:] = jnp.where(feat >= DA_HEAD, q_t, 0.0).astype(BF16)
    m_ref[...] = jnp.full(m_ref.shape, NEG_INF, F32)
    l_ref[...] = jnp.zeros_like(l_ref)
    acc_ref[...] = jnp.zeros_like(acc_ref)

    def sub_rows(j):
        return slice(j * QBLOCK, (j + 1) * QBLOCK)

    def near_terms(s_half, j, diag, sub):
        tiles = [s_half[:, sub_rows(qq)] for qq in range(nsub)]
        if sub and j == nsub - 1:
            tiles[0] = tiles[0] + bias_ref[0, 1]
        if diag:
            key = lax.broadcasted_iota(jnp.int32, (QBLOCK, QBLOCK), 0)
            query = lax.broadcasted_iota(jnp.int32, (QBLOCK, QBLOCK), 1)
            allowed = (lax.shift_right_logical(key, CHUNK_SHIFT)
                       <= lax.shift_right_logical(query, CHUNK_SHIFT))
            for qq in range(j):
                tiles[qq] = jnp.full((QBLOCK, QBLOCK), NEG_INF, F32)
            tiles[j] = jnp.where(allowed, tiles[j] + bias_ref[0, 0], NEG_INF)
            if j + 1 < nsub:
                tiles[j + 1] = tiles[j + 1] + bias_ref[0, 1]
        return jnp.concatenate(tiles, axis=1)

    def scores_part(kb, j0, nj, diag, sub, first):
        rows = pl.ds(pl.multiple_of(kb * tq + j0 * QBLOCK, QBLOCK), nj * QBLOCK)
        s = _dot(k_ref[0, rows, :], qt_ref[...])
        pieces = []
        for jj in range(nj):
            j = j0 + jj
            piece = s[sub_rows(jj), :]
            if diag or (sub and j == nsub - 1):
                piece = jnp.concatenate([near_terms(piece[:, :tq], j, diag, sub),
                                         near_terms(piece[:, tq:], j, diag, sub)], axis=1)
            if first and j == 0:
                valid = lax.broadcasted_iota(jnp.int32, (QBLOCK, 1), 0) >= FRONT_PAD
                piece = jnp.where(valid, piece, NEG_INF)
            pieces.append(piece)
        s = pieces[0] if nj == 1 else jnp.concatenate(pieces, axis=0)
        s_ref[j0 * QBLOCK:(j0 + nj) * QBLOCK, :] = s
        return jnp.max(s.reshape(nj * QBLOCK // 8, 8, 2 * tq), axis=0)

    parts_after = {min(j0 + 1, nsub - 1): (j0, min(2, nsub - j0)) for j0 in range(0, nsub, 2)}

    def stage(kb, nxt=None, diag=False, sub=False, first=False):
        if kb is not None:
            m_old = m_ref[...]
            m_new = jnp.maximum(m_old, mblk_ref[...])
        lsum = None
        running = None
        for j in range(nsub):
            if kb is not None:
                p = jnp.exp2(s_ref[sub_rows(j), :] - m_new)
                lj = jnp.sum(p.reshape(QBLOCK // 8, 8, 2 * tq), axis=0)
                lsum = lj if lsum is None else lsum + lj
                p_ref[sub_rows(j), :] = p.astype(BF16)
            if nxt is not None and j in parts_after:
                mj = scores_part(nxt, *parts_after[j], diag, sub, first)
                running = mj if running is None else jnp.maximum(running, mj)
        if nxt is not None:
            mblk_ref[...] = jnp.max(running, axis=0, keepdims=True)
        if kb is not None:
            alpha = jnp.exp2(m_old - m_new)
            l_ref[...] = alpha * l_ref[...] + jnp.sum(lsum, axis=0, keepdims=True)
            acc_ref[...] = alpha * acc_ref[...] + _dot(vt_ref[kb], p_ref[...])
            m_ref[...] = m_new

    def region(pred, *args, **kwargs):
        @pl.when(pred)
        def _():
            stage(*args, **kwargs)

    region(qi == 0, None, 0, diag=True, first=True)
    region(qi < 1, 0)

    region(qi == 1, None, 0, sub=True, first=True)
    region(qi - 1 == 0, 0, 1, diag=True)
    region(qi + 1 == 2, 1)

    region(qi >= 2, None, 0, first=True)

    @pl.when(qi > 1)
    def _():
        def body(kb, carry):
            stage(kb, kb + 1)
            return carry

        lax.fori_loop(0, qi - 2, body, 0)

    region(qi >= 2, qi - 2, qi - 1, sub=True)
    region(qi > 1, qi - 1, qi, diag=True)
    region(qi - 2 >= 0, qi)

    lamv = lamv_ref[...]
    lam = (jnp.exp(jnp.sum(lamv[0:1] * lamv[1:2], axis=-1, keepdims=True))
           - jnp.exp(jnp.sum(lamv[2:3] * lamv[3:4], axis=-1, keepdims=True)) + lambda_init)
    on = acc_ref[...] * (1.0 / l_ref[...])
    o_t = on[:, :tq] - lam * on[:, tq:]
    o = _rmsnorm_rows(o_t.T, subw_ref[...]) * (1.0 - lambda_init)
    o_ref[0] = o.astype(o_ref.dtype)


def diff_attention_core(proj, bias, lamv, subln_w, lambda_init):
    b, L, _ = proj.shape
    tq = _pick(L, (640, 128))
    hw = 2 * DA_HEAD
    return pl.pallas_call(
        functools.partial(_da_kernel, tq=tq, lambda_init=lambda_init),
        grid=(b, DA_HEADS, L // tq),
        in_specs=[
            pl.BlockSpec((1, tq, hw), lambda bi, h, i: (bi, i, h)),
            pl.BlockSpec((1, L, hw), lambda bi, h, i: (bi, 0, DA_HEADS + h)),
            pl.BlockSpec((1, L, hw), lambda bi, h, i: (bi, 0, 2 * DA_HEADS + h)),
            pl.BlockSpec((1, 2, QBLOCK, QBLOCK), lambda bi, h, i: (h, 0, 0, 0)),
            pl.BlockSpec((4, DA_HEAD), lambda bi, h, i: (0, 0)),
            pl.BlockSpec((1, hw), lambda bi, h, i: (0, 0)),
        ],
        out_specs=pl.BlockSpec((1, tq, hw), lambda bi, h, i: (bi, i, h)),
        out_shape=jax.ShapeDtypeStruct((b, L, DA_HEADS * hw), BF16),
        scratch_shapes=[pltpu.VMEM((L // tq, hw, tq), BF16), pltpu.VMEM((hw, 2 * tq), BF16),
                        pltpu.VMEM((tq, 2 * tq), F32), pltpu.VMEM((tq, 2 * tq), BF16),
                        pltpu.VMEM((1, 2 * tq), F32), pltpu.VMEM((1, 2 * tq), F32),
                        pltpu.VMEM((1, 2 * tq), F32), pltpu.VMEM((hw, 2 * tq), F32)],
        compiler_params=_params("arbitrary", "arbitrary", "arbitrary"),
        name="diff_attention",
    )(proj, proj, proj, bias, lamv, subln_w.reshape(1, hw))


def diff_attention_mixer(h, nw, w_in, lam_q1, lam_k1, lam_q2, lam_k2, subln_w,
                         rel_bias, lambda_init):
    qk = DA_HEADS * 2 * DA_HEAD
    col_scale = jnp.concatenate([jnp.full((qk,), DA_Q_SCALE, F32),
                                 jnp.ones((w_in[0].shape[2] - qk,), F32)])
    proj = norm_matmul(h, nw, w_in, BF16, col_scale)
    bias = bias_tiles(rel_bias)
    lamv = jnp.stack([lam_q1, lam_k1, lam_q2, lam_k2]).astype(F32)
    return diff_attention_core(proj, bias, lamv, subln_w, lambda_init)


def _lru_kernel(gate_ref, x_ref, halo_ref, cw_ref, cb_ref, wr_ref, br_ref, wi_ref, bi_ref,
                lam_ref, o_ref, xs_ref, a_ref, b_ref, h_ref):
    i = pl.program_id(1)
    tl = x_ref.shape[1]

    @pl.when(i == 0)
    def _():
        h_ref[...] = jnp.zeros_like(h_ref)

    halo = halo_ref[0, BF16_ROWS - CONV_HALO:, :].astype(F32)
    xr = _causal_conv_rows(xs_ref, x_ref[0].astype(F32), halo, i == 0, cw_ref[...], LRU_CONV)
    xr = jnp.where(_keep_rows(i, tl), xr + cb_ref[...], 0.0)
    neg_sp = -LRU_C * jnp.logaddexp(-lam_ref[...], 0.0)
    sub = jnp.bitwise_and(lax.broadcasted_iota(jnp.int32, (tl, 1), 0), 7)
    for g in range(LRU_BLOCKS):
        cols = slice(g * LRU_BLOCK, (g + 1) * LRU_BLOCK)
        xg = xr[:, cols]
        x16 = xg.astype(BF16)
        r = jax.nn.sigmoid(_dot(x16, wr_ref[g]) + br_ref[:, cols])
        ig = jax.nn.sigmoid(_dot(x16, wi_ref[g]) + bi_ref[:, cols])
        log_a = r * neg_sp[:, cols]
        a = jnp.exp(log_a)
        inp = jnp.sqrt(jnp.maximum(-jnp.tanh(log_a) * (a * a + 1.0), 0.0)) * (ig * xg)
        for s in (1, 2, 4):
            a_sh = pltpu.roll(a, s, 0)
            b_sh = pltpu.roll(inp, s, 0)
            use = sub >= s
            inp = jnp.where(use, a * b_sh + inp, inp)
            a = jnp.where(use, a * a_sh, a)
        a_ref[:, cols] = a
        b_ref[:, cols] = inp

    def body(t, hprev):
        rows = pl.ds(pl.multiple_of(t * 8, 8), 8)
        hs = b_ref[rows, :] + a_ref[rows, :] * hprev
        b_ref[rows, :] = hs
        return hs[7:8, :]

    h_ref[...] = lax.fori_loop(0, tl // 8, body, h_ref[...])
    gate = gate_ref[0].astype(F32)
    gelu = 0.5 * gate * (1.0 + jnp.tanh(math.sqrt(2.0 / math.pi)
                                        * (gate + 0.044715 * (gate * gate * gate))))
    o_ref[0] = (b_ref[...] * gelu).astype(o_ref.dtype)


def lru_core(proj, conv_w, conv_b, w_r, b_r, w_i, b_i, lam):
    b, L, _ = proj.shape
    tl = _pick(L, (640, 320, 128))
    wd = LRU_WIDTH
    row = lambda a: a.astype(F32).reshape(1, wd)
    return pl.pallas_call(
        _lru_kernel,
        grid=(b, L // tl),
        in_specs=[
            pl.BlockSpec((1, tl, wd), lambda bi, i: (bi, i, 0)),
            pl.BlockSpec((1, tl, wd), lambda bi, i: (bi, i, 1)),
            pl.BlockSpec((1, BF16_ROWS, wd),
                         lambda bi, i: (bi, jnp.maximum(i * (tl // BF16_ROWS) - 1, 0), 1)),
            pl.BlockSpec((LRU_CONV, wd), lambda bi, i: (0, 0)),
            pl.BlockSpec((1, wd), lambda bi, i: (0, 0)),
            pl.BlockSpec((LRU_BLOCKS, LRU_BLOCK, LRU_BLOCK), lambda bi, i: (0, 0, 0)),
            pl.BlockSpec((1, wd), lambda bi, i: (0, 0)),
            pl.BlockSpec((LRU_BLOCKS, LRU_BLOCK, LRU_BLOCK), lambda bi, i: (0, 0, 0)),
            pl.BlockSpec((1, wd), lambda bi, i: (0, 0)),
            pl.BlockSpec((1, wd), lambda bi, i: (0, 0)),
        ],
        out_specs=pl.BlockSpec((1, tl, wd), lambda bi, i: (bi, i, 0)),
        out_shape=jax.ShapeDtypeStruct((b, L, wd), BF16),
        scratch_shapes=[pltpu.VMEM((tl + CONV_HALO, wd), F32), pltpu.VMEM((tl, wd), F32),
                        pltpu.VMEM((tl, wd), F32), pltpu.VMEM((1, wd), F32)],
        compiler_params=_params("arbitrary", "arbitrary"),
        name="rglru",
    )(proj, proj, proj, conv_w, row(conv_b), w_r.astype(BF16), row(b_r), w_i.astype(BF16),
      row(b_i), row(lam))


def rglru_mixer(h, nw, w_in, conv_w, conv_b, w_r, b_r, w_i, b_i, lam):
    proj = norm_matmul(h, nw, w_in, BF16)
    return lru_core(proj, conv_w, conv_b, w_r, b_r, w_i, b_i, lam)


def kernel(x, meta_tokens, rel_bias, norm_mix_w, norm_mlp_w, final_norm_w, dn_w_in, dn_conv_w, dn_a_log, dn_dt_bias, dn_norm_w, dn_w_out, da_w_in, da_lam_q1, da_lam_k1, da_lam_q2, da_lam_k2, da_subln_w, da_w_out, lru_w_in, lru_conv_w, lru_conv_b, lru_w_rgate, lru_b_rgate, lru_w_igate, lru_b_igate, lru_lambda, lru_w_out, mlp_w1, mlp_w2):
    b = x.shape[0]
    depth = norm_mix_w.shape[0]
    h = jnp.concatenate([
        jnp.zeros((b, FRONT_PAD, D_MODEL), x.dtype),
        jnp.broadcast_to(meta_tokens[None].astype(x.dtype), (b, N_META, D_MODEL)),
        x,
    ], axis=1)
    dn_pad = jnp.zeros(dn_w_in.shape[:2] + (DN_SMALL - 2 * DN_HEADS,), dn_w_in.dtype)
    dn_w_all = jnp.concatenate([dn_w_in, dn_pad], axis=2).astype(BF16)
    dn_w_out, da_w_in, da_w_out, lru_w_in, lru_w_out, mlp_w1, mlp_w2 = (
        w.astype(BF16) for w in (dn_w_out, da_w_in, da_w_out, lru_w_in, lru_w_out, mlp_w1, mlp_w2))
    for layer in range(depth):
        kind = layer % N_MIXERS
        slot = layer // N_MIXERS
        if kind == 0:
            y = gated_deltanet_mixer(h, norm_mix_w[layer], (dn_w_all, slot), dn_conv_w[slot],
                                     dn_a_log[slot], dn_dt_bias[slot], dn_norm_w[slot])
            w_out = dn_w_out
        elif kind == 1:
            lambda_init = 0.8 - 0.6 * math.exp(-0.3 * layer)
            y = diff_attention_mixer(h, norm_mix_w[layer], (da_w_in, slot), da_lam_q1[slot],
                                     da_lam_k1[slot], da_lam_q2[slot], da_lam_k2[slot],
                                     da_subln_w[slot], rel_bias, lambda_init)
            w_out = da_w_out
        else:
            y = rglru_mixer(h, norm_mix_w[layer], (lru_w_in, slot), lru_conv_w[slot],
                            lru_conv_b[slot], lru_w_rgate[slot], lru_b_rgate[slot],
                            lru_w_igate[slot], lru_b_igate[slot], lru_lambda[slot])
            w_out = lru_w_out
        final_w = final_norm_w if layer == depth - 1 else None
        h = block_tail(y, (w_out, slot), h, norm_mlp_w[layer], (mlp_w1, layer), (mlp_w2, layer),
                       final_w)
    return h
```

```python
import functools
import math

import jax
import jax.numpy as jnp
from jax import lax
from jax.experimental import pallas as pl
from jax.experimental.pallas import tpu as pltpu

F32 = jnp.float32
BF16 = jnp.bfloat16

D_MODEL = 1024
N_META = 16
QBLOCK = 128
FRONT_PAD = QBLOCK - N_META
N_MIXERS = 3
EPS = 1e-6
CHUNK = 64
CHUNK_SHIFT = 6
CONV_HALO = 8
BF16_ROWS = 16

DN_HEADS = 8
DN_HEAD = 128
DN_CONV = 4
DN_QKV = 3 * DN_HEADS * DN_HEAD
DN_MAIN = DN_QKV + DN_HEADS * DN_HEAD
DN_SMALL = 128
DN_CHUNK = 128
INV_BASE_LOG = 4

DA_HEADS = 8
DA_HEAD = 64
N_BUCKETS = 32
MAX_DISTANCE = 128
NEG_INF = -1e30
LOG2E = math.log2(math.e)
DA_Q_SCALE = DA_HEAD ** -0.5 * LOG2E

LRU_WIDTH = 1024
LRU_BLOCKS = 4
LRU_BLOCK = LRU_WIDTH // LRU_BLOCKS
LRU_CONV = 4
LRU_C = 8.0

D_FF = 4 * D_MODEL

V7X_VMEM_LIMIT_BYTES = 56 * 1024 * 1024


def _params(*semantics):
    return pltpu.CompilerParams(dimension_semantics=semantics,
                                vmem_limit_bytes=V7X_VMEM_LIMIT_BYTES)


def _pick(n, candidates):
    for c in candidates:
        if n % c == 0:
            return c
    raise ValueError(f"no tile for {n} in {candidates}")


def _dot(a, b):
    return jnp.dot(a, b, preferred_element_type=F32)


def _dot_nt(a, b):
    return lax.dot_general(a, b, (((1,), (1,)), ((), ())), preferred_element_type=F32)


def _split2(a):
    hi = a.astype(BF16)
    lo = (a - hi.astype(F32)).astype(BF16)
    return hi, lo


def _dot3(a, b):
    ah, al = _split2(a)
    bh, bl = _split2(b)
    return _dot(ah, bh) + (_dot(ah, bl) + _dot(al, bh))


def _rmsnorm_rows(x, w):
    return x * lax.rsqrt(jnp.mean(x * x, axis=-1, keepdims=True) + EPS) * w


def _keep_rows(tile_index, rows):
    pos = tile_index * rows + lax.broadcasted_iota(jnp.int32, (rows, 1), 0)
    return pos >= FRONT_PAD


def _norm_matmul_kernel(h_ref, nw_ref, w_ref, *rest, scaled, tail):
    rest = list(rest)
    cs_ref = rest.pop(0) if scaled else None
    o_ref = rest.pop(0)
    t_ref = rest.pop(0) if tail else None
    u_ref = rest.pop(0)
    j = pl.program_id(2)

    @pl.when(j == 0)
    def _():
        u_ref[...] = _rmsnorm_rows(h_ref[0], nw_ref[...]).astype(BF16)

    y = _dot(u_ref[...], w_ref[...])
    if scaled:
        y = y * cs_ref[...]
    o_ref[0] = y.astype(o_ref.dtype)
    if tail:
        @pl.when(j == pl.num_programs(2) - 1)
        def _():
            t_ref[0] = y[:, y.shape[1] - tail:]


def norm_matmul(h, nw, w, out_dtype, col_scale=None, tail=0):
    b, L, d = h.shape
    w, slot = w
    n = w.shape[2]
    tm = _pick(L, (1040, 640, 320, 128))
    tn = _pick(n, (1408, 1024, 512, 128))
    scaled = col_scale is not None
    in_specs = [
        pl.BlockSpec((1, tm, d), lambda bi, i, j: (bi, i, 0)),
        pl.BlockSpec((1, d), lambda bi, i, j: (0, 0)),
        pl.BlockSpec((None, d, tn), lambda bi, i, j: (slot, 0, j)),
    ]
    args = [h, nw.reshape(1, d), w]
    if scaled:
        in_specs.append(pl.BlockSpec((1, tn), lambda bi, i, j: (0, j)))
        args.append(col_scale.astype(F32).reshape(1, n))
    out_specs = pl.BlockSpec((1, tm, tn), lambda bi, i, j: (bi, i, j))
    out_shape = jax.ShapeDtypeStruct((b, L, n), out_dtype)
    if tail:
        out_specs = [out_specs, pl.BlockSpec((1, tm, tail), lambda bi, i, j: (bi, i, 0))]
        out_shape = [out_shape, jax.ShapeDtypeStruct((b, L, tail), F32)]
    return pl.pallas_call(
        functools.partial(_norm_matmul_kernel, scaled=scaled, tail=tail),
        grid=(b, L // tm, n // tn),
        in_specs=in_specs,
        out_specs=out_specs,
        out_shape=out_shape,
        scratch_shapes=[pltpu.VMEM((tm, d), BF16)],
        compiler_params=_params("arbitrary", "arbitrary", "arbitrary"),
        name="norm_matmul",
    )(*args)


def _tail_kernel(y_ref, wo_ref, h_ref, nw_ref, w1_ref, w2_ref, fw_ref, o_ref,
                 u_ref, hs_ref, acc_ref, *, final_norm, chunks):
    f = pl.program_id(2)
    tm = h_ref.shape[1]

    def kept(x):
        if final_norm:
            return x
        return jnp.where(_keep_rows(pl.program_id(1), tm), x, 0.0)

    @pl.when(f == 0)
    def _():
        keep_all = None if final_norm else _keep_rows(pl.program_id(1), tm)
        step = tm // chunks
        for c in range(chunks):
            rows = slice(c * step, (c + 1) * step)
            mix = _dot(y_ref[0, rows, :], wo_ref[...])
            if keep_all is not None:
                mix = jnp.where(keep_all[rows], mix, 0.0)
            h1 = h_ref[0, rows, :] + mix
            hs_ref[rows, :] = h1
            u_ref[rows, :] = _rmsnorm_rows(h1, nw_ref[...]).astype(BF16)
        acc_ref[...] = jnp.zeros_like(acc_ref)

    a = _dot(u_ref[...], w1_ref[...])
    a = jnp.square(jnp.maximum(a, 0.0)).astype(BF16)
    acc_ref[...] += _dot(a, w2_ref[...])

    @pl.when(f == pl.num_programs(2) - 1)
    def _():
        hn = hs_ref[...] + kept(acc_ref[...])
        if final_norm:
            hn = _rmsnorm_rows(hn, fw_ref[...])
        o_ref[0] = hn


def block_tail(y, w_out, h, nw, w1, w2, final_w=None):
    b, L, d = h.shape
    (w_out, slot), (w1, layer), (w2, _) = w_out, w1, w2
    k = y.shape[-1]
    ff = w1.shape[2]
    tf = _pick(ff, (512, 128))
    final_norm = final_w is not None
    if final_norm:
        first_row = FRONT_PAD + N_META
        rows = L - first_row
        tm = _pick(rows, (1024, 512, 128))

        def row_spec(width):
            return pl.BlockSpec(
                (pl.Element(1), pl.Element(tm), pl.Element(width)),
                lambda bi, i, f: (bi, pl.multiple_of(first_row + i * tm, QBLOCK), 0))
    else:
        rows = L
        tm = _pick(L, (1040, 640, 320, 128))

        def row_spec(width):
            return pl.BlockSpec((1, tm, width), lambda bi, i, f: (bi, i, 0))
    fw = (final_w if final_norm else nw).reshape(1, d)
    chunks = _pick(tm // BF16_ROWS, (5, 4, 2, 1))
    return pl.pallas_call(
        functools.partial(_tail_kernel, final_norm=final_norm, chunks=chunks),
        grid=(b, rows // tm, ff // tf),
        in_specs=[
            row_spec(k),
            pl.BlockSpec((None, k, d), lambda bi, i, f: (slot, 0, 0)),
            row_spec(d),
            pl.BlockSpec((1, d), lambda bi, i, f: (0, 0)),
            pl.BlockSpec((None, d, tf), lambda bi, i, f: (layer, 0, f)),
            pl.BlockSpec((None, tf, d), lambda bi, i, f: (layer, f, 0)),
            pl.BlockSpec((1, d), lambda bi, i, f: (0, 0)),
        ],
        out_specs=pl.BlockSpec((1, tm, d), lambda bi, i, f: (bi, i, 0)),
        out_shape=jax.ShapeDtypeStruct((b, rows, d), F32),
        scratch_shapes=[pltpu.VMEM((tm, d), BF16), pltpu.VMEM((tm, d), F32),
                        pltpu.VMEM((tm, d), F32)],
        compiler_params=_params("arbitrary", "arbitrary", "arbitrary"),
        name="block_tail",
    )(y, w_out, h, nw.reshape(1, d), w1, w2, fw)


def _causal_conv_rows(xs_ref, cur, halo, first_tile, w, width):
    tl = cur.shape[0]
    xs_ref[0:CONV_HALO, :] = jnp.where(first_tile, 0.0, halo)
    xs_ref[CONV_HALO:, :] = cur
    acc = None
    for j in range(width):
        off = CONV_HALO - (width - 1) + j
        term = xs_ref[off:off + tl, :] * w[j:j + 1, :]
        acc = term if acc is None else acc + term
    return acc


def _approx_unit_lower_inverses(a_list):
    n = a_list[0].shape[0]
    row = lax.broadcasted_iota(jnp.int32, (n, n), 0)
    col = lax.broadcasted_iota(jnp.int32, (n, n), 1)

    def same_block(log_size):
        return lax.shift_right_logical(row, log_size) == lax.shift_right_logical(col, log_size)

    log_size = INV_BASE_LOG
    in_diag = same_block(log_size)
    eye = (row == col).astype(F32)
    ad = [jnp.where(in_diag, a, 0.0) for a in a_list]
    t = [eye - x for x in ad]
    bk = [x.astype(BF16) for x in ad]
    for _ in range(log_size - 1):
        bk = [_dot(x, x).astype(BF16) for x in bk]
        t = [ti + _dot(ti.astype(BF16), x) for ti, x in zip(t, bk)]
    while (1 << log_size) < n:
        sel = same_block(log_size + 1) & jnp.logical_not(same_block(log_size))
        off = [jnp.where(sel, a, 0.0).astype(BF16) for a in a_list]
        t16 = [ti.astype(BF16) for ti in t]
        left = [_dot(ti, o).astype(BF16) for ti, o in zip(t16, off)]
        t = [ti - _dot(x, ti16) for ti, x, ti16 in zip(t, left, t16)]
        log_size += 1
    return t


def _dn_chunk_kernel(q_ref, k_ref, v_ref, z_ref, s_ref, cw_ref, alog_ref, dtb_ref, nw_ref,
                     o_ref, state_ref, xs_ref):
    c = pl.program_id(0)
    n = DN_CHUNK
    nb = q_ref.shape[0]
    hd = DN_HEADS * DN_HEAD
    streams = [(b, h) for b in range(nb) for h in range(DN_HEADS)]

    @pl.when(c == 0)
    def _():
        state_ref[...] = jnp.zeros_like(state_ref)
        xs_ref[:, 0:CONV_HALO, :] = jnp.zeros((nb, CONV_HALO, 3 * hd), F32)

    for b in range(nb):
        xs_ref[b, CONV_HALO:, 0:hd] = q_ref[b].astype(F32)
        xs_ref[b, CONV_HALO:, hd:2 * hd] = k_ref[b].astype(F32)
        xs_ref[b, CONV_HALO:, 2 * hd:] = v_ref[b].astype(F32)

    def conv_act(b, g):
        lanes = slice(g * DN_HEAD, (g + 1) * DN_HEAD)
        x = xs_ref[b, :, lanes]
        acc = x * cw_ref[DN_CONV - 1:DN_CONV, lanes]
        for j in range(DN_CONV - 2, -1, -1):
            x = pltpu.roll(x, 1, 0)
            acc = acc + x * cw_ref[j:j + 1, lanes]
        acc = acc[CONV_HALO:, :]
        return acc * jax.nn.sigmoid(acc)

    def unit_rows(x):
        return x * lax.rsqrt(jnp.sum(x * x, axis=-1, keepdims=True) + EPS)

    keep = _keep_rows(c, n)
    row = lax.broadcasted_iota(jnp.int32, (n, n), 0)
    col = lax.broadcasted_iota(jnp.int32, (n, n), 1)
    incl = row >= col
    strict = row > col
    tri = incl.astype(BF16)

    beta, gc, gc_t = [], [], []
    for b in range(nb):
        small = s_ref[b]
        beta.append(jax.nn.sigmoid(small))
        sp = jnp.logaddexp(small + dtb_ref[...], 0.0)
        g = jnp.where(keep, -jnp.exp(alog_ref[...]) * sp, 0.0)
        g_hi = g.astype(BF16)
        r1 = g - g_hi.astype(F32)
        g_mid = r1.astype(BF16)
        g_lo = (r1 - g_mid.astype(F32)).astype(BF16)
        gcb = _dot(tri, g_hi) + (_dot(tri, g_mid) + _dot(tri, g_lo))
        gc.append(gcb)
        gc_t.append(gcb.T)

    def cols(h):
        return slice(h * DN_HEAD, (h + 1) * DN_HEAD)

    bcol = [beta[b][:, h:h + 1] for b, h in streams]
    gcol = [gc[b][:, DN_HEADS + h:DN_HEADS + h + 1] for b, h in streams]
    grow = [gc_t[b][DN_HEADS + h:DN_HEADS + h + 1, :] for b, h in streams]
    decay = [jnp.exp(jnp.where(incl, gi - gj, -jnp.inf)) for gi, gj in zip(gcol, grow)]
    k = [jnp.where(keep, unit_rows(conv_act(b, DN_HEADS + h)), 0.0) for b, h in streams]
    k16 = [x.astype(BF16) for x in k]
    kb = [x * bc for x, bc in zip(k, bcol)]
    a = [jnp.where(strict, _dot_nt(x.astype(BF16), y) * dc, 0.0)
         for x, y, dc in zip(kb, k16, decay)]
    t16 = [x.astype(BF16) for x in _approx_unit_lower_inverses(a)]
    a_split = [_split2(x) for x in a]

    egc = [jnp.exp(x) for x in gcol]
    s = [state_ref[i] for i in range(len(streams))]
    s16 = [x.astype(BF16) for x in s]
    rhs = [jnp.where(keep, conv_act(b, 2 * DN_HEADS + h), 0.0) * bc
           - _dot((kbi * e).astype(BF16), si)
           for (b, h), bc, kbi, e, si in zip(streams, bcol, kb, egc, s16)]
    x0 = [_dot(ti, r.astype(BF16)) for ti, r in zip(t16, rhs)]
    resid = []
    for (ah, al), x, r in zip(a_split, x0, rhs):
        xh, xl = _split2(x)
        resid.append(r - x - (_dot(ah, xh) + (_dot(ah, xl) + _dot(al, xh))))
    v_new = [x + _dot(ti, r.astype(BF16)) for x, ti, r in zip(x0, t16, resid)]
    v16 = [x.astype(BF16) for x in v_new]

    q = [unit_rows(conv_act(b, h)) * (DN_HEAD ** -0.5) for b, h in streams]
    attn = [(_dot_nt(x.astype(BF16), y) * dc).astype(BF16) for x, y, dc in zip(q, k16, decay)]
    o = [_dot((x * e).astype(BF16), si) + _dot(at, vi)
         for x, e, si, at, vi in zip(q, egc, s16, attn, v16)]
    g_last = [x[n - 1:n, :] for x in gcol]
    kdec = [(x * jnp.exp(gl - gi)).T.astype(BF16) for x, gl, gi in zip(k, g_last, gcol)]
    for i, (si, gl, kd, vi) in enumerate(zip(s, g_last, kdec, v16)):
        state_ref[i] = si * jnp.exp(gl) + _dot(kd, vi)
    for (b, h), oi in zip(streams, o):
        zh = z_ref[b, :, cols(h)].astype(F32)
        y = _rmsnorm_rows(oi, nw_ref[...]) * (zh * jax.nn.sigmoid(zh))
        o_ref[b, :, cols(h)] = y.astype(o_ref.dtype)
    xs_ref[:, 0:CONV_HALO, :] = xs_ref[:, n:n + CONV_HALO, :]


def dn_chunk(proj, small, conv_w, a_log_row, dt_bias_row, norm_w):
    b, L, _ = proj.shape
    n = DN_CHUNK
    hd = DN_HEADS * DN_HEAD
    return pl.pallas_call(
        _dn_chunk_kernel,
        grid=(L // n,),
        in_specs=[
            pl.BlockSpec((b, n, hd), lambda c: (0, c, 0)),
            pl.BlockSpec((b, n, hd), lambda c: (0, c, 1)),
            pl.BlockSpec((b, n, hd), lambda c: (0, c, 2)),
            pl.BlockSpec((b, n, hd), lambda c: (0, c, 3)),
            pl.BlockSpec((b, n, DN_SMALL), lambda c: (0, c, 0)),
            pl.BlockSpec((DN_CONV, DN_QKV), lambda c: (0, 0)),
            pl.BlockSpec((1, DN_SMALL), lambda c: (0, 0)),
            pl.BlockSpec((1, DN_SMALL), lambda c: (0, 0)),
            pl.BlockSpec((1, DN_HEAD), lambda c: (0, 0)),
        ],
        out_specs=pl.BlockSpec((b, n, hd), lambda c: (0, c, 0)),
        out_shape=jax.ShapeDtypeStruct((b, L, hd), BF16),
        scratch_shapes=[pltpu.VMEM((b * DN_HEADS, DN_HEAD, DN_HEAD), F32),
                        pltpu.VMEM((b, CONV_HALO + n, DN_QKV), F32)],
        compiler_params=_params("arbitrary"),
        name="dn_chunk",
    )(proj, proj, proj, proj, small, conv_w, a_log_row, dt_bias_row,
      norm_w.reshape(1, DN_HEAD))


def gated_deltanet_mixer(h, nw, w_all, conv_w, a_log, dt_bias, norm_w):
    lane_pad = jnp.zeros((DN_SMALL - 2 * DN_HEADS,), F32)
    head_pad = jnp.zeros((DN_HEADS,), F32)
    a_log_row = jnp.concatenate([head_pad, a_log.astype(F32), lane_pad]).reshape(1, DN_SMALL)
    dt_bias_row = jnp.concatenate([head_pad, dt_bias.astype(F32), lane_pad]).reshape(1, DN_SMALL)
    proj, small = norm_matmul(h, nw, w_all, BF16, tail=DN_SMALL)
    return dn_chunk(proj, small, conv_w, a_log_row, dt_bias_row, norm_w)


def _t5_bucket(rel):
    nb = N_BUCKETS // 2
    ret = jnp.where(rel > 0, nb, 0)
    n = jnp.abs(rel)
    max_exact = nb // 2
    nf = jnp.maximum(n, 1).astype(F32)
    large = max_exact + (jnp.log(nf / max_exact) / math.log(MAX_DISTANCE / max_exact)
                         * (nb - max_exact)).astype(jnp.int32)
    large = jnp.minimum(large, nb - 1)
    return ret + jnp.where(n < max_exact, n, large)


def _bias_tile_kernel(tab_ref, o_ref):
    h = pl.program_id(0)
    which = pl.program_id(1)
    key = lax.broadcasted_iota(jnp.int32, (QBLOCK, QBLOCK), 0)
    query = lax.broadcasted_iota(jnp.int32, (QBLOCK, QBLOCK), 1)
    bucket = _t5_bucket(key - query - QBLOCK * which)
    acc = jnp.zeros((QBLOCK, QBLOCK), F32)
    for bkt in range(N_BUCKETS):
        acc = jnp.where(bucket == bkt, tab_ref[bkt, h], acc)
    o_ref[0, 0] = (acc - tab_ref[N_BUCKETS // 2 - 1, h]) * LOG2E


def bias_tiles(rel_bias):
    return pl.pallas_call(
        _bias_tile_kernel,
        grid=(DA_HEADS, 2),
        in_specs=[pl.BlockSpec(memory_space=pltpu.SMEM)],
        out_specs=pl.BlockSpec((1, 1, QBLOCK, QBLOCK), lambda h, w: (h, w, 0, 0)),
        out_shape=jax.ShapeDtypeStruct((DA_HEADS, 2, QBLOCK, QBLOCK), F32),
        compiler_params=_params("arbitrary", "arbitrary"),
        name="bias_tiles",
    )(rel_bias.astype(F32))


def _da_kernel(q_ref, qnext_ref, k_ref, v_ref, bias_ref, lamv_ref, subw_ref, o_ref,
               vt_ref, qt_ref, s_ref, p_ref, mblk_ref, m_ref, l_ref, acc_ref,
               *, tq, lambda_init):
    qi = pl.program_id(2)
    nsub = tq // QBLOCK
    hw = 2 * DA_HEAD
    n_blocks = k_ref.shape[1] // tq

    @pl.when(qi == 0)
    def _():
        def prep(t, carry):
            rows = pl.ds(pl.multiple_of(t * tq, tq), tq)
            vt_ref[t] = v_ref[0, rows, :].astype(F32).T.astype(BF16)
            return carry

        lax.fori_loop(0, n_blocks, prep, 0)

    def load_queries(src_ref):
        feat = lax.broadcasted_iota(jnp.int32, (hw, 1), 0)
        q_t = src_ref[0].astype(F32).T
        qt_ref[:, :tq] = jnp.where(feat < DA_HEAD, q_t, 0.0).astype(BF16)
        qt_ref[:, tq:] = jnp.where(feat >= DA_HEAD, q_t, 0.0).astype(BF16)

    m_ref[...] = jnp.full(m_ref.shape, NEG_INF, F32)
    l_ref[...] = jnp.zeros_like(l_ref)
    acc_ref[...] = jnp.zeros_like(acc_ref)

    def sub_rows(j):
        return slice(j * QBLOCK, (j + 1) * QBLOCK)

    def near_terms(s_half, j, diag, sub):
        tiles = [s_half[:, sub_rows(qq)] for qq in range(nsub)]
        if sub and j == nsub - 1:
            tiles[0] = tiles[0] + bias_ref[0, 1]
        if diag:
            key = lax.broadcasted_iota(jnp.int32, (QBLOCK, QBLOCK), 0)
            query = lax.broadcasted_iota(jnp.int32, (QBLOCK, QBLOCK), 1)
            allowed = (lax.shift_right_logical(key, CHUNK_SHIFT)
                       <= lax.shift_right_logical(query, CHUNK_SHIFT))
            for qq in range(j):
                tiles[qq] = jnp.full((QBLOCK, QBLOCK), NEG_INF, F32)
            tiles[j] = jnp.where(allowed, tiles[j] + bias_ref[0, 0], NEG_INF)
            if j + 1 < nsub:
                tiles[j + 1] = tiles[j + 1] + bias_ref[0, 1]
        return jnp.concatenate(tiles, axis=1)

    def scores_part(kb, j0, nj, diag, sub, first):
        rows = pl.ds(pl.multiple_of(kb * tq + j0 * QBLOCK, QBLOCK), nj * QBLOCK)
        s = _dot(k_ref[0, rows, :], qt_ref[...])
        pieces = []
        for jj in range(nj):
            j = j0 + jj
            piece = s[sub_rows(jj), :]
            if diag or (sub and j == nsub - 1):
                piece = jnp.concatenate([near_terms(piece[:, :tq], j, diag, sub),
                                         near_terms(piece[:, tq:], j, diag, sub)], axis=1)
            if first and j == 0:
                valid = lax.broadcasted_iota(jnp.int32, (QBLOCK, 1), 0) >= FRONT_PAD
                piece = jnp.where(valid, piece, NEG_INF)
            pieces.append(piece)
        s = pieces[0] if nj == 1 else jnp.concatenate(pieces, axis=0)
        s_ref[j0 * QBLOCK:(j0 + nj) * QBLOCK, :] = s
        return jnp.max(s.reshape(nj * QBLOCK // 8, 8, 2 * tq), axis=0)

    parts_after = {min(j0 + 1, nsub - 1): (j0, min(2, nsub - j0)) for j0 in range(0, nsub, 2)}

    def stage(kb, nxt=None, diag=False, sub=False, first=False):
        if kb is not None:
            m_old = m_ref[...]
            m_new = jnp.maximum(m_old, mblk_ref[...])
        lsum = None
        running = None
        for j in range(nsub):
            if kb is not None:
                p = jnp.exp2(s_ref[sub_rows(j), :] - m_new)
                lj = jnp.sum(p.reshape(QBLOCK // 8, 8, 2 * tq), axis=0)
                lsum = lj if lsum is None else lsum + lj
                p_ref[sub_rows(j), :] = p.astype(BF16)
            if nxt is not None and j in parts_after:
                mj = scores_part(nxt, *parts_after[j], diag, sub, first)
                running = mj if running is None else jnp.maximum(running, mj)
        if nxt is not None:
            mblk_ref[...] = jnp.max(running, axis=0, keepdims=True)
        if kb is not None:
            alpha = jnp.exp2(m_old - m_new)
            l_ref[...] = alpha * l_ref[...] + jnp.sum(lsum, axis=0, keepdims=True)
            acc_ref[...] = alpha * acc_ref[...] + _dot(vt_ref[kb], p_ref[...])
            m_ref[...] = m_new

    def region(pred, *args, **kwargs):
        @pl.when(pred)
        def _():
            stage(*args, **kwargs)

    last = pl.num_programs(2) - 1

    @pl.when(qi == 0)
    def _():
        load_queries(q_ref)

    region(qi < 1, None, 0, diag=True, first=True)

    region(qi == 1, 0, 1, diag=True)

    @pl.when(qi > 1)
    def _():
        def body(kb, carry):
            stage(kb, kb + 1)
            return carry

        lax.fori_loop(0, qi - 2, body, 0)

    region(qi >= 2, qi - 2, qi - 1, sub=True)
    region(qi - 2 >= 0, qi - 1, qi, diag=True)

    @pl.when(qi < last)
    def _():
        load_queries(qnext_ref)

    region((qi + 1 == 1) & (qi < last), qi, 0, sub=True, first=True)
    region((qi >= 1) & (qi < last), qi, 0, first=True)
    region(qi == last, qi)

    lamv = lamv_ref[...]
    lam = (jnp.exp(jnp.sum(lamv[0:1] * lamv[1:2], axis=-1, keepdims=True))
           - jnp.exp(jnp.sum(lamv[2:3] * lamv[3:4], axis=-1, keepdims=True)) + lambda_init)
    on = acc_ref[...] * (1.0 / l_ref[...])
    o_t = on[:, :tq] - lam * on[:, tq:]
    o = _rmsnorm_rows(o_t.T, subw_ref[...]) * (1.0 - lambda_init)
    o_ref[0] = o.astype(o_ref.dtype)


def diff_attention_core(proj, bias, lamv, subln_w, lambda_init):
    b, L, _ = proj.shape
    tq = _pick(L, (640, 128))
    hw = 2 * DA_HEAD
    return pl.pallas_call(
        functools.partial(_da_kernel, tq=tq, lambda_init=lambda_init),
        grid=(b, DA_HEADS, L // tq),
        in_specs=[
            pl.BlockSpec((1, tq, hw), lambda bi, h, i: (bi, i, h)),
            pl.BlockSpec((1, tq, hw), lambda bi, h, i: (bi, jnp.minimum(i + 1, L // tq - 1), h)),
            pl.BlockSpec((1, L, hw), lambda bi, h, i: (bi, 0, DA_HEADS + h)),
            pl.BlockSpec((1, L, hw), lambda bi, h, i: (bi, 0, 2 * DA_HEADS + h)),
            pl.BlockSpec((1, 2, QBLOCK, QBLOCK), lambda bi, h, i: (h, 0, 0, 0)),
            pl.BlockSpec((4, DA_HEAD), lambda bi, h, i: (0, 0)),
            pl.BlockSpec((1, hw), lambda bi, h, i: (0, 0)),
        ],
        out_specs=pl.BlockSpec((1, tq, hw), lambda bi, h, i: (bi, i, h)),
        out_shape=jax.ShapeDtypeStruct((b, L, DA_HEADS * hw), BF16),
        scratch_shapes=[pltpu.VMEM((L // tq, hw, tq), BF16), pltpu.VMEM((hw, 2 * tq), BF16),
                        pltpu.VMEM((tq, 2 * tq), F32), pltpu.VMEM((tq, 2 * tq), BF16),
                        pltpu.VMEM((1, 2 * tq), F32), pltpu.VMEM((1, 2 * tq), F32),
                        pltpu.VMEM((1, 2 * tq), F32), pltpu.VMEM((hw, 2 * tq), F32)],
        compiler_params=_params("arbitrary", "arbitrary", "arbitrary"),
        name="diff_attention",
    )(proj, proj, proj, proj, bias, lamv, subln_w.reshape(1, hw))


def diff_attention_mixer(h, nw, w_in, lam_q1, lam_k1, lam_q2, lam_k2, subln_w,
                         rel_bias, lambda_init):
    qk = DA_HEADS * 2 * DA_HEAD
    col_scale = jnp.concatenate([jnp.full((qk,), DA_Q_SCALE, F32),
                                 jnp.ones((w_in[0].shape[2] - qk,), F32)])
    proj = norm_matmul(h, nw, w_in, BF16, col_scale)
    bias = bias_tiles(rel_bias)
    lamv = jnp.stack([lam_q1, lam_k1, lam_q2, lam_k2]).astype(F32)
    return diff_attention_core(proj, bias, lamv, subln_w, lambda_init)


def _lru_kernel(gate_ref, x_ref, halo_ref, cw_ref, cb_ref, wr_ref, br_ref, wi_ref, bi_ref,
                lam_ref, o_ref, xs_ref, a_ref, b_ref, h_ref):
    i = pl.program_id(1)
    tl = x_ref.shape[1]

    @pl.when(i == 0)
    def _():
        h_ref[...] = jnp.zeros_like(h_ref)

    halo = halo_ref[0, BF16_ROWS - CONV_HALO:, :].astype(F32)
    xr = _causal_conv_rows(xs_ref, x_ref[0].astype(F32), halo, i == 0, cw_ref[...], LRU_CONV)
    xr = jnp.where(_keep_rows(i, tl), xr + cb_ref[...], 0.0)
    neg_sp = -LRU_C * jnp.logaddexp(-lam_ref[...], 0.0)
    sub = jnp.bitwise_and(lax.broadcasted_iota(jnp.int32, (tl, 1), 0), 7)
    for g in range(LRU_BLOCKS):
        cols = slice(g * LRU_BLOCK, (g + 1) * LRU_BLOCK)
        xg = xr[:, cols]
        x16 = xg.astype(BF16)
        r = jax.nn.sigmoid(_dot(x16, wr_ref[g]) + br_ref[:, cols])
        ig = jax.nn.sigmoid(_dot(x16, wi_ref[g]) + bi_ref[:, cols])
        log_a = r * neg_sp[:, cols]
        a = jnp.exp(log_a)
        inp = jnp.sqrt(jnp.maximum(-jnp.tanh(log_a) * (a * a + 1.0), 0.0)) * (ig * xg)
        for s in (1, 2, 4):
            a_sh = pltpu.roll(a, s, 0)
            b_sh = pltpu.roll(inp, s, 0)
            use = sub >= s
            inp = jnp.where(use, a * b_sh + inp, inp)
            a = jnp.where(use, a * a_sh, a)
        a_ref[:, cols] = a
        b_ref[:, cols] = inp

    def body(t, hprev):
        rows = pl.ds(pl.multiple_of(t * 8, 8), 8)
        hs = b_ref[rows, :] + a_ref[rows, :] * hprev
        b_ref[rows, :] = hs
        return hs[7:8, :]

    h_ref[...] = lax.fori_loop(0, tl // 8, body, h_ref[...])
    gate = gate_ref[0].astype(F32)
    gelu = 0.5 * gate * (1.0 + jnp.tanh(math.sqrt(2.0 / math.pi)
                                        * (gate + 0.044715 * (gate * gate * gate))))
    o_ref[0] = (b_ref[...] * gelu).astype(o_ref.dtype)


def lru_core(proj, conv_w, conv_b, w_r, b_r, w_i, b_i, lam):
    b, L, _ = proj.shape
    tl = _pick(L, (640, 320, 128))
    wd = LRU_WIDTH
    row = lambda a: a.astype(F32).reshape(1, wd)
    return pl.pallas_call(
        _lru_kernel,
        grid=(b, L // tl),
        in_specs=[
            pl.BlockSpec((1, tl, wd), lambda bi, i: (bi, i, 0)),
            pl.BlockSpec((1, tl, wd), lambda bi, i: (bi, i, 1)),
            pl.BlockSpec((1, BF16_ROWS, wd),
                         lambda bi, i: (bi, jnp.maximum(i * (tl // BF16_ROWS) - 1, 0), 1)),
            pl.BlockSpec((LRU_CONV, wd), lambda bi, i: (0, 0)),
            pl.BlockSpec((1, wd), lambda bi, i: (0, 0)),
            pl.BlockSpec((LRU_BLOCKS, LRU_BLOCK, LRU_BLOCK), lambda bi, i: (0, 0, 0)),
            pl.BlockSpec((1, wd), lambda bi, i: (0, 0)),
            pl.BlockSpec((LRU_BLOCKS, LRU_BLOCK, LRU_BLOCK), lambda bi, i: (0, 0, 0)),
            pl.BlockSpec((1, wd), lambda bi, i: (0, 0)),
            pl.BlockSpec((1, wd), lambda bi, i: (0, 0)),
        ],
        out_specs=pl.BlockSpec((1, tl, wd), lambda bi, i: (bi, i, 0)),
        out_shape=jax.ShapeDtypeStruct((b, L, wd), BF16),
        scratch_shapes=[pltpu.VMEM((tl + CONV_HALO, wd), F32), pltpu.VMEM((tl, wd), F32),
                        pltpu.VMEM((tl, wd), F32), pltpu.VMEM((1, wd), F32)],
        compiler_params=_params("arbitrary", "arbitrary"),
        name="rglru",
    )(proj, proj, proj, conv_w, row(conv_b), w_r.astype(BF16), row(b_r), w_i.astype(BF16),
      row(b_i), row(lam))


def rglru_mixer(h, nw, w_in, conv_w, conv_b, w_r, b_r, w_i, b_i, lam):
    proj = norm_matmul(h, nw, w_in, BF16)
    return lru_core(proj, conv_w, conv_b, w_r, b_r, w_i, b_i, lam)


def kernel(x, meta_tokens, rel_bias, norm_mix_w, norm_mlp_w, final_norm_w, dn_w_in, dn_conv_w, dn_a_log, dn_dt_bias, dn_norm_w, dn_w_out, da_w_in, da_lam_q1, da_lam_k1, da_lam_q2, da_lam_k2, da_subln_w, da_w_out, lru_w_in, lru_conv_w, lru_conv_b, lru_w_rgate, lru_b_rgate, lru_w_igate, lru_b_igate, lru_lambda, lru_w_out, mlp_w1, mlp_w2):
    b = x.shape[0]
    depth = norm_mix_w.shape[0]
    h = jnp.concatenate([
        jnp.zeros((b, FRONT_PAD, D_MODEL), x.dtype),
        jnp.broadcast_to(meta_tokens[None].astype(x.dtype), (b, N_META, D_MODEL)),
        x,
    ], axis=1)
    dn_pad = jnp.zeros(dn_w_in.shape[:2] + (DN_SMALL - 2 * DN_HEADS,), dn_w_in.dtype)
    dn_w_all = jnp.concatenate([dn_w_in, dn_pad], axis=2).astype(BF16)
    dn_w_out, da_w_in, da_w_out, lru_w_in, lru_w_out, mlp_w1, mlp_w2 = (
        w.astype(BF16) for w in (dn_w_out, da_w_in, da_w_out, lru_w_in, lru_w_out, mlp_w1, mlp_w2))
    for layer in range(depth):
        kind = layer % N_MIXERS
        slot = layer // N_MIXERS
        if kind == 0:
            y = gated_deltanet_mixer(h, norm_mix_w[layer], (dn_w_all, slot), dn_conv_w[slot],
                                     dn_a_log[slot], dn_dt_bias[slot], dn_norm_w[slot])
            w_out = dn_w_out
        elif kind == 1:
            lambda_init = 0.8 - 0.6 * math.exp(-0.3 * layer)
            y = diff_attention_mixer(h, norm_mix_w[layer], (da_w_in, slot), da_lam_q1[slot],
                                     da_lam_k1[slot], da_lam_q2[slot], da_lam_k2[slot],
                                     da_subln_w[slot], rel_bias, lambda_init)
            w_out = da_w_out
        else:
            y = rglru_mixer(h, norm_mix_w[layer], (lru_w_in, slot), lru_conv_w[slot],
                            lru_conv_b[slot], lru_w_rgate[slot], lru_b_rgate[slot],
                            lru_w_igate[slot], lru_b_igate[slot], lru_lambda[slot])
            w_out = lru_w_out
        final_w = final_norm_w if layer == depth - 1 else None
        h = block_tail(y, (w_out, slot), h, norm_mlp_w[layer], (mlp_w1, layer), (mlp_w2, layer),
                       final_w)
    return h
```

```python
import functools
import math

import jax
import jax.numpy as jnp
from jax import lax
from jax.experimental import pallas as pl
from jax.experimental.pallas import tpu as pltpu

F32 = jnp.float32
BF16 = jnp.bfloat16

D_MODEL = 1024
N_META = 16
QBLOCK = 128
FRONT_PAD = QBLOCK - N_META
N_MIXERS = 3
EPS = 1e-6
CHUNK = 64
CHUNK_SHIFT = 6
CONV_HALO = 8
BF16_ROWS = 16
LANES = 128

DN_HEADS = 8
DN_HEAD = 128
DN_CONV = 4
DN_QKV = 3 * DN_HEADS * DN_HEAD
DN_MAIN = DN_QKV + DN_HEADS * DN_HEAD
DN_SMALL = 128
DN_CHUNK = 128
INV_BASE_LOG = 4

DA_HEADS = 8
DA_HEAD = 64
N_BUCKETS = 32
MAX_DISTANCE = 128
NEG_INF = -1e30
LOG2E = math.log2(math.e)
DA_Q_SCALE = DA_HEAD ** -0.5 * LOG2E

LRU_WIDTH = 1024
LRU_BLOCKS = 4
LRU_BLOCK = LRU_WIDTH // LRU_BLOCKS
LRU_CONV = 4
LRU_C = 8.0
SCAN_GROUP = 8

D_FF = 4 * D_MODEL

V7X_VMEM_LIMIT_BYTES = 56 * 1024 * 1024


def _params(*semantics):
    return pltpu.CompilerParams(dimension_semantics=semantics,
                                vmem_limit_bytes=V7X_VMEM_LIMIT_BYTES)


def _pick(n, candidates):
    for c in candidates:
        if n % c == 0:
            return c
    raise ValueError(f"no tile for {n} in {candidates}")


def _dot(a, b):
    return jnp.dot(a, b, preferred_element_type=F32)


def _dot_nt(a, b):
    return lax.dot_general(a, b, (((1,), (1,)), ((), ())), preferred_element_type=F32)


def _split2(a):
    hi = a.astype(BF16)
    lo = (a - hi.astype(F32)).astype(BF16)
    return hi, lo


def _dot3(a, b):
    ah, al = _split2(a)
    bh, bl = _split2(b)
    return _dot(ah, bh) + (_dot(ah, bl) + _dot(al, bh))


def _rmsnorm_rows(x, w):
    return x * lax.rsqrt(jnp.mean(x * x, axis=-1, keepdims=True) + EPS) * w


def _keep_rows(tile_index, rows):
    pos = tile_index * rows + lax.broadcasted_iota(jnp.int32, (rows, 1), 0)
    return pos >= FRONT_PAD


def _norm_matmul_kernel(h_ref, nw_ref, w_ref, *rest, scaled, tail):
    rest = list(rest)
    cs_ref = rest.pop(0) if scaled else None
    o_ref = rest.pop(0)
    t_ref = rest.pop(0) if tail else None
    u_ref = rest.pop(0)
    j = pl.program_id(2)

    @pl.when(j == 0)
    def _():
        u_ref[...] = _rmsnorm_rows(h_ref[0], nw_ref[...]).astype(BF16)

    y = _dot(u_ref[...], w_ref[...])
    if scaled:
        y = y * cs_ref[...]
    o_ref[0] = y.astype(o_ref.dtype)
    if tail:
        @pl.when(j == pl.num_programs(2) - 1)
        def _():
            t_ref[0] = y[:, y.shape[1] - tail:]


def norm_matmul(h, nw, w, out_dtype, col_scale=None, tail=0):
    b, L, d = h.shape
    w, slot = w
    n = w.shape[2]
    tm = _pick(L, (1040, 640, 320, 128))
    tn = _pick(n, (1408, 1024, 512, 128))
    scaled = col_scale is not None
    in_specs = [
        pl.BlockSpec((1, tm, d), lambda bi, i, j: (bi, i, 0)),
        pl.BlockSpec((1, d), lambda bi, i, j: (0, 0)),
        pl.BlockSpec((None, d, tn), lambda bi, i, j: (slot, 0, j)),
    ]
    args = [h, nw.reshape(1, d), w]
    if scaled:
        in_specs.append(pl.BlockSpec((1, tn), lambda bi, i, j: (0, j)))
        args.append(col_scale.astype(F32).reshape(1, n))
    out_specs = pl.BlockSpec((1, tm, tn), lambda bi, i, j: (bi, i, j))
    out_shape = jax.ShapeDtypeStruct((b, L, n), out_dtype)
    if tail:
        out_specs = [out_specs, pl.BlockSpec((1, tm, tail), lambda bi, i, j: (bi, i, 0))]
        out_shape = [out_shape, jax.ShapeDtypeStruct((b, L, tail), F32)]
    return pl.pallas_call(
        functools.partial(_norm_matmul_kernel, scaled=scaled, tail=tail),
        grid=(b, L // tm, n // tn),
        in_specs=in_specs,
        out_specs=out_specs,
        out_shape=out_shape,
        scratch_shapes=[pltpu.VMEM((tm, d), BF16)],
        compiler_params=_params("arbitrary", "arbitrary", "arbitrary"),
        name="norm_matmul",
    )(*args)


def _tail_kernel(y_ref, wo_ref, h_ref, nw_ref, w1_ref, w2_ref, fw_ref, o_ref,
                 u_ref, hs_ref, acc_ref, *, final_norm, chunks):
    f = pl.program_id(2)
    tm = h_ref.shape[1]

    def kept(x):
        if final_norm:
            return x
        return jnp.where(_keep_rows(pl.program_id(1), tm), x, 0.0)

    @pl.when(f == 0)
    def _():
        keep_all = None if final_norm else _keep_rows(pl.program_id(1), tm)
        step = tm // chunks
        for c in range(chunks):
            rows = slice(c * step, (c + 1) * step)
            mix = _dot(y_ref[0, rows, :], wo_ref[...])
            if keep_all is not None:
                mix = jnp.where(keep_all[rows], mix, 0.0)
            h1 = h_ref[0, rows, :] + mix
            hs_ref[rows, :] = h1
            u_ref[rows, :] = _rmsnorm_rows(h1, nw_ref[...]).astype(BF16)
        acc_ref[...] = jnp.zeros_like(acc_ref)

    a = _dot(u_ref[...], w1_ref[...])
    a = jnp.square(jnp.maximum(a, 0.0)).astype(BF16)
    acc_ref[...] += _dot(a, w2_ref[...])

    @pl.when(f == pl.num_programs(2) - 1)
    def _():
        hn = hs_ref[...] + kept(acc_ref[...])
        if final_norm:
            hn = _rmsnorm_rows(hn, fw_ref[...])
        o_ref[0] = hn


def block_tail(y, w_out, h, nw, w1, w2, final_w=None):
    b, L, d = h.shape
    (w_out, slot), (w1, layer), (w2, _) = w_out, w1, w2
    k = y.shape[-1]
    ff = w1.shape[2]
    tf = _pick(ff, (1024, 512, 128))
    final_norm = final_w is not None
    if final_norm:
        first_row = FRONT_PAD + N_META
        rows = L - first_row
        tm = _pick(rows, (1024, 512, 128))

        def row_spec(width):
            return pl.BlockSpec(
                (pl.Element(1), pl.Element(tm), pl.Element(width)),
                lambda bi, i, f: (bi, pl.multiple_of(first_row + i * tm, QBLOCK), 0))
    else:
        rows = L
        tm = _pick(L, (1040, 640, 320, 128))

        def row_spec(width):
            return pl.BlockSpec((1, tm, width), lambda bi, i, f: (bi, i, 0))
    fw = (final_w if final_norm else nw).reshape(1, d)
    chunks = _pick(tm // BF16_ROWS, (5, 4, 2, 1))
    return pl.pallas_call(
        functools.partial(_tail_kernel, final_norm=final_norm, chunks=chunks),
        grid=(b, rows // tm, ff // tf),
        in_specs=[
            row_spec(k),
            pl.BlockSpec((None, k, d), lambda bi, i, f: (slot, 0, 0)),
            row_spec(d),
            pl.BlockSpec((1, d), lambda bi, i, f: (0, 0)),
            pl.BlockSpec((None, d, tf), lambda bi, i, f: (layer, 0, f)),
            pl.BlockSpec((None, tf, d), lambda bi, i, f: (layer, f, 0)),
            pl.BlockSpec((1, d), lambda bi, i, f: (0, 0)),
        ],
        out_specs=pl.BlockSpec((1, tm, d), lambda bi, i, f: (bi, i, 0)),
        out_shape=jax.ShapeDtypeStruct((b, rows, d), F32),
        scratch_shapes=[pltpu.VMEM((tm, d), BF16), pltpu.VMEM((tm, d), F32),
                        pltpu.VMEM((tm, d), F32)],
        compiler_params=_params("arbitrary", "arbitrary", "arbitrary"),
        name="block_tail",
    )(y, w_out, h, nw.reshape(1, d), w1, w2, fw)


def _causal_conv_rows(xs_ref, cur, halo, first_tile, w, width):
    xs_ref[0:CONV_HALO, :] = jnp.where(first_tile, 0.0, halo)
    xs_ref[CONV_HALO:, :] = cur
    x = xs_ref[...]
    acc = x * w[width - 1:width, :]
    for j in range(width - 2, -1, -1):
        x = pltpu.roll(x, 1, 0)
        acc = acc + x * w[j:j + 1, :]
    return acc[CONV_HALO:, :]


def _approx_unit_lower_inverses(a_list):
    n = a_list[0].shape[0]
    row = lax.broadcasted_iota(jnp.int32, (n, n), 0)
    col = lax.broadcasted_iota(jnp.int32, (n, n), 1)

    def same_block(log_size):
        return lax.shift_right_logical(row, log_size) == lax.shift_right_logical(col, log_size)

    log_size = INV_BASE_LOG
    in_diag = same_block(log_size)
    eye = (row == col).astype(F32)
    ad = [jnp.where(in_diag, a, 0.0) for a in a_list]
    t = [eye - x for x in ad]
    bk = [x.astype(BF16) for x in ad]
    for _ in range(log_size - 1):
        bk = [_dot(x, x).astype(BF16) for x in bk]
        t = [ti + _dot(ti.astype(BF16), x) for ti, x in zip(t, bk)]
    while (1 << log_size) < n:
        sel = same_block(log_size + 1) & jnp.logical_not(same_block(log_size))
        off = [jnp.where(sel, a, 0.0).astype(BF16) for a in a_list]
        t16 = [ti.astype(BF16) for ti in t]
        left = [_dot(ti, o).astype(BF16) for ti, o in zip(t16, off)]
        t = [ti - _dot(x, ti16) for ti, x, ti16 in zip(t, left, t16)]
        log_size += 1
    return t


def _dn_chunk_kernel(q_ref, k_ref, v_ref, z_ref, s_ref, cw_ref, alog_ref, dtb_ref, nw_ref,
                     o_ref, state_ref, xs_ref):
    c = pl.program_id(0)
    n = DN_CHUNK
    nb = q_ref.shape[0]
    hd = DN_HEADS * DN_HEAD
    streams = [(b, h) for b in range(nb) for h in range(DN_HEADS)]

    @pl.when(c == 0)
    def _():
        state_ref[...] = jnp.zeros_like(state_ref)
        xs_ref[:, 0:CONV_HALO, :] = jnp.zeros((nb, CONV_HALO, 3 * hd), F32)

    for b in range(nb):
        xs_ref[b, CONV_HALO:, 0:hd] = q_ref[b].astype(F32)
        xs_ref[b, CONV_HALO:, hd:2 * hd] = k_ref[b].astype(F32)
        xs_ref[b, CONV_HALO:, 2 * hd:] = v_ref[b].astype(F32)

    def conv_act(b, g):
        lanes = slice(g * DN_HEAD, (g + 1) * DN_HEAD)
        x = xs_ref[b, :, lanes]
        acc = x * cw_ref[DN_CONV - 1:DN_CONV, lanes]
        for j in range(DN_CONV - 2, -1, -1):
            x = pltpu.roll(x, 1, 0)
            acc = acc + x * cw_ref[j:j + 1, lanes]
        acc = acc[CONV_HALO:, :]
        return acc * jax.nn.sigmoid(acc)

    def unit_rows(x):
        return x * lax.rsqrt(jnp.sum(x * x, axis=-1, keepdims=True) + EPS)

    keep = _keep_rows(c, n)
    row = lax.broadcasted_iota(jnp.int32, (n, n), 0)
    col = lax.broadcasted_iota(jnp.int32, (n, n), 1)
    incl = row >= col
    strict = row > col
    tri = incl.astype(BF16)

    beta, gc, gc_t = [], [], []
    for b in range(nb):
        small = s_ref[b]
        beta.append(jax.nn.sigmoid(small))
        sp = jnp.logaddexp(small + dtb_ref[...], 0.0)
        g = jnp.where(keep, -jnp.exp(alog_ref[...]) * sp, 0.0)
        g_hi = g.astype(BF16)
        r1 = g - g_hi.astype(F32)
        g_mid = r1.astype(BF16)
        g_lo = (r1 - g_mid.astype(F32)).astype(BF16)
        gcb = _dot(tri, g_hi) + (_dot(tri, g_mid) + _dot(tri, g_lo))
        gc.append(gcb)
        gc_t.append(gcb.T)

    def cols(h):
        return slice(h * DN_HEAD, (h + 1) * DN_HEAD)

    bcol = [beta[b][:, h:h + 1] for b, h in streams]
    gcol = [gc[b][:, DN_HEADS + h:DN_HEADS + h + 1] for b, h in streams]
    grow = [gc_t[b][DN_HEADS + h:DN_HEADS + h + 1, :] for b, h in streams]
    decay = [jnp.exp(jnp.where(incl, gi - gj, -jnp.inf)) for gi, gj in zip(gcol, grow)]
    k = [jnp.where(keep, unit_rows(conv_act(b, DN_HEADS + h)), 0.0) for b, h in streams]
    k16 = [x.astype(BF16) for x in k]
    kb = [x * bc for x, bc in zip(k, bcol)]
    a = [jnp.where(strict, _dot_nt(x.astype(BF16), y) * dc, 0.0)
         for x, y, dc in zip(kb, k16, decay)]
    t16 = [x.astype(BF16) for x in _approx_unit_lower_inverses(a)]
    a_split = [_split2(x) for x in a]

    egc = [jnp.exp(x) for x in gcol]
    s = [state_ref[i] for i in range(len(streams))]
    s16 = [x.astype(BF16) for x in s]
    rhs = [jnp.where(keep, conv_act(b, 2 * DN_HEADS + h), 0.0) * bc
           - _dot((kbi * e).astype(BF16), si)
           for (b, h), bc, kbi, e, si in zip(streams, bcol, kb, egc, s16)]
    x0 = [_dot(ti, r.astype(BF16)) for ti, r in zip(t16, rhs)]
    resid = []
    for (ah, al), x, r in zip(a_split, x0, rhs):
        xh, xl = _split2(x)
        resid.append(r - x - (_dot(ah, xh) + (_dot(ah, xl) + _dot(al, xh))))
    v_new = [x + _dot(ti, r.astype(BF16)) for x, ti, r in zip(x0, t16, resid)]
    v16 = [x.astype(BF16) for x in v_new]

    q = [unit_rows(conv_act(b, h)) * (DN_HEAD ** -0.5) for b, h in streams]
    attn = [(_dot_nt(x.astype(BF16), y) * dc).astype(BF16) for x, y, dc in zip(q, k16, decay)]
    o = [_dot((x * e).astype(BF16), si) + _dot(at, vi)
         for x, e, si, at, vi in zip(q, egc, s16, attn, v16)]
    g_last = [x[n - 1:n, :] for x in gcol]
    kdec = [(x * jnp.exp(gl - gi)).T.astype(BF16) for x, gl, gi in zip(k, g_last, gcol)]
    for i, (si, gl, kd, vi) in enumerate(zip(s, g_last, kdec, v16)):
        state_ref[i] = si * jnp.exp(gl) + _dot(kd, vi)
    for (b, h), oi in zip(streams, o):
        zh = z_ref[b, :, cols(h)].astype(F32)
        y = _rmsnorm_rows(oi, nw_ref[...]) * (zh * jax.nn.sigmoid(zh))
        o_ref[b, :, cols(h)] = y.astype(o_ref.dtype)
    xs_ref[:, 0:CONV_HALO, :] = xs_ref[:, n:n + CONV_HALO, :]


def dn_chunk(proj, small, conv_w, a_log_row, dt_bias_row, norm_w):
    b, L, _ = proj.shape
    n = DN_CHUNK
    hd = DN_HEADS * DN_HEAD
    return pl.pallas_call(
        _dn_chunk_kernel,
        grid=(L // n,),
        in_specs=[
            pl.BlockSpec((b, n, hd), lambda c: (0, c, 0)),
            pl.BlockSpec((b, n, hd), lambda c: (0, c, 1)),
            pl.BlockSpec((b, n, hd), lambda c: (0, c, 2)),
            pl.BlockSpec((b, n, hd), lambda c: (0, c, 3)),
            pl.BlockSpec((b, n, DN_SMALL), lambda c: (0, c, 0)),
            pl.BlockSpec((DN_CONV, DN_QKV), lambda c: (0, 0)),
            pl.BlockSpec((1, DN_SMALL), lambda c: (0, 0)),
            pl.BlockSpec((1, DN_SMALL), lambda c: (0, 0)),
            pl.BlockSpec((1, DN_HEAD), lambda c: (0, 0)),
        ],
        out_specs=pl.BlockSpec((b, n, hd), lambda c: (0, c, 0)),
        out_shape=jax.ShapeDtypeStruct((b, L, hd), BF16),
        scratch_shapes=[pltpu.VMEM((b * DN_HEADS, DN_HEAD, DN_HEAD), F32),
                        pltpu.VMEM((b, CONV_HALO + n, DN_QKV), F32)],
        compiler_params=_params("arbitrary"),
        name="dn_chunk",
    )(proj, proj, proj, proj, small, conv_w, a_log_row, dt_bias_row,
      norm_w.reshape(1, DN_HEAD))


def gated_deltanet_mixer(h, nw, w_all, conv_w, a_log, dt_bias, norm_w):
    lane_pad = jnp.zeros((DN_SMALL - 2 * DN_HEADS,), F32)
    head_pad = jnp.zeros((DN_HEADS,), F32)
    a_log_row = jnp.concatenate([head_pad, a_log.astype(F32), lane_pad]).reshape(1, DN_SMALL)
    dt_bias_row = jnp.concatenate([head_pad, dt_bias.astype(F32), lane_pad]).reshape(1, DN_SMALL)
    proj, small = norm_matmul(h, nw, w_all, BF16, tail=DN_SMALL)
    return dn_chunk(proj, small, conv_w, a_log_row, dt_bias_row, norm_w)


def _t5_bucket(rel):
    nb = N_BUCKETS // 2
    ret = jnp.where(rel > 0, nb, 0)
    n = jnp.abs(rel)
    max_exact = nb // 2
    nf = jnp.maximum(n, 1).astype(F32)
    large = max_exact + (jnp.log(nf / max_exact) / math.log(MAX_DISTANCE / max_exact)
                         * (nb - max_exact)).astype(jnp.int32)
    large = jnp.minimum(large, nb - 1)
    return ret + jnp.where(n < max_exact, n, large)


def _bias_tile_kernel(tab_ref, o_ref):
    h = pl.program_id(0)
    which = pl.program_id(1)
    key = lax.broadcasted_iota(jnp.int32, (QBLOCK, QBLOCK), 0)
    query = lax.broadcasted_iota(jnp.int32, (QBLOCK, QBLOCK), 1)
    bucket = _t5_bucket(key - query - QBLOCK * which)
    acc = jnp.zeros((QBLOCK, QBLOCK), F32)
    for bkt in range(N_BUCKETS):
        acc = jnp.where(bucket == bkt, tab_ref[bkt, h], acc)
    o_ref[0, 0] = (acc - tab_ref[N_BUCKETS // 2 - 1, h]) * LOG2E


def bias_tiles(rel_bias):
    return pl.pallas_call(
        _bias_tile_kernel,
        grid=(DA_HEADS, 2),
        in_specs=[pl.BlockSpec(memory_space=pltpu.SMEM)],
        out_specs=pl.BlockSpec((1, 1, QBLOCK, QBLOCK), lambda h, w: (h, w, 0, 0)),
        out_shape=jax.ShapeDtypeStruct((DA_HEADS, 2, QBLOCK, QBLOCK), F32),
        compiler_params=_params("arbitrary", "arbitrary"),
        name="bias_tiles",
    )(rel_bias.astype(F32))


def _da_kernel(q_ref, qnext_ref, k_ref, v_ref, bias_ref, lamv_ref, subw_ref, o_ref,
               vt_ref, qt_ref, s_ref, p_ref, mblk_ref, m_ref, l_ref, acc_ref,
               *, tq, lambda_init):
    qi = pl.program_id(2)
    nsub = tq // QBLOCK
    hw = 2 * DA_HEAD
    n_blocks = k_ref.shape[1] // tq

    @pl.when(qi == 0)
    def _():
        def prep(t, carry):
            rows = pl.ds(pl.multiple_of(t * tq, tq), tq)
            vt_ref[t] = v_ref[0, rows, :].astype(F32).T.astype(BF16)
            return carry

        lax.fori_loop(0, n_blocks, prep, 0)

    def load_queries(src_ref):
        feat = lax.broadcasted_iota(jnp.int32, (hw, 1), 0)
        q_t = src_ref[0].astype(F32).T
        qt_ref[:, :tq] = jnp.where(feat < DA_HEAD, q_t, 0.0).astype(BF16)
        qt_ref[:, tq:] = jnp.where(feat >= DA_HEAD, q_t, 0.0).astype(BF16)

    m_ref[...] = jnp.full(m_ref.shape, NEG_INF, F32)
    l_ref[...] = jnp.zeros_like(l_ref)
    acc_ref[...] = jnp.zeros_like(acc_ref)

    def sub_rows(j):
        return slice(j * QBLOCK, (j + 1) * QBLOCK)

    def near_terms(s_half, j, diag, sub):
        tiles = [s_half[:, sub_rows(qq)] for qq in range(nsub)]
        if sub and j == nsub - 1:
            tiles[0] = tiles[0] + bias_ref[0, 1]
        if diag:
            key = lax.broadcasted_iota(jnp.int32, (QBLOCK, QBLOCK), 0)
            query = lax.broadcasted_iota(jnp.int32, (QBLOCK, QBLOCK), 1)
            allowed = (lax.shift_right_logical(key, CHUNK_SHIFT)
                       <= lax.shift_right_logical(query, CHUNK_SHIFT))
            for qq in range(j):
                tiles[qq] = jnp.full((QBLOCK, QBLOCK), NEG_INF, F32)
            tiles[j] = jnp.where(allowed, tiles[j] + bias_ref[0, 0], NEG_INF)
            if j + 1 < nsub:
                tiles[j + 1] = tiles[j + 1] + bias_ref[0, 1]
        return jnp.concatenate(tiles, axis=1)

    def scores_part(kb, j0, nj, diag, sub, first):
        rows = pl.ds(pl.multiple_of(kb * tq + j0 * QBLOCK, QBLOCK), nj * QBLOCK)
        s = _dot(k_ref[0, rows, :], qt_ref[...])
        pieces = []
        for jj in range(nj):
            j = j0 + jj
            piece = s[sub_rows(jj), :]
            if diag or (sub and j == nsub - 1):
                piece = jnp.concatenate([near_terms(piece[:, :tq], j, diag, sub),
                                         near_terms(piece[:, tq:], j, diag, sub)], axis=1)
            if first and j == 0:
                valid = lax.broadcasted_iota(jnp.int32, (QBLOCK, 1), 0) >= FRONT_PAD
                piece = jnp.where(valid, piece, NEG_INF)
            pieces.append(piece)
        s = pieces[0] if nj == 1 else jnp.concatenate(pieces, axis=0)
        s_ref[j0 * QBLOCK:(j0 + nj) * QBLOCK, :] = s
        return jnp.max(s.reshape(nj * QBLOCK // 8, 8, 2 * tq), axis=0)

    parts_after = {min(j0 + 1, nsub - 1): (j0, min(2, nsub - j0)) for j0 in range(0, nsub, 2)}

    def stage(kb, nxt=None, diag=False, sub=False, first=False):
        if kb is not None:
            m_old = m_ref[...]
            m_new = jnp.maximum(m_old, mblk_ref[...])
        lsum = None
        running = None
        for j in range(nsub):
            if kb is not None:
                p = jnp.exp2(s_ref[sub_rows(j), :] - m_new)
                lj = jnp.sum(p.reshape(QBLOCK // 8, 8, 2 * tq), axis=0)
                lsum = lj if lsum is None else lsum + lj
                p_ref[sub_rows(j), :] = p.astype(BF16)
            if nxt is not None and j in parts_after:
                mj = scores_part(nxt, *parts_after[j], diag, sub, first)
                running = mj if running is None else jnp.maximum(running, mj)
        if nxt is not None:
            mblk_ref[...] = jnp.max(running, axis=0, keepdims=True)
        if kb is not None:
            alpha = jnp.exp2(m_old - m_new)
            l_ref[...] = alpha * l_ref[...] + jnp.sum(lsum, axis=0, keepdims=True)
            acc_ref[...] = alpha * acc_ref[...] + _dot(vt_ref[kb], p_ref[...])
            m_ref[...] = m_new

    def region(pred, *args, **kwargs):
        @pl.when(pred)
        def _():
            stage(*args, **kwargs)

    last = pl.num_programs(2) - 1

    @pl.when(qi == 0)
    def _():
        load_queries(q_ref)

    region(qi < 1, None, 0, diag=True, first=True)

    region(qi == 1, 0, 1, diag=True)

    @pl.when(qi > 1)
    def _():
        def body(kb, carry):
            stage(kb, kb + 1)
            return carry

        lax.fori_loop(0, qi - 2, body, 0)

    region(qi >= 2, qi - 2, qi - 1, sub=True)
    region(qi - 2 >= 0, qi - 1, qi, diag=True)

    @pl.when(qi < last)
    def _():
        load_queries(qnext_ref)

    region((qi + 1 == 1) & (qi < last), qi, 0, sub=True, first=True)
    region((qi >= 1) & (qi < last), qi, 0, first=True)
    region(qi == last, qi)

    lamv = lamv_ref[...]
    lam = (jnp.exp(jnp.sum(lamv[0:1] * lamv[1:2], axis=-1, keepdims=True))
           - jnp.exp(jnp.sum(lamv[2:3] * lamv[3:4], axis=-1, keepdims=True)) + lambda_init)
    on = acc_ref[...] * (1.0 / l_ref[...])
    o_t = on[:, :tq] - lam * on[:, tq:]
    o = _rmsnorm_rows(o_t.T, subw_ref[...]) * (1.0 - lambda_init)
    o_ref[0] = o.astype(o_ref.dtype)


def diff_attention_core(proj, bias, lamv, subln_w, lambda_init):
    b, L, _ = proj.shape
    tq = _pick(L, (640, 128))
    hw = 2 * DA_HEAD
    return pl.pallas_call(
        functools.partial(_da_kernel, tq=tq, lambda_init=lambda_init),
        grid=(b, DA_HEADS, L // tq),
        in_specs=[
            pl.BlockSpec((1, tq, hw), lambda bi, h, i: (bi, i, h)),
            pl.BlockSpec((1, tq, hw), lambda bi, h, i: (bi, jnp.minimum(i + 1, L // tq - 1), h)),
            pl.BlockSpec((1, L, hw), lambda bi, h, i: (bi, 0, DA_HEADS + h)),
            pl.BlockSpec((1, L, hw), lambda bi, h, i: (bi, 0, 2 * DA_HEADS + h)),
            pl.BlockSpec((1, 2, QBLOCK, QBLOCK), lambda bi, h, i: (h, 0, 0, 0)),
            pl.BlockSpec((4, DA_HEAD), lambda bi, h, i: (0, 0)),
            pl.BlockSpec((1, hw), lambda bi, h, i: (0, 0)),
        ],
        out_specs=pl.BlockSpec((1, tq, hw), lambda bi, h, i: (bi, i, h)),
        out_shape=jax.ShapeDtypeStruct((b, L, DA_HEADS * hw), BF16),
        scratch_shapes=[pltpu.VMEM((L // tq, hw, tq), BF16), pltpu.VMEM((hw, 2 * tq), BF16),
                        pltpu.VMEM((tq, 2 * tq), F32), pltpu.VMEM((tq, 2 * tq), BF16),
                        pltpu.VMEM((1, 2 * tq), F32), pltpu.VMEM((1, 2 * tq), F32),
                        pltpu.VMEM((1, 2 * tq), F32), pltpu.VMEM((hw, 2 * tq), F32)],
        compiler_params=_params("arbitrary", "arbitrary", "arbitrary"),
        name="diff_attention",
    )(proj, proj, proj, proj, bias, lamv, subln_w.reshape(1, hw))


def diff_attention_mixer(h, nw, w_in, lam_q1, lam_k1, lam_q2, lam_k2, subln_w,
                         rel_bias, lambda_init):
    qk = DA_HEADS * 2 * DA_HEAD
    col_scale = jnp.concatenate([jnp.full((qk,), DA_Q_SCALE, F32),
                                 jnp.ones((w_in[0].shape[2] - qk,), F32)])
    proj = norm_matmul(h, nw, w_in, BF16, col_scale)
    bias = bias_tiles(rel_bias)
    lamv = jnp.stack([lam_q1, lam_k1, lam_q2, lam_k2]).astype(F32)
    return diff_attention_core(proj, bias, lamv, subln_w, lambda_init)


def _lru_kernel(gate_ref, x_ref, halo_ref, cw_ref, cb_ref, wr_ref, br_ref, wi_ref, bi_ref,
                lam_ref, o_ref, xs_ref, a_ref, b_ref, h_ref):
    i = pl.program_id(1)
    tl = x_ref.shape[1]

    @pl.when(i == 0)
    def _():
        h_ref[...] = jnp.zeros_like(h_ref)

    halo = halo_ref[0, BF16_ROWS - CONV_HALO:, :].astype(F32)
    xr = _causal_conv_rows(xs_ref, x_ref[0].astype(F32), halo, i == 0, cw_ref[...], LRU_CONV)
    xr = jnp.where(_keep_rows(i, tl), xr + cb_ref[...], 0.0)
    neg_sp = -LRU_C * jnp.logaddexp(-lam_ref[...], 0.0)
    for g in range(LRU_BLOCKS):
        cols = slice(g * LRU_BLOCK, (g + 1) * LRU_BLOCK)
        xg = xr[:, cols]
        x16 = xg.astype(BF16)
        r = jax.nn.sigmoid(_dot(x16, wr_ref[g]) + br_ref[:, cols])
        ig = jax.nn.sigmoid(_dot(x16, wi_ref[g]) + bi_ref[:, cols])
        log_a = r * neg_sp[:, cols]
        a = jnp.exp(log_a)
        inp = jnp.sqrt(jnp.maximum(-jnp.tanh(log_a) * (a * a + 1.0), 0.0)) * (ig * xg)
        for half in range(LRU_BLOCK // LANES):
            lanes = slice(half * LANES, (half + 1) * LANES)
            a_ref[g * (LRU_BLOCK // LANES) + half] = a[:, lanes]
            b_ref[g * (LRU_BLOCK // LANES) + half] = inp[:, lanes]

    groups = tl // SCAN_GROUP
    for cb in range(LRU_WIDTH // LANES):
        a_cum = a_ref[cb, pl.ds(0, groups, stride=SCAN_GROUP), :]
        b_loc = b_ref[cb, pl.ds(0, groups, stride=SCAN_GROUP), :]
        for r in range(1, SCAN_GROUP):
            rows_r = pl.ds(r, groups, stride=SCAN_GROUP)
            a_r = a_ref[cb, rows_r, :]
            b_loc = a_r * b_loc + b_ref[cb, rows_r, :]
            a_cum = a_cum * a_r
            a_ref[cb, rows_r, :] = a_cum
            b_ref[cb, rows_r, :] = b_loc

    def body(t, hprev):
        rows = pl.ds(pl.multiple_of(t * SCAN_GROUP, SCAN_GROUP), SCAN_GROUP)
        hs = b_ref[:, rows, :] + a_ref[:, rows, :] * hprev
        b_ref[:, rows, :] = hs
        return hs[:, SCAN_GROUP - 1:SCAN_GROUP, :]

    h_ref[...] = lax.fori_loop(0, groups, body, h_ref[...])
    gate = gate_ref[0].astype(F32)
    gelu = 0.5 * gate * (1.0 + jnp.tanh(math.sqrt(2.0 / math.pi)
                                        * (gate + 0.044715 * (gate * gate * gate))))
    hs_all = jnp.concatenate([b_ref[cb] for cb in range(LRU_WIDTH // LANES)], axis=1)
    o_ref[0] = (hs_all * gelu).astype(o_ref.dtype)


def lru_core(proj, conv_w, conv_b, w_r, b_r, w_i, b_i, lam):
    b, L, _ = proj.shape
    tl = _pick(L, (640, 320, 128))
    wd = LRU_WIDTH
    row = lambda a: a.astype(F32).reshape(1, wd)
    return pl.pallas_call(
        _lru_kernel,
        grid=(b, L // tl),
        in_specs=[
            pl.BlockSpec((1, tl, wd), lambda bi, i: (bi, i, 0)),
            pl.BlockSpec((1, tl, wd), lambda bi, i: (bi, i, 1)),
            pl.BlockSpec((1, BF16_ROWS, wd),
                         lambda bi, i: (bi, jnp.maximum(i * (tl // BF16_ROWS) - 1, 0), 1)),
            pl.BlockSpec((LRU_CONV, wd), lambda bi, i: (0, 0)),
            pl.BlockSpec((1, wd), lambda bi, i: (0, 0)),
            pl.BlockSpec((LRU_BLOCKS, LRU_BLOCK, LRU_BLOCK), lambda bi, i: (0, 0, 0)),
            pl.BlockSpec((1, wd), lambda bi, i: (0, 0)),
            pl.BlockSpec((LRU_BLOCKS, LRU_BLOCK, LRU_BLOCK), lambda bi, i: (0, 0, 0)),
            pl.BlockSpec((1, wd), lambda bi, i: (0, 0)),
            pl.BlockSpec((1, wd), lambda bi, i: (0, 0)),
        ],
        out_specs=pl.BlockSpec((1, tl, wd), lambda bi, i: (bi, i, 0)),
        out_shape=jax.ShapeDtypeStruct((b, L, wd), BF16),
        scratch_shapes=[pltpu.VMEM((tl + CONV_HALO, wd), F32),
                        pltpu.VMEM((wd // LANES, tl, LANES), F32),
                        pltpu.VMEM((wd // LANES, tl, LANES), F32),
                        pltpu.VMEM((wd // LANES, 1, LANES), F32)],
        compiler_params=_params("arbitrary", "arbitrary"),
        name="rglru",
    )(proj, proj, proj, conv_w, row(conv_b), w_r.astype(BF16), row(b_r), w_i.astype(BF16),
      row(b_i), row(lam))


def rglru_mixer(h, nw, w_in, conv_w, conv_b, w_r, b_r, w_i, b_i, lam):
    proj = norm_matmul(h, nw, w_in, BF16)
    return lru_core(proj, conv_w, conv_b, w_r, b_r, w_i, b_i, lam)


def kernel(x, meta_tokens, rel_bias, norm_mix_w, norm_mlp_w, final_norm_w, dn_w_in, dn_conv_w, dn_a_log, dn_dt_bias, dn_norm_w, dn_w_out, da_w_in, da_lam_q1, da_lam_k1, da_lam_q2, da_lam_k2, da_subln_w, da_w_out, lru_w_in, lru_conv_w, lru_conv_b, lru_w_rgate, lru_b_rgate, lru_w_igate, lru_b_igate, lru_lambda, lru_w_out, mlp_w1, mlp_w2):
    b = x.shape[0]
    depth = norm_mix_w.shape[0]
    h = jnp.concatenate([
        jnp.zeros((b, FRONT_PAD, D_MODEL), x.dtype),
        jnp.broadcast_to(meta_tokens[None].astype(x.dtype), (b, N_META, D_MODEL)),
        x,
    ], axis=1)
    dn_pad = jnp.zeros(dn_w_in.shape[:2] + (DN_SMALL - 2 * DN_HEADS,), dn_w_in.dtype)
    dn_w_all = jnp.concatenate([dn_w_in, dn_pad], axis=2).astype(BF16)
    dn_w_out, da_w_in, da_w_out, lru_w_in, lru_w_out, mlp_w1, mlp_w2 = (
        w.astype(BF16) for w in (dn_w_out, da_w_in, da_w_out, lru_w_in, lru_w_out, mlp_w1, mlp_w2))
    for layer in range(depth):
        kind = layer % N_MIXERS
        slot = layer // N_MIXERS
        if kind == 0:
            y = gated_deltanet_mixer(h, norm_mix_w[layer], (dn_w_all, slot), dn_conv_w[slot],
                                     dn_a_log[slot], dn_dt_bias[slot], dn_norm_w[slot])
            w_out = dn_w_out
        elif kind == 1:
            lambda_init = 0.8 - 0.6 * math.exp(-0.3 * layer)
            y = diff_attention_mixer(h, norm_mix_w[layer], (da_w_in, slot), da_lam_q1[slot],
                                     da_lam_k1[slot], da_lam_q2[slot], da_lam_k2[slot],
                                     da_subln_w[slot], rel_bias, lambda_init)
            w_out = da_w_out
        else:
            y = rglru_mixer(h, norm_mix_w[layer], (lru_w_in, slot), lru_conv_w[slot],
                            lru_conv_b[slot], lru_w_rgate[slot], lru_b_rgate[slot],
                            lru_w_igate[slot], lru_b_igate[slot], lru_lambda[slot])
            w_out = lru_w_out
        final_w = final_norm_w if layer == depth - 1 else None
        h = block_tail(y, (w_out, slot), h, norm_mlp_w[layer], (mlp_w1, layer), (mlp_w2, layer),
                       final_w)
    return h
```

```python
import functools
import math

import jax
import jax.numpy as jnp
from jax import lax
from jax.experimental import pallas as pl
from jax.experimental.pallas import tpu as pltpu

F32 = jnp.float32
BF16 = jnp.bfloat16

D_MODEL = 1024
N_META = 16
QBLOCK = 128
FRONT_PAD = QBLOCK - N_META
N_MIXERS = 3
EPS = 1e-6
CHUNK = 64
CHUNK_SHIFT = 6
CONV_HALO = 8
BF16_ROWS = 16
LANES = 128

DN_HEADS = 8
DN_HEAD = 128
DN_CONV = 4
DN_QKV = 3 * DN_HEADS * DN_HEAD
DN_MAIN = DN_QKV + DN_HEADS * DN_HEAD
DN_SMALL = 128
DN_CHUNK = 128
INV_BASE_LOG = 4

DA_HEADS = 8
DA_HEAD = 64
N_BUCKETS = 32
MAX_DISTANCE = 128
NEG_INF = -1e30
LOG2E = math.log2(math.e)
DA_Q_SCALE = DA_HEAD ** -0.5 * LOG2E

LRU_WIDTH = 1024
LRU_BLOCKS = 4
LRU_BLOCK = LRU_WIDTH // LRU_BLOCKS
LRU_CONV = 4
LRU_C = 8.0
SCAN_GROUP = 8

D_FF = 4 * D_MODEL

V7X_VMEM_LIMIT_BYTES = 56 * 1024 * 1024


def _params(*semantics):
    return pltpu.CompilerParams(dimension_semantics=semantics,
                                vmem_limit_bytes=V7X_VMEM_LIMIT_BYTES)


def _pick(n, candidates):
    for c in candidates:
        if n % c == 0:
            return c
    raise ValueError(f"no tile for {n} in {candidates}")


def _dot(a, b):
    return jnp.dot(a, b, preferred_element_type=F32)


def _dot_nt(a, b):
    return lax.dot_general(a, b, (((1,), (1,)), ((), ())), preferred_element_type=F32)


def _split2(a):
    hi = a.astype(BF16)
    lo = (a - hi.astype(F32)).astype(BF16)
    return hi, lo


def _dot3(a, b):
    ah, al = _split2(a)
    bh, bl = _split2(b)
    return _dot(ah, bh) + (_dot(ah, bl) + _dot(al, bh))


def _rmsnorm_rows(x, w):
    return x * lax.rsqrt(jnp.mean(x * x, axis=-1, keepdims=True) + EPS) * w


def _keep_rows(tile_index, rows):
    pos = tile_index * rows + lax.broadcasted_iota(jnp.int32, (rows, 1), 0)
    return pos >= FRONT_PAD


def _norm_matmul_kernel(h_ref, nw_ref, w_ref, *rest, scaled, tail):
    rest = list(rest)
    cs_ref = rest.pop(0) if scaled else None
    o_ref = rest.pop(0)
    t_ref = rest.pop(0) if tail else None
    u_ref = rest.pop(0)
    j = pl.program_id(2)

    @pl.when(j == 0)
    def _():
        u_ref[...] = _rmsnorm_rows(h_ref[0], nw_ref[...]).astype(BF16)

    y = _dot(u_ref[...], w_ref[...])
    if scaled:
        y = y * cs_ref[...]
    o_ref[0] = y.astype(o_ref.dtype)
    if tail:
        @pl.when(j == pl.num_programs(2) - 1)
        def _():
            t_ref[0] = y[:, y.shape[1] - tail:]


def norm_matmul(h, nw, w, out_dtype, col_scale=None, tail=0):
    b, L, d = h.shape
    w, slot = w
    n = w.shape[2]
    tm = _pick(L, (2080, 640, 320, 128))
    tn = _pick(n, (1408, 1024, 512, 128))
    scaled = col_scale is not None
    in_specs = [
        pl.BlockSpec((1, tm, d), lambda bi, i, j: (bi, i, 0)),
        pl.BlockSpec((1, d), lambda bi, i, j: (0, 0)),
        pl.BlockSpec((None, d, tn), lambda bi, i, j: (slot, 0, j)),
    ]
    args = [h, nw.reshape(1, d), w]
    if scaled:
        in_specs.append(pl.BlockSpec((1, tn), lambda bi, i, j: (0, j)))
        args.append(col_scale.astype(F32).reshape(1, n))
    out_specs = pl.BlockSpec((1, tm, tn), lambda bi, i, j: (bi, i, j))
    out_shape = jax.ShapeDtypeStruct((b, L, n), out_dtype)
    if tail:
        out_specs = [out_specs, pl.BlockSpec((1, tm, tail), lambda bi, i, j: (bi, i, 0))]
        out_shape = [out_shape, jax.ShapeDtypeStruct((b, L, tail), F32)]
    return pl.pallas_call(
        functools.partial(_norm_matmul_kernel, scaled=scaled, tail=tail),
        grid=(b, L // tm, n // tn),
        in_specs=in_specs,
        out_specs=out_specs,
        out_shape=out_shape,
        scratch_shapes=[pltpu.VMEM((tm, d), BF16)],
        compiler_params=_params("arbitrary", "arbitrary", "arbitrary"),
        name="norm_matmul",
    )(*args)


def _tail_kernel(y_ref, wo_ref, h_ref, nw_ref, w1_ref, w2_ref, fw_ref, o_ref,
                 u_ref, hs_ref, acc_ref, *, final_norm, chunks):
    f = pl.program_id(2)
    tm = h_ref.shape[1]

    def kept(x):
        if final_norm:
            return x
        return jnp.where(_keep_rows(pl.program_id(1), tm), x, 0.0)

    @pl.when(f == 0)
    def _():
        keep_all = None if final_norm else _keep_rows(pl.program_id(1), tm)
        step = tm // chunks
        for c in range(chunks):
            rows = slice(c * step, (c + 1) * step)
            mix = _dot(y_ref[0, rows, :], wo_ref[...])
            if keep_all is not None:
                mix = jnp.where(keep_all[rows], mix, 0.0)
            h1 = h_ref[0, rows, :] + mix
            hs_ref[rows, :] = h1
            u_ref[rows, :] = _rmsnorm_rows(h1, nw_ref[...]).astype(BF16)
        acc_ref[...] = jnp.zeros_like(acc_ref)

    a = _dot(u_ref[...], w1_ref[...])
    a = jnp.square(jnp.maximum(a, 0.0)).astype(BF16)
    acc_ref[...] += _dot(a, w2_ref[...])

    @pl.when(f == pl.num_programs(2) - 1)
    def _():
        hn = hs_ref[...] + kept(acc_ref[...])
        if final_norm:
            hn = _rmsnorm_rows(hn, fw_ref[...])
        o_ref[0] = hn


def block_tail(y, w_out, h, nw, w1, w2, final_w=None):
    b, L, d = h.shape
    (w_out, slot), (w1, layer), (w2, _) = w_out, w1, w2
    k = y.shape[-1]
    ff = w1.shape[2]
    tf = _pick(ff, (1024, 512, 128))
    final_norm = final_w is not None
    if final_norm:
        first_row = FRONT_PAD + N_META
        rows = L - first_row
        tm = _pick(rows, (1024, 512, 128))

        def row_spec(width):
            return pl.BlockSpec(
                (pl.Element(1), pl.Element(tm), pl.Element(width)),
                lambda bi, i, f: (bi, pl.multiple_of(first_row + i * tm, QBLOCK), 0))
    else:
        rows = L
        tm = _pick(L, (1040, 640, 320, 128))

        def row_spec(width):
            return pl.BlockSpec((1, tm, width), lambda bi, i, f: (bi, i, 0))
    fw = (final_w if final_norm else nw).reshape(1, d)
    chunks = _pick(tm // BF16_ROWS, (5, 4, 2, 1))
    return pl.pallas_call(
        functools.partial(_tail_kernel, final_norm=final_norm, chunks=chunks),
        grid=(b, rows // tm, ff // tf),
        in_specs=[
            row_spec(k),
            pl.BlockSpec((None, k, d), lambda bi, i, f: (slot, 0, 0)),
            row_spec(d),
            pl.BlockSpec((1, d), lambda bi, i, f: (0, 0)),
            pl.BlockSpec((None, d, tf), lambda bi, i, f: (layer, 0, f)),
            pl.BlockSpec((None, tf, d), lambda bi, i, f: (layer, f, 0)),
            pl.BlockSpec((1, d), lambda bi, i, f: (0, 0)),
        ],
        out_specs=pl.BlockSpec((1, tm, d), lambda bi, i, f: (bi, i, 0)),
        out_shape=jax.ShapeDtypeStruct((b, rows, d), F32),
        scratch_shapes=[pltpu.VMEM((tm, d), BF16), pltpu.VMEM((tm, d), F32),
                        pltpu.VMEM((tm, d), F32)],
        compiler_params=_params("arbitrary", "arbitrary", "arbitrary"),
        name="block_tail",
    )(y, w_out, h, nw.reshape(1, d), w1, w2, fw)


def _causal_conv_rows(xs_ref, cur, halo, first_tile, w, width):
    xs_ref[0:CONV_HALO, :] = jnp.where(first_tile, 0.0, halo)
    xs_ref[CONV_HALO:, :] = cur
    x = xs_ref[...]
    acc = x * w[width - 1:width, :]
    for j in range(width - 2, -1, -1):
        x = pltpu.roll(x, 1, 0)
        acc = acc + x * w[j:j + 1, :]
    return acc[CONV_HALO:, :]


def _approx_unit_lower_inverses(a_list):
    n = a_list[0].shape[0]
    row = lax.broadcasted_iota(jnp.int32, (n, n), 0)
    col = lax.broadcasted_iota(jnp.int32, (n, n), 1)

    def same_block(log_size):
        return lax.shift_right_logical(row, log_size) == lax.shift_right_logical(col, log_size)

    log_size = INV_BASE_LOG
    in_diag = same_block(log_size)
    eye = (row == col).astype(F32)
    ad = [jnp.where(in_diag, a, 0.0) for a in a_list]
    t = [eye - x for x in ad]
    bk = [x.astype(BF16) for x in ad]
    for _ in range(log_size - 1):
        bk = [_dot(x, x).astype(BF16) for x in bk]
        t = [ti + _dot(ti.astype(BF16), x) for ti, x in zip(t, bk)]
    while (1 << log_size) < n:
        sel = same_block(log_size + 1) & jnp.logical_not(same_block(log_size))
        off = [jnp.where(sel, a, 0.0).astype(BF16) for a in a_list]
        t16 = [ti.astype(BF16) for ti in t]
        left = [_dot(ti, o).astype(BF16) for ti, o in zip(t16, off)]
        t = [ti - _dot(x, ti16) for ti, x, ti16 in zip(t, left, t16)]
        log_size += 1
    return t


def _dn_chunk_kernel(q_ref, k_ref, v_ref, z_ref, s_ref, cw_ref, alog_ref, dtb_ref, nw_ref,
                     o_ref, state_ref, xs_ref):
    c = pl.program_id(0)
    n = DN_CHUNK
    nb = q_ref.shape[0]
    hd = DN_HEADS * DN_HEAD
    streams = [(b, h) for b in range(nb) for h in range(DN_HEADS)]

    @pl.when(c == 0)
    def _():
        state_ref[...] = jnp.zeros_like(state_ref)
        xs_ref[:, 0:CONV_HALO, :] = jnp.zeros((nb, CONV_HALO, 3 * hd), F32)

    for b in range(nb):
        xs_ref[b, CONV_HALO:, 0:hd] = q_ref[b].astype(F32)
        xs_ref[b, CONV_HALO:, hd:2 * hd] = k_ref[b].astype(F32)
        xs_ref[b, CONV_HALO:, 2 * hd:] = v_ref[b].astype(F32)

    def conv_act(b, g):
        lanes = slice(g * DN_HEAD, (g + 1) * DN_HEAD)
        x = xs_ref[b, :, lanes]
        acc = x * cw_ref[DN_CONV - 1:DN_CONV, lanes]
        for j in range(DN_CONV - 2, -1, -1):
            x = pltpu.roll(x, 1, 0)
            acc = acc + x * cw_ref[j:j + 1, lanes]
        acc = acc[CONV_HALO:, :]
        return acc * jax.nn.sigmoid(acc)

    def unit_rows(x):
        return x * lax.rsqrt(jnp.sum(x * x, axis=-1, keepdims=True) + EPS)

    keep = _keep_rows(c, n)
    row = lax.broadcasted_iota(jnp.int32, (n, n), 0)
    col = lax.broadcasted_iota(jnp.int32, (n, n), 1)
    incl = row >= col
    strict = row > col
    tri = incl.astype(BF16)

    beta, gc, gc_t = [], [], []
    for b in range(nb):
        small = s_ref[b]
        beta.append(jax.nn.sigmoid(small))
        sp = jnp.logaddexp(small + dtb_ref[...], 0.0)
        g = jnp.where(keep, -jnp.exp(alog_ref[...]) * sp, 0.0)
        g_hi = g.astype(BF16)
        r1 = g - g_hi.astype(F32)
        g_mid = r1.astype(BF16)
        g_lo = (r1 - g_mid.astype(F32)).astype(BF16)
        gcb = _dot(tri, g_hi) + (_dot(tri, g_mid) + _dot(tri, g_lo))
        gc.append(gcb)
        gc_t.append(gcb.T)

    def cols(h):
        return slice(h * DN_HEAD, (h + 1) * DN_HEAD)

    bcol = [beta[b][:, h:h + 1] for b, h in streams]
    gcol = [gc[b][:, DN_HEADS + h:DN_HEADS + h + 1] for b, h in streams]
    grow = [gc_t[b][DN_HEADS + h:DN_HEADS + h + 1, :] for b, h in streams]
    decay = [jnp.exp(jnp.where(incl, gi - gj, -jnp.inf)) for gi, gj in zip(gcol, grow)]
    k = [jnp.where(keep, unit_rows(conv_act(b, DN_HEADS + h)), 0.0) for b, h in streams]
    k16 = [x.astype(BF16) for x in k]
    kb = [x * bc for x, bc in zip(k, bcol)]
    a = [jnp.where(strict, _dot_nt(x.astype(BF16), y) * dc, 0.0)
         for x, y, dc in zip(kb, k16, decay)]
    t16 = [x.astype(BF16) for x in _approx_unit_lower_inverses(a)]
    a_split = [_split2(x) for x in a]

    egc = [jnp.exp(x) for x in gcol]
    s = [state_ref[i] for i in range(len(streams))]
    s16 = [x.astype(BF16) for x in s]
    rhs = [jnp.where(keep, conv_act(b, 2 * DN_HEADS + h), 0.0) * bc
           - _dot((kbi * e).astype(BF16), si)
           for (b, h), bc, kbi, e, si in zip(streams, bcol, kb, egc, s16)]
    x0 = [_dot(ti, r.astype(BF16)) for ti, r in zip(t16, rhs)]
    resid = []
    for (ah, al), x, r in zip(a_split, x0, rhs):
        xh, xl = _split2(x)
        resid.append(r - x - (_dot(ah, xh) + (_dot(ah, xl) + _dot(al, xh))))
    v_new = [x + _dot(ti, r.astype(BF16)) for x, ti, r in zip(x0, t16, resid)]
    v16 = [x.astype(BF16) for x in v_new]

    q = [unit_rows(conv_act(b, h)) * (DN_HEAD ** -0.5) for b, h in streams]
    attn = [(_dot_nt(x.astype(BF16), y) * dc).astype(BF16) for x, y, dc in zip(q, k16, decay)]
    o = [_dot((x * e).astype(BF16), si) + _dot(at, vi)
         for x, e, si, at, vi in zip(q, egc, s16, attn, v16)]
    g_last = [x[n - 1:n, :] for x in gcol]
    kdec = [(x * jnp.exp(gl - gi)).T.astype(BF16) for x, gl, gi in zip(k, g_last, gcol)]
    for i, (si, gl, kd, vi) in enumerate(zip(s, g_last, kdec, v16)):
        state_ref[i] = si * jnp.exp(gl) + _dot(kd, vi)
    for (b, h), oi in zip(streams, o):
        zh = z_ref[b, :, cols(h)].astype(F32)
        y = _rmsnorm_rows(oi, nw_ref[...]) * (zh * jax.nn.sigmoid(zh))
        o_ref[b, :, cols(h)] = y.astype(o_ref.dtype)
    xs_ref[:, 0:CONV_HALO, :] = xs_ref[:, n:n + CONV_HALO, :]


def dn_chunk(proj, small, conv_w, a_log_row, dt_bias_row, norm_w):
    b, L, _ = proj.shape
    n = DN_CHUNK
    hd = DN_HEADS * DN_HEAD
    return pl.pallas_call(
        _dn_chunk_kernel,
        grid=(L // n,),
        in_specs=[
            pl.BlockSpec((b, n, hd), lambda c: (0, c, 0)),
            pl.BlockSpec((b, n, hd), lambda c: (0, c, 1)),
            pl.BlockSpec((b, n, hd), lambda c: (0, c, 2)),
            pl.BlockSpec((b, n, hd), lambda c: (0, c, 3)),
            pl.BlockSpec((b, n, DN_SMALL), lambda c: (0, c, 0)),
            pl.BlockSpec((DN_CONV, DN_QKV), lambda c: (0, 0)),
            pl.BlockSpec((1, DN_SMALL), lambda c: (0, 0)),
            pl.BlockSpec((1, DN_SMALL), lambda c: (0, 0)),
            pl.BlockSpec((1, DN_HEAD), lambda c: (0, 0)),
        ],
        out_specs=pl.BlockSpec((b, n, hd), lambda c: (0, c, 0)),
        out_shape=jax.ShapeDtypeStruct((b, L, hd), BF16),
        scratch_shapes=[pltpu.VMEM((b * DN_HEADS, DN_HEAD, DN_HEAD), F32),
                        pltpu.VMEM((b, CONV_HALO + n, DN_QKV), F32)],
        compiler_params=_params("arbitrary"),
        name="dn_chunk",
    )(proj, proj, proj, proj, small, conv_w, a_log_row, dt_bias_row,
      norm_w.reshape(1, DN_HEAD))


def gated_deltanet_mixer(h, nw, w_all, conv_w, a_log, dt_bias, norm_w):
    lane_pad = jnp.zeros((DN_SMALL - 2 * DN_HEADS,), F32)
    head_pad = jnp.zeros((DN_HEADS,), F32)
    a_log_row = jnp.concatenate([head_pad, a_log.astype(F32), lane_pad]).reshape(1, DN_SMALL)
    dt_bias_row = jnp.concatenate([head_pad, dt_bias.astype(F32), lane_pad]).reshape(1, DN_SMALL)
    proj, small = norm_matmul(h, nw, w_all, BF16, tail=DN_SMALL)
    return dn_chunk(proj, small, conv_w, a_log_row, dt_bias_row, norm_w)


def _t5_bucket(rel):
    nb = N_BUCKETS // 2
    ret = jnp.where(rel > 0, nb, 0)
    n = jnp.abs(rel)
    max_exact = nb // 2
    nf = jnp.maximum(n, 1).astype(F32)
    large = max_exact + (jnp.log(nf / max_exact) / math.log(MAX_DISTANCE / max_exact)
                         * (nb - max_exact)).astype(jnp.int32)
    large = jnp.minimum(large, nb - 1)
    return ret + jnp.where(n < max_exact, n, large)


def _bias_tile_kernel(tab_ref, o_ref):
    h = pl.program_id(0)
    which = pl.program_id(1)
    key = lax.broadcasted_iota(jnp.int32, (QBLOCK, QBLOCK), 0)
    query = lax.broadcasted_iota(jnp.int32, (QBLOCK, QBLOCK), 1)
    bucket = _t5_bucket(key - query - QBLOCK * which)
    acc = jnp.zeros((QBLOCK, QBLOCK), F32)
    for bkt in range(N_BUCKETS):
        acc = jnp.where(bucket == bkt, tab_ref[bkt, h], acc)
    o_ref[0, 0] = (acc - tab_ref[N_BUCKETS // 2 - 1, h]) * LOG2E


def bias_tiles(rel_bias):
    return pl.pallas_call(
        _bias_tile_kernel,
        grid=(DA_HEADS, 2),
        in_specs=[pl.BlockSpec(memory_space=pltpu.SMEM)],
        out_specs=pl.BlockSpec((1, 1, QBLOCK, QBLOCK), lambda h, w: (h, w, 0, 0)),
        out_shape=jax.ShapeDtypeStruct((DA_HEADS, 2, QBLOCK, QBLOCK), F32),
        compiler_params=_params("arbitrary", "arbitrary"),
        name="bias_tiles",
    )(rel_bias.astype(F32))


def _da_kernel(q_ref, qnext_ref, k_ref, v_ref, bias_ref, lamv_ref, subw_ref, o_ref,
               vt_ref, qt_ref, s_ref, p_ref, mblk_ref, m_ref, l_ref, acc_ref,
               *, tq, lambda_init):
    qi = pl.program_id(2)
    nsub = tq // QBLOCK
    hw = 2 * DA_HEAD
    n_blocks = k_ref.shape[1] // tq

    @pl.when(qi == 0)
    def _():
        def prep(t, carry):
            rows = pl.ds(pl.multiple_of(t * tq, tq), tq)
            vt_ref[t] = v_ref[0, rows, :].astype(F32).T.astype(BF16)
            return carry

        lax.fori_loop(0, n_blocks, prep, 0)

    def load_queries(src_ref):
        feat = lax.broadcasted_iota(jnp.int32, (hw, 1), 0)
        q_t = src_ref[0].astype(F32).T
        qt_ref[:, :tq] = jnp.where(feat < DA_HEAD, q_t, 0.0).astype(BF16)
        qt_ref[:, tq:] = jnp.where(feat >= DA_HEAD, q_t, 0.0).astype(BF16)

    m_ref[...] = jnp.full(m_ref.shape, NEG_INF, F32)
    l_ref[...] = jnp.zeros_like(l_ref)
    acc_ref[...] = jnp.zeros_like(acc_ref)

    def sub_rows(j):
        return slice(j * QBLOCK, (j + 1) * QBLOCK)

    def near_terms(s_half, j, diag, sub):
        tiles = [s_half[:, sub_rows(qq)] for qq in range(nsub)]
        if sub and j == nsub - 1:
            tiles[0] = tiles[0] + bias_ref[0, 1]
        if diag:
            key = lax.broadcasted_iota(jnp.int32, (QBLOCK, QBLOCK), 0)
            query = lax.broadcasted_iota(jnp.int32, (QBLOCK, QBLOCK), 1)
            allowed = (lax.shift_right_logical(key, CHUNK_SHIFT)
                       <= lax.shift_right_logical(query, CHUNK_SHIFT))
            for qq in range(j):
                tiles[qq] = jnp.full((QBLOCK, QBLOCK), NEG_INF, F32)
            tiles[j] = jnp.where(allowed, tiles[j] + bias_ref[0, 0], NEG_INF)
            if j + 1 < nsub:
                tiles[j + 1] = tiles[j + 1] + bias_ref[0, 1]
        return jnp.concatenate(tiles, axis=1)

    def scores_part(kb, j0, nj, diag, sub, first):
        rows = pl.ds(pl.multiple_of(kb * tq + j0 * QBLOCK, QBLOCK), nj * QBLOCK)
        s = _dot(k_ref[0, rows, :], qt_ref[...])
        pieces = []
        for jj in range(nj):
            j = j0 + jj
            piece = s[sub_rows(jj), :]
            if diag or (sub and j == nsub - 1):
                piece = jnp.concatenate([near_terms(piece[:, :tq], j, diag, sub),
                                         near_terms(piece[:, tq:], j, diag, sub)], axis=1)
            if first and j == 0:
                valid = lax.broadcasted_iota(jnp.int32, (QBLOCK, 1), 0) >= FRONT_PAD
                piece = jnp.where(valid, piece, NEG_INF)
            pieces.append(piece)
        s = pieces[0] if nj == 1 else jnp.concatenate(pieces, axis=0)
        s_ref[j0 * QBLOCK:(j0 + nj) * QBLOCK, :] = s
        return jnp.max(s.reshape(nj * QBLOCK // 8, 8, 2 * tq), axis=0)

    parts_after = {min(j0 + 1, nsub - 1): (j0, min(2, nsub - j0)) for j0 in range(0, nsub, 2)}

    def stage(kb, nxt=None, diag=False, sub=False, first=False):
        if kb is not None:
            m_old = m_ref[...]
            m_new = jnp.maximum(m_old, mblk_ref[...])
        lsum = None
        running = None
        for j in range(nsub):
            if kb is not None:
                p = jnp.exp2(s_ref[sub_rows(j), :] - m_new)
                lj = jnp.sum(p.reshape(QBLOCK // 8, 8, 2 * tq), axis=0)
                lsum = lj if lsum is None else lsum + lj
                p_ref[sub_rows(j), :] = p.astype(BF16)
            if nxt is not None and j in parts_after:
                mj = scores_part(nxt, *parts_after[j], diag, sub, first)
                running = mj if running is None else jnp.maximum(running, mj)
        if nxt is not None:
            mblk_ref[...] = jnp.max(running, axis=0, keepdims=True)
        if kb is not None:
            alpha = jnp.exp2(m_old - m_new)
            l_ref[...] = alpha * l_ref[...] + jnp.sum(lsum, axis=0, keepdims=True)
            acc_ref[...] = alpha * acc_ref[...] + _dot(vt_ref[kb], p_ref[...])
            m_ref[...] = m_new

    def region(pred, *args, **kwargs):
        @pl.when(pred)
        def _():
            stage(*args, **kwargs)

    last = pl.num_programs(2) - 1

    @pl.when(qi == 0)
    def _():
        load_queries(q_ref)

    region(qi < 1, None, 0, diag=True, first=True)

    region(qi == 1, 0, 1, diag=True)

    @pl.when(qi > 1)
    def _():
        def body(kb, carry):
            stage(kb, kb + 1)
            return carry

        lax.fori_loop(0, qi - 2, body, 0)

    region(qi >= 2, qi - 2, qi - 1, sub=True)
    region(qi - 2 >= 0, qi - 1, qi, diag=True)

    @pl.when(qi < last)
    def _():
        load_queries(qnext_ref)

    region((qi + 1 == 1) & (qi < last), qi, 0, sub=True, first=True)
    region((qi >= 1) & (qi < last), qi, 0, first=True)
    region(qi == last, qi)

    lamv = lamv_ref[...]
    lam = (jnp.exp(jnp.sum(lamv[0:1] * lamv[1:2], axis=-1, keepdims=True))
           - jnp.exp(jnp.sum(lamv[2:3] * lamv[3:4], axis=-1, keepdims=True)) + lambda_init)
    on = acc_ref[...] * (1.0 / l_ref[...])
    o_t = on[:, :tq] - lam * on[:, tq:]
    o = _rmsnorm_rows(o_t.T, subw_ref[...]) * (1.0 - lambda_init)
    o_ref[0] = o.astype(o_ref.dtype)


def diff_attention_core(proj, bias, lamv, subln_w, lambda_init):
    b, L, _ = proj.shape
    tq = _pick(L, (640, 128))
    hw = 2 * DA_HEAD
    return pl.pallas_call(
        functools.partial(_da_kernel, tq=tq, lambda_init=lambda_init),
        grid=(b, DA_HEADS, L // tq),
        in_specs=[
            pl.BlockSpec((1, tq, hw), lambda bi, h, i: (bi, i, h)),
            pl.BlockSpec((1, tq, hw), lambda bi, h, i: (bi, jnp.minimum(i + 1, L // tq - 1), h)),
            pl.BlockSpec((1, L, hw), lambda bi, h, i: (bi, 0, DA_HEADS + h)),
            pl.BlockSpec((1, L, hw), lambda bi, h, i: (bi, 0, 2 * DA_HEADS + h)),
            pl.BlockSpec((1, 2, QBLOCK, QBLOCK), lambda bi, h, i: (h, 0, 0, 0)),
            pl.BlockSpec((4, DA_HEAD), lambda bi, h, i: (0, 0)),
            pl.BlockSpec((1, hw), lambda bi, h, i: (0, 0)),
        ],
        out_specs=pl.BlockSpec((1, tq, hw), lambda bi, h, i: (bi, i, h)),
        out_shape=jax.ShapeDtypeStruct((b, L, DA_HEADS * hw), BF16),
        scratch_shapes=[pltpu.VMEM((L // tq, hw, tq), BF16), pltpu.VMEM((hw, 2 * tq), BF16),
                        pltpu.VMEM((tq, 2 * tq), F32), pltpu.VMEM((tq, 2 * tq), BF16),
                        pltpu.VMEM((1, 2 * tq), F32), pltpu.VMEM((1, 2 * tq), F32),
                        pltpu.VMEM((1, 2 * tq), F32), pltpu.VMEM((hw, 2 * tq), F32)],
        compiler_params=_params("arbitrary", "arbitrary", "arbitrary"),
        name="diff_attention",
    )(proj, proj, proj, proj, bias, lamv, subln_w.reshape(1, hw))


def diff_attention_mixer(h, nw, w_in, lam_q1, lam_k1, lam_q2, lam_k2, subln_w,
                         rel_bias, lambda_init):
    qk = DA_HEADS * 2 * DA_HEAD
    col_scale = jnp.concatenate([jnp.full((qk,), DA_Q_SCALE, F32),
                                 jnp.ones((w_in[0].shape[2] - qk,), F32)])
    proj = norm_matmul(h, nw, w_in, BF16, col_scale)
    bias = bias_tiles(rel_bias)
    lamv = jnp.stack([lam_q1, lam_k1, lam_q2, lam_k2]).astype(F32)
    return diff_attention_core(proj, bias, lamv, subln_w, lambda_init)


def _lru_kernel(gate_ref, x_ref, halo_ref, cw_ref, cb_ref, wr_ref, br_ref, wi_ref, bi_ref,
                lam_ref, o_ref, xs_ref, a_ref, b_ref, h_ref):
    i = pl.program_id(1)
    tl = x_ref.shape[1]

    @pl.when(i == 0)
    def _():
        h_ref[...] = jnp.zeros_like(h_ref)

    halo = halo_ref[0, BF16_ROWS - CONV_HALO:, :].astype(F32)
    xr = _causal_conv_rows(xs_ref, x_ref[0].astype(F32), halo, i == 0, cw_ref[...], LRU_CONV)
    xr = jnp.where(_keep_rows(i, tl), xr + cb_ref[...], 0.0)
    neg_sp = -LRU_C * jnp.logaddexp(-lam_ref[...], 0.0)
    for g in range(LRU_BLOCKS):
        cols = slice(g * LRU_BLOCK, (g + 1) * LRU_BLOCK)
        xg = xr[:, cols]
        x16 = xg.astype(BF16)
        r = jax.nn.sigmoid(_dot(x16, wr_ref[g]) + br_ref[:, cols])
        ig = jax.nn.sigmoid(_dot(x16, wi_ref[g]) + bi_ref[:, cols])
        log_a = r * neg_sp[:, cols]
        a = jnp.exp(log_a)
        inp = jnp.sqrt(jnp.maximum(-jnp.tanh(log_a) * (a * a + 1.0), 0.0)) * (ig * xg)
        for half in range(LRU_BLOCK // LANES):
            lanes = slice(half * LANES, (half + 1) * LANES)
            a_ref[g * (LRU_BLOCK // LANES) + half] = a[:, lanes]
            b_ref[g * (LRU_BLOCK // LANES) + half] = inp[:, lanes]

    groups = tl // SCAN_GROUP
    for cb in range(LRU_WIDTH // LANES):
        a_cum = a_ref[cb, pl.ds(0, groups, stride=SCAN_GROUP), :]
        b_loc = b_ref[cb, pl.ds(0, groups, stride=SCAN_GROUP), :]
        for r in range(1, SCAN_GROUP):
            rows_r = pl.ds(r, groups, stride=SCAN_GROUP)
            a_r = a_ref[cb, rows_r, :]
            b_loc = a_r * b_loc + b_ref[cb, rows_r, :]
            a_cum = a_cum * a_r
            a_ref[cb, rows_r, :] = a_cum
            b_ref[cb, rows_r, :] = b_loc

    def body(t, hprev):
        rows = pl.ds(pl.multiple_of(t * SCAN_GROUP, SCAN_GROUP), SCAN_GROUP)
        hs = b_ref[:, rows, :] + a_ref[:, rows, :] * hprev
        b_ref[:, rows, :] = hs
        return hs[:, SCAN_GROUP - 1:SCAN_GROUP, :]

    h_ref[...] = lax.fori_loop(0, groups, body, h_ref[...])
    gate = gate_ref[0].astype(F32)
    gelu = 0.5 * gate * (1.0 + jnp.tanh(math.sqrt(2.0 / math.pi)
                                        * (gate + 0.044715 * (gate * gate * gate))))
    hs_all = jnp.concatenate([b_ref[cb] for cb in range(LRU_WIDTH // LANES)], axis=1)
    o_ref[0] = (hs_all * gelu).astype(o_ref.dtype)


def lru_core(proj, conv_w, conv_b, w_r, b_r, w_i, b_i, lam):
    b, L, _ = proj.shape
    tl = _pick(L, (640, 320, 128))
    wd = LRU_WIDTH
    row = lambda a: a.astype(F32).reshape(1, wd)
    return pl.pallas_call(
        _lru_kernel,
        grid=(b, L // tl),
        in_specs=[
            pl.BlockSpec((1, tl, wd), lambda bi, i: (bi, i, 0)),
            pl.BlockSpec((1, tl, wd), lambda bi, i: (bi, i, 1)),
            pl.BlockSpec((1, BF16_ROWS, wd),
                         lambda bi, i: (bi, jnp.maximum(i * (tl // BF16_ROWS) - 1, 0), 1)),
            pl.BlockSpec((LRU_CONV, wd), lambda bi, i: (0, 0)),
            pl.BlockSpec((1, wd), lambda bi, i: (0, 0)),
            pl.BlockSpec((LRU_BLOCKS, LRU_BLOCK, LRU_BLOCK), lambda bi, i: (0, 0, 0)),
            pl.BlockSpec((1, wd), lambda bi, i: (0, 0)),
            pl.BlockSpec((LRU_BLOCKS, LRU_BLOCK, LRU_BLOCK), lambda bi, i: (0, 0, 0)),
            pl.BlockSpec((1, wd), lambda bi, i: (0, 0)),
            pl.BlockSpec((1, wd), lambda bi, i: (0, 0)),
        ],
        out_specs=pl.BlockSpec((1, tl, wd), lambda bi, i: (bi, i, 0)),
        out_shape=jax.ShapeDtypeStruct((b, L, wd), BF16),
        scratch_shapes=[pltpu.VMEM((tl + CONV_HALO, wd), F32),
                        pltpu.VMEM((wd // LANES, tl, LANES), F32),
                        pltpu.VMEM((wd // LANES, tl, LANES), F32),
                        pltpu.VMEM((wd // LANES, 1, LANES), F32)],
        compiler_params=_params("arbitrary", "arbitrary"),
        name="rglru",
    )(proj, proj, proj, conv_w, row(conv_b), w_r.astype(BF16), row(b_r), w_i.astype(BF16),
      row(b_i), row(lam))


def rglru_mixer(h, nw, w_in, conv_w, conv_b, w_r, b_r, w_i, b_i, lam):
    proj = norm_matmul(h, nw, w_in, BF16)
    return lru_core(proj, conv_w, conv_b, w_r, b_r, w_i, b_i, lam)


def kernel(x, meta_tokens, rel_bias, norm_mix_w, norm_mlp_w, final_norm_w, dn_w_in, dn_conv_w, dn_a_log, dn_dt_bias, dn_norm_w, dn_w_out, da_w_in, da_lam_q1, da_lam_k1, da_lam_q2, da_lam_k2, da_subln_w, da_w_out, lru_w_in, lru_conv_w, lru_conv_b, lru_w_rgate, lru_b_rgate, lru_w_igate, lru_b_igate, lru_lambda, lru_w_out, mlp_w1, mlp_w2):
    b = x.shape[0]
    depth = norm_mix_w.shape[0]
    h = jnp.concatenate([
        jnp.zeros((b, FRONT_PAD, D_MODEL), x.dtype),
        jnp.broadcast_to(meta_tokens[None].astype(x.dtype), (b, N_META, D_MODEL)),
        x,
    ], axis=1)
    dn_pad = jnp.zeros(dn_w_in.shape[:2] + (DN_SMALL - 2 * DN_HEADS,), dn_w_in.dtype)
    dn_w_all = jnp.concatenate([dn_w_in, dn_pad], axis=2).astype(BF16)
    dn_w_out, da_w_in, da_w_out, lru_w_in, lru_w_out, mlp_w1, mlp_w2 = (
        w.astype(BF16) for w in (dn_w_out, da_w_in, da_w_out, lru_w_in, lru_w_out, mlp_w1, mlp_w2))
    for layer in range(depth):
        kind = layer % N_MIXERS
        slot = layer // N_MIXERS
        if kind == 0:
            y = gated_deltanet_mixer(h, norm_mix_w[layer], (dn_w_all, slot), dn_conv_w[slot],
                                     dn_a_log[slot], dn_dt_bias[slot], dn_norm_w[slot])
            w_out = dn_w_out
        elif kind == 1:
            lambda_init = 0.8 - 0.6 * math.exp(-0.3 * layer)
            y = diff_attention_mixer(h, norm_mix_w[layer], (da_w_in, slot), da_lam_q1[slot],
                                     da_lam_k1[slot], da_lam_q2[slot], da_lam_k2[slot],
                                     da_subln_w[slot], rel_bias, lambda_init)
            w_out = da_w_out
        else:
            y = rglru_mixer(h, norm_mix_w[layer], (lru_w_in, slot), lru_conv_w[slot],
                            lru_conv_b[slot], lru_w_rgate[slot], lru_b_rgate[slot],
                            lru_w_igate[slot], lru_b_igate[slot], lru_lambda[slot])
            w_out = lru_w_out
        final_w = final_norm_w if layer == depth - 1 else None
        h = block_tail(y, (w_out, slot), h, norm_mlp_w[layer], (mlp_w1, layer), (mlp_w2, layer),
                       final_w)
    return h
```

```python
import functools
import math

import jax
import jax.numpy as jnp
from jax import lax
from jax.experimental import pallas as pl
from jax.experimental.pallas import tpu as pltpu

F32 = jnp.float32
BF16 = jnp.bfloat16

D_MODEL = 1024
N_META = 16
QBLOCK = 128
FRONT_PAD = QBLOCK - N_META
N_MIXERS = 3
EPS = 1e-6
CHUNK = 64
CHUNK_SHIFT = 6
SUBLANES = 8
LANES = 128
BF16_ROWS = 2 * SUBLANES
CONV_HALO = SUBLANES

DN_HEADS = 8
DN_HEAD = 128
DN_CONV = 4
DN_QKV = 3 * DN_HEADS * DN_HEAD
DN_SMALL = 128
DN_CHUNK = 128
INV_BASE_LOG = 4

DA_HEADS = 8
DA_HEAD = 64
N_BUCKETS = 32
MAX_DISTANCE = 128
NEG_INF = -1e30
LOG2E = math.log2(math.e)
DA_Q_SCALE = DA_HEAD ** -0.5 * LOG2E

LRU_WIDTH = 1024
LRU_BLOCKS = 4
LRU_BLOCK = LRU_WIDTH // LRU_BLOCKS
LRU_CONV = 4
LRU_C = 8.0
SCAN_GROUP = SUBLANES

V7X_VMEM_LIMIT_BYTES = 56 * 1024 * 1024


def _params(*semantics):
    return pltpu.CompilerParams(dimension_semantics=semantics,
                                vmem_limit_bytes=V7X_VMEM_LIMIT_BYTES)


def _pick(n, candidates):
    for c in candidates:
        if n % c == 0:
            return c
    raise ValueError(f"no tile for {n} in {candidates}")


def _dot(a, b):
    return jnp.dot(a, b, preferred_element_type=F32)


def _dot_nt(a, b):
    return lax.dot_general(a, b, (((1,), (1,)), ((), ())), preferred_element_type=F32)


def _split2(a):
    hi = a.astype(BF16)
    lo = (a - hi.astype(F32)).astype(BF16)
    return hi, lo


def _rmsnorm_rows(x, w):
    return x * lax.rsqrt(jnp.mean(x * x, axis=-1, keepdims=True) + EPS) * w


def _keep_rows(tile_index, rows):
    pos = tile_index * rows + lax.broadcasted_iota(jnp.int32, (rows, 1), 0)
    return pos >= FRONT_PAD


def _norm_matmul_kernel(h_ref, nw_ref, w_ref, *rest, scaled, tail):
    rest = list(rest)
    cs_ref = rest.pop(0) if scaled else None
    o_ref = rest.pop(0)
    t_ref = rest.pop(0) if tail else None
    u_ref = rest.pop(0)
    j = pl.program_id(2)

    @pl.when(j == 0)
    def _():
        u_ref[...] = _rmsnorm_rows(h_ref[0], nw_ref[...]).astype(BF16)

    y = _dot(u_ref[...], w_ref[...])
    if scaled:
        y = y * cs_ref[...]
    o_ref[0] = y.astype(o_ref.dtype)
    if tail:
        @pl.when(j == pl.num_programs(2) - 1)
        def _():
            t_ref[0] = y[:, y.shape[1] - tail:]


def norm_matmul(h, nw, w, out_dtype, col_scale=None, tail=0):
    b, L, d = h.shape
    w, slot = w
    n = w.shape[2]
    tm = _pick(L, (2080, 640, 320, 128))
    tn = _pick(n, (1408, 1024, 512, 128))
    scaled = col_scale is not None
    in_specs = [
        pl.BlockSpec((1, tm, d), lambda bi, i, j: (bi, i, 0)),
        pl.BlockSpec((1, d), lambda bi, i, j: (0, 0)),
        pl.BlockSpec((None, d, tn), lambda bi, i, j: (slot, 0, j)),
    ]
    args = [h, nw.reshape(1, d), w]
    if scaled:
        in_specs.append(pl.BlockSpec((1, tn), lambda bi, i, j: (0, j)))
        args.append(col_scale.astype(F32).reshape(1, n))
    out_specs = pl.BlockSpec((1, tm, tn), lambda bi, i, j: (bi, i, j))
    out_shape = jax.ShapeDtypeStruct((b, L, n), out_dtype)
    if tail:
        out_specs = [out_specs, pl.BlockSpec((1, tm, tail), lambda bi, i, j: (bi, i, 0))]
        out_shape = [out_shape, jax.ShapeDtypeStruct((b, L, tail), F32)]
    return pl.pallas_call(
        functools.partial(_norm_matmul_kernel, scaled=scaled, tail=tail),
        grid=(b, L // tm, n // tn),
        in_specs=in_specs,
        out_specs=out_specs,
        out_shape=out_shape,
        scratch_shapes=[pltpu.VMEM((tm, d), BF16)],
        compiler_params=_params("arbitrary", "arbitrary", "arbitrary"),
        name="norm_matmul",
    )(*args)


def _tail_kernel(y_ref, wo_ref, h_ref, nw_ref, w1_ref, w2_ref, fw_ref, o_ref,
                 u_ref, hs_ref, acc_ref, *, final_norm, chunks):
    f = pl.program_id(2)
    tm = h_ref.shape[1]

    def kept(x):
        if final_norm:
            return x
        return jnp.where(_keep_rows(pl.program_id(1), tm), x, 0.0)

    @pl.when(f == 0)
    def _():
        keep_all = None if final_norm else _keep_rows(pl.program_id(1), tm)
        step = tm // chunks
        for c in range(chunks):
            rows = slice(c * step, (c + 1) * step)
            mix = _dot(y_ref[0, rows, :], wo_ref[...])
            if keep_all is not None:
                mix = jnp.where(keep_all[rows], mix, 0.0)
            h1 = h_ref[0, rows, :] + mix
            hs_ref[rows, :] = h1
            u_ref[rows, :] = _rmsnorm_rows(h1, nw_ref[...]).astype(BF16)
        acc_ref[...] = jnp.zeros_like(acc_ref)

    a = _dot(u_ref[...], w1_ref[...])
    a = jnp.square(jnp.maximum(a, 0.0)).astype(BF16)
    acc_ref[...] += _dot(a, w2_ref[...])

    @pl.when(f == pl.num_programs(2) - 1)
    def _():
        hn = hs_ref[...] + kept(acc_ref[...])
        if final_norm:
            hn = _rmsnorm_rows(hn, fw_ref[...])
        o_ref[0] = hn


def block_tail(y, w_out, h, nw, w1, w2, final_w=None):
    b, L, d = h.shape
    (w_out, slot), (w1, layer), (w2, _) = w_out, w1, w2
    k = y.shape[-1]
    ff = w1.shape[2]
    tf = _pick(ff, (1024, 512, 128))
    final_norm = final_w is not None
    if final_norm:
        first_row = FRONT_PAD + N_META
        rows = L - first_row
        tm = _pick(rows, (1024, 512, 128))

        def row_spec(width):
            return pl.BlockSpec(
                (pl.Element(1), pl.Element(tm), pl.Element(width)),
                lambda bi, i, f: (bi, pl.multiple_of(first_row + i * tm, QBLOCK), 0))
    else:
        rows = L
        tm = _pick(L, (1040, 640, 320, 128))

        def row_spec(width):
            return pl.BlockSpec((1, tm, width), lambda bi, i, f: (bi, i, 0))
    fw = (final_w if final_norm else nw).reshape(1, d)
    chunks = _pick(tm // BF16_ROWS, (5, 4, 2, 1))
    return pl.pallas_call(
        functools.partial(_tail_kernel, final_norm=final_norm, chunks=chunks),
        grid=(b, rows // tm, ff // tf),
        in_specs=[
            row_spec(k),
            pl.BlockSpec((None, k, d), lambda bi, i, f: (slot, 0, 0)),
            row_spec(d),
            pl.BlockSpec((1, d), lambda bi, i, f: (0, 0)),
            pl.BlockSpec((None, d, tf), lambda bi, i, f: (layer, 0, f)),
            pl.BlockSpec((None, tf, d), lambda bi, i, f: (layer, f, 0)),
            pl.BlockSpec((1, d), lambda bi, i, f: (0, 0)),
        ],
        out_specs=pl.BlockSpec((1, tm, d), lambda bi, i, f: (bi, i, 0)),
        out_shape=jax.ShapeDtypeStruct((b, rows, d), F32),
        scratch_shapes=[pltpu.VMEM((tm, d), BF16), pltpu.VMEM((tm, d), F32),
                        pltpu.VMEM((tm, d), F32)],
        compiler_params=_params("arbitrary", "arbitrary", "arbitrary"),
        name="block_tail",
    )(y, w_out, h, nw.reshape(1, d), w1, w2, fw)


def _causal_conv_rows(xs_ref, cur, halo, first_tile, w, width):
    xs_ref[0:CONV_HALO, :] = jnp.where(first_tile, 0.0, halo)
    xs_ref[CONV_HALO:, :] = cur
    x = xs_ref[...]
    acc = x * w[width - 1:width, :]
    for j in range(width - 2, -1, -1):
        x = pltpu.roll(x, 1, 0)
        acc = acc + x * w[j:j + 1, :]
    return acc[CONV_HALO:, :]


def _approx_unit_lower_inverses(a_list):
    n = a_list[0].shape[0]
    row = lax.broadcasted_iota(jnp.int32, (n, n), 0)
    col = lax.broadcasted_iota(jnp.int32, (n, n), 1)

    def same_block(log_size):
        return lax.shift_right_logical(row, log_size) == lax.shift_right_logical(col, log_size)

    log_size = INV_BASE_LOG
    in_diag = same_block(log_size)
    eye = (row == col).astype(F32)
    ad = [jnp.where(in_diag, a, 0.0) for a in a_list]
    t = [eye - x for x in ad]
    bk = [x.astype(BF16) for x in ad]
    for _ in range(log_size - 1):
        bk = [_dot(x, x).astype(BF16) for x in bk]
        t = [ti + _dot(ti.astype(BF16), x) for ti, x in zip(t, bk)]
    while (1 << log_size) < n:
        sel = same_block(log_size + 1) & jnp.logical_not(same_block(log_size))
        off = [jnp.where(sel, a, 0.0).astype(BF16) for a in a_list]
        t16 = [ti.astype(BF16) for ti in t]
        left = [_dot(ti, o).astype(BF16) for ti, o in zip(t16, off)]
        t = [ti - _dot(x, ti16) for ti, x, ti16 in zip(t, left, t16)]
        log_size += 1
    return t


def _dn_chunk_kernel(q_ref, k_ref, v_ref, z_ref, s_ref, cw_ref, alog_ref, dtb_ref, nw_ref,
                     o_ref, state_ref, xs_ref):
    c = pl.program_id(0)
    n = DN_CHUNK
    nb = q_ref.shape[0]
    hd = DN_HEADS * DN_HEAD
    streams = [(b, h) for b in range(nb) for h in range(DN_HEADS)]

    @pl.when(c == 0)
    def _():
        state_ref[...] = jnp.zeros_like(state_ref)
        xs_ref[:, 0:CONV_HALO, :] = jnp.zeros((nb, CONV_HALO, 3 * hd), F32)

    for b in range(nb):
        xs_ref[b, CONV_HALO:, 0:hd] = q_ref[b].astype(F32)
        xs_ref[b, CONV_HALO:, hd:2 * hd] = k_ref[b].astype(F32)
        xs_ref[b, CONV_HALO:, 2 * hd:] = v_ref[b].astype(F32)

    def conv_act(b, g):
        lanes = slice(g * DN_HEAD, (g + 1) * DN_HEAD)
        x = xs_ref[b, :, lanes]
        acc = x * cw_ref[DN_CONV - 1:DN_CONV, lanes]
        for j in range(DN_CONV - 2, -1, -1):
            x = pltpu.roll(x, 1, 0)
            acc = acc + x * cw_ref[j:j + 1, lanes]
        acc = acc[CONV_HALO:, :]
        return acc * jax.nn.sigmoid(acc)

    def unit_rows(x):
        return x * lax.rsqrt(jnp.sum(x * x, axis=-1, keepdims=True) + EPS)

    keep = _keep_rows(c, n)
    row = lax.broadcasted_iota(jnp.int32, (n, n), 0)
    col = lax.broadcasted_iota(jnp.int32, (n, n), 1)
    incl = row >= col
    strict = row > col
    tri = incl.astype(BF16)

    beta, gc, gc_t = [], [], []
    for b in range(nb):
        small = s_ref[b]
        beta.append(jax.nn.sigmoid(small))
        sp = jnp.logaddexp(small + dtb_ref[...], 0.0)
        g = jnp.where(keep, -jnp.exp(alog_ref[...]) * sp, 0.0)
        g_hi = g.astype(BF16)
        r1 = g - g_hi.astype(F32)
        g_mid = r1.astype(BF16)
        g_lo = (r1 - g_mid.astype(F32)).astype(BF16)
        gcb = _dot(tri, g_hi) + (_dot(tri, g_mid) + _dot(tri, g_lo))
        gc.append(gcb)
        gc_t.append(gcb.T)

    def cols(h):
        return slice(h * DN_HEAD, (h + 1) * DN_HEAD)

    bcol = [beta[b][:, h:h + 1] for b, h in streams]
    gcol = [gc[b][:, DN_HEADS + h:DN_HEADS + h + 1] for b, h in streams]
    grow = [gc_t[b][DN_HEADS + h:DN_HEADS + h + 1, :] for b, h in streams]
    decay = [jnp.exp(jnp.where(incl, gi - gj, -jnp.inf)) for gi, gj in zip(gcol, grow)]
    k = [jnp.where(keep, unit_rows(conv_act(b, DN_HEADS + h)), 0.0) for b, h in streams]
    k16 = [x.astype(BF16) for x in k]
    kb = [x * bc for x, bc in zip(k, bcol)]
    a = [jnp.where(strict, _dot_nt(x.astype(BF16), y) * dc, 0.0)
         for x, y, dc in zip(kb, k16, decay)]
    t16 = [x.astype(BF16) for x in _approx_unit_lower_inverses(a)]
    a_split = [_split2(x) for x in a]

    egc = [jnp.exp(x) for x in gcol]
    s = [state_ref[i] for i in range(len(streams))]
    s16 = [x.astype(BF16) for x in s]
    rhs = [jnp.where(keep, conv_act(b, 2 * DN_HEADS + h), 0.0) * bc
           - _dot((kbi * e).astype(BF16), si)
           for (b, h), bc, kbi, e, si in zip(streams, bcol, kb, egc, s16)]
    x0 = [_dot(ti, r.astype(BF16)) for ti, r in zip(t16, rhs)]
    resid = []
    for (ah, al), x, r in zip(a_split, x0, rhs):
        xh, xl = _split2(x)
        resid.append(r - x - (_dot(ah, xh) + (_dot(ah, xl) + _dot(al, xh))))
    v_new = [x + _dot(ti, r.astype(BF16)) for x, ti, r in zip(x0, t16, resid)]
    v16 = [x.astype(BF16) for x in v_new]

    q = [unit_rows(conv_act(b, h)) * (DN_HEAD ** -0.5) for b, h in streams]
    attn = [(_dot_nt(x.astype(BF16), y) * dc).astype(BF16) for x, y, dc in zip(q, k16, decay)]
    o = [_dot((x * e).astype(BF16), si) + _dot(at, vi)
         for x, e, si, at, vi in zip(q, egc, s16, attn, v16)]
    g_last = [x[n - 1:n, :] for x in gcol]
    kdec = [(x * jnp.exp(gl - gi)).T.astype(BF16) for x, gl, gi in zip(k, g_last, gcol)]
    for i, (si, gl, kd, vi) in enumerate(zip(s, g_last, kdec, v16)):
        state_ref[i] = si * jnp.exp(gl) + _dot(kd, vi)
    for (b, h), oi in zip(streams, o):
        zh = z_ref[b, :, cols(h)].astype(F32)
        y = _rmsnorm_rows(oi, nw_ref[...]) * (zh * jax.nn.sigmoid(zh))
        o_ref[b, :, cols(h)] = y.astype(o_ref.dtype)
    xs_ref[:, 0:CONV_HALO, :] = xs_ref[:, n:n + CONV_HALO, :]


def dn_chunk(proj, small, conv_w, a_log_row, dt_bias_row, norm_w):
    b, L, _ = proj.shape
    n = DN_CHUNK
    hd = DN_HEADS * DN_HEAD
    return pl.pallas_call(
        _dn_chunk_kernel,
        grid=(L // n,),
        in_specs=[
            pl.BlockSpec((b, n, hd), lambda c: (0, c, 0)),
            pl.BlockSpec((b, n, hd), lambda c: (0, c, 1)),
            pl.BlockSpec((b, n, hd), lambda c: (0, c, 2)),
            pl.BlockSpec((b, n, hd), lambda c: (0, c, 3)),
            pl.BlockSpec((b, n, DN_SMALL), lambda c: (0, c, 0)),
            pl.BlockSpec((DN_CONV, DN_QKV), lambda c: (0, 0)),
            pl.BlockSpec((1, DN_SMALL), lambda c: (0, 0)),
            pl.BlockSpec((1, DN_SMALL), lambda c: (0, 0)),
            pl.BlockSpec((1, DN_HEAD), lambda c: (0, 0)),
        ],
        out_specs=pl.BlockSpec((b, n, hd), lambda c: (0, c, 0)),
        out_shape=jax.ShapeDtypeStruct((b, L, hd), BF16),
        scratch_shapes=[pltpu.VMEM((b * DN_HEADS, DN_HEAD, DN_HEAD), F32),
                        pltpu.VMEM((b, CONV_HALO + n, DN_QKV), F32)],
        compiler_params=_params("arbitrary"),
        name="dn_chunk",
    )(proj, proj, proj, proj, small, conv_w, a_log_row, dt_bias_row,
      norm_w.reshape(1, DN_HEAD))


def gated_deltanet_mixer(h, nw, w_all, conv_w, a_log, dt_bias, norm_w):
    lane_pad = jnp.zeros((DN_SMALL - 2 * DN_HEADS,), F32)
    head_pad = jnp.zeros((DN_HEADS,), F32)
    a_log_row = jnp.concatenate([head_pad, a_log.astype(F32), lane_pad]).reshape(1, DN_SMALL)
    dt_bias_row = jnp.concatenate([head_pad, dt_bias.astype(F32), lane_pad]).reshape(1, DN_SMALL)
    proj, small = norm_matmul(h, nw, w_all, BF16, tail=DN_SMALL)
    return dn_chunk(proj, small, conv_w, a_log_row, dt_bias_row, norm_w)


def _t5_bucket(rel):
    nb = N_BUCKETS // 2
    ret = jnp.where(rel > 0, nb, 0)
    n = jnp.abs(rel)
    max_exact = nb // 2
    nf = jnp.maximum(n, 1).astype(F32)
    large = max_exact + (jnp.log(nf / max_exact) / math.log(MAX_DISTANCE / max_exact)
                         * (nb - max_exact)).astype(jnp.int32)
    large = jnp.minimum(large, nb - 1)
    return ret + jnp.where(n < max_exact, n, large)


def _bias_tile_kernel(tab_ref, o_ref):
    h = pl.program_id(0)
    which = pl.program_id(1)
    key = lax.broadcasted_iota(jnp.int32, (QBLOCK, QBLOCK), 0)
    query = lax.broadcasted_iota(jnp.int32, (QBLOCK, QBLOCK), 1)
    bucket = _t5_bucket(key - query - QBLOCK * which)
    acc = jnp.zeros((QBLOCK, QBLOCK), F32)
    for bkt in range(N_BUCKETS):
        acc = jnp.where(bucket == bkt, tab_ref[bkt, h], acc)
    o_ref[0, 0] = (acc - tab_ref[N_BUCKETS // 2 - 1, h]) * LOG2E


def bias_tiles(rel_bias):
    return pl.pallas_call(
        _bias_tile_kernel,
        grid=(DA_HEADS, 2),
        in_specs=[pl.BlockSpec(memory_space=pltpu.SMEM)],
        out_specs=pl.BlockSpec((1, 1, QBLOCK, QBLOCK), lambda h, w: (h, w, 0, 0)),
        out_shape=jax.ShapeDtypeStruct((DA_HEADS, 2, QBLOCK, QBLOCK), F32),
        compiler_params=_params("arbitrary", "arbitrary"),
        name="bias_tiles",
    )(rel_bias.astype(F32))


def _da_kernel(q_ref, qnext_ref, k_ref, v_ref, bias_ref, lamv_ref, subw_ref, o_ref,
               vt_ref, qt_ref, s_ref, p_ref, mblk_ref, m_ref, l_ref, acc_ref,
               *, tq, lambda_init):
    qi = pl.program_id(2)
    nsub = tq // QBLOCK
    hw = 2 * DA_HEAD
    n_blocks = k_ref.shape[1] // tq

    @pl.when(qi == 0)
    def _():
        def prep(t, carry):
            rows = pl.ds(pl.multiple_of(t * tq, tq), tq)
            vt_ref[t] = v_ref[0, rows, :].astype(F32).T.astype(BF16)
            return carry

        lax.fori_loop(0, n_blocks, prep, 0)

    def load_queries(src_ref):
        feat = lax.broadcasted_iota(jnp.int32, (hw, 1), 0)
        q_t = src_ref[0].astype(F32).T
        qt_ref[:, :tq] = jnp.where(feat < DA_HEAD, q_t, 0.0).astype(BF16)
        qt_ref[:, tq:] = jnp.where(feat >= DA_HEAD, q_t, 0.0).astype(BF16)

    m_ref[...] = jnp.full(m_ref.shape, NEG_INF, F32)
    l_ref[...] = jnp.zeros_like(l_ref)
    acc_ref[...] = jnp.zeros_like(acc_ref)

    def sub_rows(j):
        return slice(j * QBLOCK, (j + 1) * QBLOCK)

    def near_terms(s_half, j, diag, sub):
        tiles = [s_half[:, sub_rows(qq)] for qq in range(nsub)]
        if sub and j == nsub - 1:
            tiles[0] = tiles[0] + bias_ref[0, 1]
        if diag:
            key = lax.broadcasted_iota(jnp.int32, (QBLOCK, QBLOCK), 0)
            query = lax.broadcasted_iota(jnp.int32, (QBLOCK, QBLOCK), 1)
            allowed = (lax.shift_right_logical(key, CHUNK_SHIFT)
                       <= lax.shift_right_logical(query, CHUNK_SHIFT))
            for qq in range(j):
                tiles[qq] = jnp.full((QBLOCK, QBLOCK), NEG_INF, F32)
            tiles[j] = jnp.where(allowed, tiles[j] + bias_ref[0, 0], NEG_INF)
            if j + 1 < nsub:
                tiles[j + 1] = tiles[j + 1] + bias_ref[0, 1]
        return jnp.concatenate(tiles, axis=1)

    def scores_part(kb, j0, nj, diag, sub, first):
        rows = pl.ds(pl.multiple_of(kb * tq + j0 * QBLOCK, QBLOCK), nj * QBLOCK)
        s = _dot(k_ref[0, rows, :], qt_ref[...])
        pieces = []
        for jj in range(nj):
            j = j0 + jj
            piece = s[sub_rows(jj), :]
            if diag or (sub and j == nsub - 1):
                piece = jnp.concatenate([near_terms(piece[:, :tq], j, diag, sub),
                                         near_terms(piece[:, tq:], j, diag, sub)], axis=1)
            if first and j == 0:
                valid = lax.broadcasted_iota(jnp.int32, (QBLOCK, 1), 0) >= FRONT_PAD
                piece = jnp.where(valid, piece, NEG_INF)
            pieces.append(piece)
        s = pieces[0] if nj == 1 else jnp.concatenate(pieces, axis=0)
        s_ref[j0 * QBLOCK:(j0 + nj) * QBLOCK, :] = s
        return jnp.max(s.reshape(nj * QBLOCK // SUBLANES, SUBLANES, 2 * tq), axis=0)

    parts_after = {min(j0 + 1, nsub - 1): (j0, min(2, nsub - j0)) for j0 in range(0, nsub, 2)}

    def stage(kb, nxt=None, diag=False, sub=False, first=False):
        if kb is not None:
            m_old = m_ref[...]
            m_new = jnp.maximum(m_old, mblk_ref[...])
        lsum = None
        running = None
        for j in range(nsub):
            if kb is not None:
                p = jnp.exp2(s_ref[sub_rows(j), :] - m_new)
                lj = jnp.sum(p.reshape(QBLOCK // SUBLANES, SUBLANES, 2 * tq), axis=0)
                lsum = lj if lsum is None else lsum + lj
                p_ref[sub_rows(j), :] = p.astype(BF16)
            if nxt is not None and j in parts_after:
                mj = scores_part(nxt, *parts_after[j], diag, sub, first)
                running = mj if running is None else jnp.maximum(running, mj)
        if nxt is not None:
            mblk_ref[...] = jnp.max(running, axis=0, keepdims=True)
        if kb is not None:
            alpha = jnp.exp2(m_old - m_new)
            l_ref[...] = alpha * l_ref[...] + jnp.sum(lsum, axis=0, keepdims=True)
            acc_ref[...] = alpha * acc_ref[...] + _dot(vt_ref[kb], p_ref[...])
            m_ref[...] = m_new

    def region(pred, *args, **kwargs):
        @pl.when(pred)
        def _():
            stage(*args, **kwargs)

    last = pl.num_programs(2) - 1

    @pl.when(qi == 0)
    def _():
        load_queries(q_ref)

    region(qi < 1, None, 0, diag=True, first=True)

    region(qi == 1, 0, 1, diag=True)

    @pl.when(qi > 1)
    def _():
        def body(kb, carry):
            stage(kb, kb + 1)
            return carry

        lax.fori_loop(0, qi - 2, body, 0)

    region(qi >= 2, qi - 2, qi - 1, sub=True)
    region(qi - 2 >= 0, qi - 1, qi, diag=True)

    @pl.when(qi < last)
    def _():
        load_queries(qnext_ref)

    region((qi + 1 == 1) & (qi < last), qi, 0, sub=True, first=True)
    region((qi >= 1) & (qi < last), qi, 0, first=True)
    region(qi == last, qi)

    lamv = lamv_ref[...]
    lam = (jnp.exp(jnp.sum(lamv[0:1] * lamv[1:2], axis=-1, keepdims=True))
           - jnp.exp(jnp.sum(lamv[2:3] * lamv[3:4], axis=-1, keepdims=True)) + lambda_init)
    on = acc_ref[...] * (1.0 / l_ref[...])
    o_t = on[:, :tq] - lam * on[:, tq:]
    o = _rmsnorm_rows(o_t.T, subw_ref[...]) * (1.0 - lambda_init)
    o_ref[0] = o.astype(o_ref.dtype)


def diff_attention_core(proj, bias, lamv, subln_w, lambda_init):
    b, L, _ = proj.shape
    tq = _pick(L, (640, 128))
    hw = 2 * DA_HEAD
    return pl.pallas_call(
        functools.partial(_da_kernel, tq=tq, lambda_init=lambda_init),
        grid=(b, DA_HEADS, L // tq),
        in_specs=[
            pl.BlockSpec((1, tq, hw), lambda bi, h, i: (bi, i, h)),
            pl.BlockSpec((1, tq, hw), lambda bi, h, i: (bi, jnp.minimum(i + 1, L // tq - 1), h)),
            pl.BlockSpec((1, L, hw), lambda bi, h, i: (bi, 0, DA_HEADS + h)),
            pl.BlockSpec((1, L, hw), lambda bi, h, i: (bi, 0, 2 * DA_HEADS + h)),
            pl.BlockSpec((1, 2, QBLOCK, QBLOCK), lambda bi, h, i: (h, 0, 0, 0)),
            pl.BlockSpec((4, DA_HEAD), lambda bi, h, i: (0, 0)),
            pl.BlockSpec((1, hw), lambda bi, h, i: (0, 0)),
        ],
        out_specs=pl.BlockSpec((1, tq, hw), lambda bi, h, i: (bi, i, h)),
        out_shape=jax.ShapeDtypeStruct((b, L, DA_HEADS * hw), BF16),
        scratch_shapes=[pltpu.VMEM((L // tq, hw, tq), BF16), pltpu.VMEM((hw, 2 * tq), BF16),
                        pltpu.VMEM((tq, 2 * tq), F32), pltpu.VMEM((tq, 2 * tq), BF16),
                        pltpu.VMEM((1, 2 * tq), F32), pltpu.VMEM((1, 2 * tq), F32),
                        pltpu.VMEM((1, 2 * tq), F32), pltpu.VMEM((hw, 2 * tq), F32)],
        compiler_params=_params("arbitrary", "arbitrary", "arbitrary"),
        name="diff_attention",
    )(proj, proj, proj, proj, bias, lamv, subln_w.reshape(1, hw))


def diff_attention_mixer(h, nw, w_in, lam_q1, lam_k1, lam_q2, lam_k2, subln_w,
                         rel_bias, lambda_init):
    qk = DA_HEADS * 2 * DA_HEAD
    col_scale = jnp.concatenate([jnp.full((qk,), DA_Q_SCALE, F32),
                                 jnp.ones((w_in[0].shape[2] - qk,), F32)])
    proj = norm_matmul(h, nw, w_in, BF16, col_scale)
    bias = bias_tiles(rel_bias)
    lamv = jnp.stack([lam_q1, lam_k1, lam_q2, lam_k2]).astype(F32)
    return diff_attention_core(proj, bias, lamv, subln_w, lambda_init)


def _lru_kernel(gate_ref, x_ref, halo_ref, cw_ref, cb_ref, wr_ref, br_ref, wi_ref, bi_ref,
                lam_ref, o_ref, xs_ref, a_ref, b_ref, h_ref):
    i = pl.program_id(1)
    tl = x_ref.shape[1]

    @pl.when(i == 0)
    def _():
        h_ref[...] = jnp.zeros_like(h_ref)

    halo = halo_ref[0, BF16_ROWS - CONV_HALO:, :].astype(F32)
    xr = _causal_conv_rows(xs_ref, x_ref[0].astype(F32), halo, i == 0, cw_ref[...], LRU_CONV)
    xr = jnp.where(_keep_rows(i, tl), xr + cb_ref[...], 0.0)
    neg_sp = -LRU_C * jnp.logaddexp(-lam_ref[...], 0.0)
    for g in range(LRU_BLOCKS):
        cols = slice(g * LRU_BLOCK, (g + 1) * LRU_BLOCK)
        xg = xr[:, cols]
        x16 = xg.astype(BF16)
        r = jax.nn.sigmoid(_dot(x16, wr_ref[g]) + br_ref[:, cols])
        ig = jax.nn.sigmoid(_dot(x16, wi_ref[g]) + bi_ref[:, cols])
        log_a = r * neg_sp[:, cols]
        a = jnp.exp(log_a)
        inp = jnp.sqrt(jnp.maximum(-jnp.tanh(log_a) * (a * a + 1.0), 0.0)) * (ig * xg)
        for half in range(LRU_BLOCK // LANES):
            lanes = slice(half * LANES, (half + 1) * LANES)
            a_ref[g * (LRU_BLOCK // LANES) + half] = a[:, lanes]
            b_ref[g * (LRU_BLOCK // LANES) + half] = inp[:, lanes]

    groups = tl // SCAN_GROUP
    for cb in range(LRU_WIDTH // LANES):
        a_cum = a_ref[cb, pl.ds(0, groups, stride=SCAN_GROUP), :]
        b_loc = b_ref[cb, pl.ds(0, groups, stride=SCAN_GROUP), :]
        for r in range(1, SCAN_GROUP):
            rows_r = pl.ds(r, groups, stride=SCAN_GROUP)
            a_r = a_ref[cb, rows_r, :]
            b_loc = a_r * b_loc + b_ref[cb, rows_r, :]
            a_cum = a_cum * a_r
            a_ref[cb, rows_r, :] = a_cum
            b_ref[cb, rows_r, :] = b_loc

    def body(t, hprev):
        rows = pl.ds(pl.multiple_of(t * SCAN_GROUP, SCAN_GROUP), SCAN_GROUP)
        hs = b_ref[:, rows, :] + a_ref[:, rows, :] * hprev
        b_ref[:, rows, :] = hs
        return hs[:, SCAN_GROUP - 1:SCAN_GROUP, :]

    h_ref[...] = lax.fori_loop(0, groups, body, h_ref[...])
    gate = gate_ref[0].astype(F32)
    gelu = 0.5 * gate * (1.0 + jnp.tanh(math.sqrt(2.0 / math.pi)
                                        * (gate + 0.044715 * (gate * gate * gate))))
    hs_all = jnp.concatenate([b_ref[cb] for cb in range(LRU_WIDTH // LANES)], axis=1)
    o_ref[0] = (hs_all * gelu).astype(o_ref.dtype)


def lru_core(proj, conv_w, conv_b, w_r, b_r, w_i, b_i, lam):
    b, L, _ = proj.shape
    tl = _pick(L, (640, 320, 128))
    wd = LRU_WIDTH
    row = lambda a: a.astype(F32).reshape(1, wd)
    return pl.pallas_call(
        _lru_kernel,
        grid=(b, L // tl),
        in_specs=[
            pl.BlockSpec((1, tl, wd), lambda bi, i: (bi, i, 0)),
            pl.BlockSpec((1, tl, wd), lambda bi, i: (bi, i, 1)),
            pl.BlockSpec((1, BF16_ROWS, wd),
                         lambda bi, i: (bi, jnp.maximum(i * (tl // BF16_ROWS) - 1, 0), 1)),
            pl.BlockSpec((LRU_CONV, wd), lambda bi, i: (0, 0)),
            pl.BlockSpec((1, wd), lambda bi, i: (0, 0)),
            pl.BlockSpec((LRU_BLOCKS, LRU_BLOCK, LRU_BLOCK), lambda bi, i: (0, 0, 0)),
            pl.BlockSpec((1, wd), lambda bi, i: (0, 0)),
            pl.BlockSpec((LRU_BLOCKS, LRU_BLOCK, LRU_BLOCK), lambda bi, i: (0, 0, 0)),
            pl.BlockSpec((1, wd), lambda bi, i: (0, 0)),
            pl.BlockSpec((1, wd), lambda bi, i: (0, 0)),
        ],
        out_specs=pl.BlockSpec((1, tl, wd), lambda bi, i: (bi, i, 0)),
        out_shape=jax.ShapeDtypeStruct((b, L, wd), BF16),
        scratch_shapes=[pltpu.VMEM((tl + CONV_HALO, wd), F32),
                        pltpu.VMEM((wd // LANES, tl, LANES), F32),
                        pltpu.VMEM((wd // LANES, tl, LANES), F32),
                        pltpu.VMEM((wd // LANES, 1, LANES), F32)],
        compiler_params=_params("arbitrary", "arbitrary"),
        name="rglru",
    )(proj, proj, proj, conv_w, row(conv_b), w_r.astype(BF16), row(b_r), w_i.astype(BF16),
      row(b_i), row(lam))


def rglru_mixer(h, nw, w_in, conv_w, conv_b, w_r, b_r, w_i, b_i, lam):
    proj = norm_matmul(h, nw, w_in, BF16)
    return lru_core(proj, conv_w, conv_b, w_r, b_r, w_i, b_i, lam)


def kernel(x, meta_tokens, rel_bias, norm_mix_w, norm_mlp_w, final_norm_w, dn_w_in, dn_conv_w, dn_a_log, dn_dt_bias, dn_norm_w, dn_w_out, da_w_in, da_lam_q1, da_lam_k1, da_lam_q2, da_lam_k2, da_subln_w, da_w_out, lru_w_in, lru_conv_w, lru_conv_b, lru_w_rgate, lru_b_rgate, lru_w_igate, lru_b_igate, lru_lambda, lru_w_out, mlp_w1, mlp_w2):
    b = x.shape[0]
    depth = norm_mix_w.shape[0]
    h = jnp.concatenate([
        jnp.zeros((b, FRONT_PAD, D_MODEL), x.dtype),
        jnp.broadcast_to(meta_tokens[None].astype(x.dtype), (b, N_META, D_MODEL)),
        x,
    ], axis=1)
    dn_pad = jnp.zeros(dn_w_in.shape[:2] + (DN_SMALL - 2 * DN_HEADS,), dn_w_in.dtype)
    dn_w_all = jnp.concatenate([dn_w_in, dn_pad], axis=2).astype(BF16)
    dn_w_out, da_w_in, da_w_out, lru_w_in, lru_w_out, mlp_w1, mlp_w2 = (
        w.astype(BF16) for w in (dn_w_out, da_w_in, da_w_out, lru_w_in, lru_w_out, mlp_w1, mlp_w2))
    for layer in range(depth):
        kind = layer % N_MIXERS
        slot = layer // N_MIXERS
        if kind == 0:
            y = gated_deltanet_mixer(h, norm_mix_w[layer], (dn_w_all, slot), dn_conv_w[slot],
                                     dn_a_log[slot], dn_dt_bias[slot], dn_norm_w[slot])
            w_out = dn_w_out
        elif kind == 1:
            lambda_init = 0.8 - 0.6 * math.exp(-0.3 * layer)
            y = diff_attention_mixer(h, norm_mix_w[layer], (da_w_in, slot), da_lam_q1[slot],
                                     da_lam_k1[slot], da_lam_q2[slot], da_lam_k2[slot],
                                     da_subln_w[slot], rel_bias, lambda_init)
            w_out = da_w_out
        else:
            y = rglru_mixer(h, norm_mix_w[layer], (lru_w_in, slot), lru_conv_w[slot],
                            lru_conv_b[slot], lru_w_rgate[slot], lru_b_rgate[slot],
                            lru_w_igate[slot], lru_b_igate[slot], lru_lambda[slot])
            w_out = lru_w_out
        final_w = final_norm_w if layer == depth - 1 else None
        h = block_tail(y, (w_out, slot), h, norm_mlp_w[layer], (mlp_w1, layer), (mlp_w2, layer),
                       final_w)
    return h
```

```python
import functools
import math

import jax
import jax.numpy as jnp
from jax import lax
from jax.experimental import pallas as pl
from jax.experimental.pallas import tpu as pltpu

F32 = jnp.float32
BF16 = jnp.bfloat16

D_MODEL = 1024
N_META = 16
QBLOCK = 128
FRONT_PAD = QBLOCK - N_META
N_MIXERS = 3
EPS = 1e-6
CHUNK = 64
CHUNK_SHIFT = 6
SUBLANES = 8
LANES = 128
BF16_ROWS = 2 * SUBLANES
CONV_HALO = SUBLANES

DN_HEADS = 8
DN_HEAD = 128
DN_CONV = 4
DN_QKV = 3 * DN_HEADS * DN_HEAD
DN_SMALL = 128
DN_CHUNK = 128
INV_BASE_LOG = 4

DA_HEADS = 8
DA_HEAD = 64
N_BUCKETS = 32
MAX_DISTANCE = 128
NEG_INF = -1e30
LOG2E = math.log2(math.e)
DA_Q_SCALE = DA_HEAD ** -0.5 * LOG2E

LRU_WIDTH = 1024
LRU_BLOCKS = 4
LRU_BLOCK = LRU_WIDTH // LRU_BLOCKS
LRU_CONV = 4
LRU_C = 8.0
SCAN_GROUP = SUBLANES

V7X_VMEM_LIMIT_BYTES = 56 * 1024 * 1024


def _params(*semantics):
    return pltpu.CompilerParams(dimension_semantics=semantics,
                                vmem_limit_bytes=V7X_VMEM_LIMIT_BYTES)


def _pick(n, candidates):
    for c in candidates:
        if n % c == 0:
            return c
    raise ValueError(f"no tile for {n} in {candidates}")


def _dot(a, b):
    return jnp.dot(a, b, preferred_element_type=F32)


def _dot_nt(a, b):
    return lax.dot_general(a, b, (((1,), (1,)), ((), ())), preferred_element_type=F32)


def _split2(a):
    hi = a.astype(BF16)
    lo = (a - hi.astype(F32)).astype(BF16)
    return hi, lo


def _rmsnorm_rows(x, w):
    return x * lax.rsqrt(jnp.mean(x * x, axis=-1, keepdims=True) + EPS) * w


def _keep_rows(tile_index, rows):
    pos = tile_index * rows + lax.broadcasted_iota(jnp.int32, (rows, 1), 0)
    return pos >= FRONT_PAD


def _norm_matmul_kernel(h_ref, nw_ref, w_ref, *rest, scaled, tail):
    rest = list(rest)
    cs_ref = rest.pop(0) if scaled else None
    o_ref = rest.pop(0)
    t_ref = rest.pop(0) if tail else None
    u_ref = rest.pop(0)
    j = pl.program_id(2)

    @pl.when(j == 0)
    def _():
        u_ref[...] = _rmsnorm_rows(h_ref[0], nw_ref[...]).astype(BF16)

    y = _dot(u_ref[...], w_ref[...])
    if scaled:
        y = y * cs_ref[...]
    o_ref[0] = y.astype(o_ref.dtype)
    if tail:
        @pl.when(j == pl.num_programs(2) - 1)
        def _():
            t_ref[0] = y[:, y.shape[1] - tail:]


def norm_matmul(h, nw, w, out_dtype, col_scale=None, tail=0):
    b, L, d = h.shape
    w, slot = w
    n = w.shape[2]
    tm = _pick(L, (2080, 640, 320, 128))
    tn = _pick(n, (1408, 1024, 512, 128))
    scaled = col_scale is not None
    in_specs = [
        pl.BlockSpec((1, tm, d), lambda bi, i, j: (bi, i, 0)),
        pl.BlockSpec((1, d), lambda bi, i, j: (0, 0)),
        pl.BlockSpec((None, d, tn), lambda bi, i, j: (slot, 0, j)),
    ]
    args = [h, nw.reshape(1, d), w]
    if scaled:
        in_specs.append(pl.BlockSpec((1, tn), lambda bi, i, j: (0, j)))
        args.append(col_scale.astype(F32).reshape(1, n))
    out_specs = pl.BlockSpec((1, tm, tn), lambda bi, i, j: (bi, i, j))
    out_shape = jax.ShapeDtypeStruct((b, L, n), out_dtype)
    if tail:
        out_specs = [out_specs, pl.BlockSpec((1, tm, tail), lambda bi, i, j: (bi, i, 0))]
        out_shape = [out_shape, jax.ShapeDtypeStruct((b, L, tail), F32)]
    return pl.pallas_call(
        functools.partial(_norm_matmul_kernel, scaled=scaled, tail=tail),
        grid=(b, L // tm, n // tn),
        in_specs=in_specs,
        out_specs=out_specs,
        out_shape=out_shape,
        scratch_shapes=[pltpu.VMEM((tm, d), BF16)],
        compiler_params=_params("arbitrary", "arbitrary", "arbitrary"),
        name="norm_matmul",
    )(*args)


def _tail_kernel(y_ref, wo_ref, h_ref, nw_ref, w1_ref, w2_ref, fw_ref, o_ref,
                 u_ref, hs_ref, acc_ref, *, final_norm, chunks):
    f = pl.program_id(2)
    tm = h_ref.shape[1]

    def kept(x):
        if final_norm:
            return x
        return jnp.where(_keep_rows(pl.program_id(1), tm), x, 0.0)

    @pl.when(f == 0)
    def _():
        keep_all = None if final_norm else _keep_rows(pl.program_id(1), tm)
        step = tm // chunks
        for c in range(chunks):
            rows = slice(c * step, (c + 1) * step)
            mix = _dot(y_ref[0, rows, :], wo_ref[...])
            if keep_all is not None:
                mix = jnp.where(keep_all[rows], mix, 0.0)
            h1 = h_ref[0, rows, :] + mix
            hs_ref[rows, :] = h1
            u_ref[rows, :] = _rmsnorm_rows(h1, nw_ref[...]).astype(BF16)
        acc_ref[...] = jnp.zeros_like(acc_ref)

    a = _dot(u_ref[...], w1_ref[...])
    a = jnp.square(jnp.maximum(a, 0.0)).astype(BF16)
    acc_ref[...] += _dot(a, w2_ref[...])

    @pl.when(f == pl.num_programs(2) - 1)
    def _():
        hn = hs_ref[...] + kept(acc_ref[...])
        if final_norm:
            hn = _rmsnorm_rows(hn, fw_ref[...])
        o_ref[0] = hn


def block_tail(y, w_out, h, nw, w1, w2, final_w=None):
    b, L, d = h.shape
    (w_out, slot), (w1, layer), (w2, _) = w_out, w1, w2
    k = y.shape[-1]
    ff = w1.shape[2]
    tf = _pick(ff, (1024, 512, 128))
    final_norm = final_w is not None
    if final_norm:
        first_row = FRONT_PAD + N_META
        rows = L - first_row
        tm = _pick(rows, (1024, 512, 128))

        def row_spec(width):
            return pl.BlockSpec(
                (pl.Element(1), pl.Element(tm), pl.Element(width)),
                lambda bi, i, f: (bi, pl.multiple_of(first_row + i * tm, QBLOCK), 0))
    else:
        rows = L
        tm = _pick(L, (1040, 640, 320, 128))

        def row_spec(width):
            return pl.BlockSpec((1, tm, width), lambda bi, i, f: (bi, i, 0))
    fw = (final_w if final_norm else nw).reshape(1, d)
    chunks = _pick(tm // BF16_ROWS, (5, 4, 2, 1))
    return pl.pallas_call(
        functools.partial(_tail_kernel, final_norm=final_norm, chunks=chunks),
        grid=(b, rows // tm, ff // tf),
        in_specs=[
            row_spec(k),
            pl.BlockSpec((None, k, d), lambda bi, i, f: (slot, 0, 0)),
            row_spec(d),
            pl.BlockSpec((1, d), lambda bi, i, f: (0, 0)),
            pl.BlockSpec((None, d, tf), lambda bi, i, f: (layer, 0, f)),
            pl.BlockSpec((None, tf, d), lambda bi, i, f: (layer, f, 0)),
            pl.BlockSpec((1, d), lambda bi, i, f: (0, 0)),
        ],
        out_specs=pl.BlockSpec((1, tm, d), lambda bi, i, f: (bi, i, 0)),
        out_shape=jax.ShapeDtypeStruct((b, rows, d), F32),
        scratch_shapes=[pltpu.VMEM((tm, d), BF16), pltpu.VMEM((tm, d), F32),
                        pltpu.VMEM((tm, d), F32)],
        compiler_params=_params("arbitrary", "arbitrary", "arbitrary"),
        name="block_tail",
    )(y, w_out, h, nw.reshape(1, d), w1, w2, fw)


def _conv4_rows(x, w):
    x1 = pltpu.roll(x, 1, 0)
    near = x * w[3:4, :] + x1 * w[2:3, :]
    far = x * w[1:2, :] + x1 * w[0:1, :]
    return near + pltpu.roll(far, 2, 0)


def _causal_conv_rows(xs_ref, cur, halo, first_tile, w, width):
    assert width == 4
    xs_ref[0:CONV_HALO, :] = jnp.where(first_tile, 0.0, halo)
    xs_ref[CONV_HALO:, :] = cur
    return _conv4_rows(xs_ref[...], w)[CONV_HALO:, :]


def _approx_unit_lower_inverses(a_list):
    n = a_list[0].shape[0]
    row = lax.broadcasted_iota(jnp.int32, (n, n), 0)
    col = lax.broadcasted_iota(jnp.int32, (n, n), 1)

    def same_block(log_size):
        return lax.shift_right_logical(row, log_size) == lax.shift_right_logical(col, log_size)

    log_size = INV_BASE_LOG
    in_diag = same_block(log_size)
    eye = (row == col).astype(F32)
    ad = [jnp.where(in_diag, a, 0.0) for a in a_list]
    t = [eye - x for x in ad]
    bk = [x.astype(BF16) for x in ad]
    for _ in range(log_size - 1):
        bk = [_dot(x, x).astype(BF16) for x in bk]
        t = [ti + _dot(ti.astype(BF16), x) for ti, x in zip(t, bk)]
    while (1 << log_size) < n:
        sel = same_block(log_size + 1) & jnp.logical_not(same_block(log_size))
        off = [jnp.where(sel, a, 0.0).astype(BF16) for a in a_list]
        t16 = [ti.astype(BF16) for ti in t]
        left = [_dot(ti, o).astype(BF16) for ti, o in zip(t16, off)]
        t = [ti - _dot(x, ti16) for ti, x, ti16 in zip(t, left, t16)]
        log_size += 1
    return t


def _dn_chunk_kernel(q_ref, k_ref, v_ref, z_ref, s_ref, cw_ref, alog_ref, dtb_ref, nw_ref,
                     o_ref, state_ref, xs_ref):
    c = pl.program_id(0)
    n = DN_CHUNK
    nb = q_ref.shape[0]
    hd = DN_HEADS * DN_HEAD
    streams = [(b, h) for b in range(nb) for h in range(DN_HEADS)]

    @pl.when(c == 0)
    def _():
        state_ref[...] = jnp.zeros_like(state_ref)
        xs_ref[:, 0:CONV_HALO, :] = jnp.zeros((nb, CONV_HALO, 3 * hd), F32)

    for b in range(nb):
        xs_ref[b, CONV_HALO:, 0:hd] = q_ref[b].astype(F32)
        xs_ref[b, CONV_HALO:, hd:2 * hd] = k_ref[b].astype(F32)
        xs_ref[b, CONV_HALO:, 2 * hd:] = v_ref[b].astype(F32)

    def conv_act(b, g):
        lanes = slice(g * DN_HEAD, (g + 1) * DN_HEAD)
        x = xs_ref[b, :, lanes]
        acc = _conv4_rows(x, cw_ref[:, lanes])[CONV_HALO:, :]
        return acc * jax.nn.sigmoid(acc)

    def unit_rows(x):
        return x * lax.rsqrt(jnp.sum(x * x, axis=-1, keepdims=True) + EPS)

    keep = _keep_rows(c, n)
    row = lax.broadcasted_iota(jnp.int32, (n, n), 0)
    col = lax.broadcasted_iota(jnp.int32, (n, n), 1)
    incl = row >= col
    strict = row > col
    tri = incl.astype(BF16)

    beta, gc, gc_t = [], [], []
    for b in range(nb):
        small = s_ref[b]
        beta.append(jax.nn.sigmoid(small))
        sp = jnp.logaddexp(small + dtb_ref[...], 0.0)
        g = jnp.where(keep, -jnp.exp(alog_ref[...]) * sp, 0.0)
        g_hi = g.astype(BF16)
        r1 = g - g_hi.astype(F32)
        g_mid = r1.astype(BF16)
        g_lo = (r1 - g_mid.astype(F32)).astype(BF16)
        gcb = _dot(tri, g_hi) + (_dot(tri, g_mid) + _dot(tri, g_lo))
        gc.append(gcb)
        gc_t.append(gcb.T)

    def cols(h):
        return slice(h * DN_HEAD, (h + 1) * DN_HEAD)

    bcol = [beta[b][:, h:h + 1] for b, h in streams]
    gcol = [gc[b][:, DN_HEADS + h:DN_HEADS + h + 1] for b, h in streams]
    grow = [gc_t[b][DN_HEADS + h:DN_HEADS + h + 1, :] for b, h in streams]
    decay = [jnp.exp(jnp.where(incl, gi - gj, -jnp.inf)) for gi, gj in zip(gcol, grow)]
    k = [jnp.where(keep, unit_rows(conv_act(b, DN_HEADS + h)), 0.0) for b, h in streams]
    k16 = [x.astype(BF16) for x in k]
    kb = [x * bc for x, bc in zip(k, bcol)]
    a = [jnp.where(strict, _dot_nt(x.astype(BF16), y) * dc, 0.0)
         for x, y, dc in zip(kb, k16, decay)]
    t16 = [x.astype(BF16) for x in _approx_unit_lower_inverses(a)]
    a_split = [_split2(x) for x in a]

    egc = [jnp.exp(x) for x in gcol]
    s = [state_ref[i] for i in range(len(streams))]
    s16 = [x.astype(BF16) for x in s]
    rhs = [jnp.where(keep, conv_act(b, 2 * DN_HEADS + h), 0.0) * bc
           - _dot((kbi * e).astype(BF16), si)
           for (b, h), bc, kbi, e, si in zip(streams, bcol, kb, egc, s16)]
    x0 = [_dot(ti, r.astype(BF16)) for ti, r in zip(t16, rhs)]
    resid = []
    for (ah, al), x, r in zip(a_split, x0, rhs):
        xh, xl = _split2(x)
        resid.append(r - x - (_dot(ah, xh) + (_dot(ah, xl) + _dot(al, xh))))
    v_new = [x + _dot(ti, r.astype(BF16)) for x, ti, r in zip(x0, t16, resid)]
    v16 = [x.astype(BF16) for x in v_new]

    q = [unit_rows(conv_act(b, h)) * (DN_HEAD ** -0.5) for b, h in streams]
    attn = [(_dot_nt(x.astype(BF16), y) * dc).astype(BF16) for x, y, dc in zip(q, k16, decay)]
    o = [_dot((x * e).astype(BF16), si) + _dot(at, vi)
         for x, e, si, at, vi in zip(q, egc, s16, attn, v16)]
    g_last = [x[n - 1:n, :] for x in gcol]
    kdec = [(x * jnp.exp(gl - gi)).T.astype(BF16) for x, gl, gi in zip(k, g_last, gcol)]
    for i, (si, gl, kd, vi) in enumerate(zip(s, g_last, kdec, v16)):
        state_ref[i] = si * jnp.exp(gl) + _dot(kd, vi)
    for (b, h), oi in zip(streams, o):
        zh = z_ref[b, :, cols(h)].astype(F32)
        y = _rmsnorm_rows(oi, nw_ref[...]) * (zh * jax.nn.sigmoid(zh))
        o_ref[b, :, cols(h)] = y.astype(o_ref.dtype)
    xs_ref[:, 0:CONV_HALO, :] = xs_ref[:, n:n + CONV_HALO, :]


def dn_chunk(proj, small, conv_w, a_log_row, dt_bias_row, norm_w):
    b, L, _ = proj.shape
    n = DN_CHUNK
    hd = DN_HEADS * DN_HEAD
    return pl.pallas_call(
        _dn_chunk_kernel,
        grid=(L // n,),
        in_specs=[
            pl.BlockSpec((b, n, hd), lambda c: (0, c, 0)),
            pl.BlockSpec((b, n, hd), lambda c: (0, c, 1)),
            pl.BlockSpec((b, n, hd), lambda c: (0, c, 2)),
            pl.BlockSpec((b, n, hd), lambda c: (0, c, 3)),
            pl.BlockSpec((b, n, DN_SMALL), lambda c: (0, c, 0)),
            pl.BlockSpec((DN_CONV, DN_QKV), lambda c: (0, 0)),
            pl.BlockSpec((1, DN_SMALL), lambda c: (0, 0)),
            pl.BlockSpec((1, DN_SMALL), lambda c: (0, 0)),
            pl.BlockSpec((1, DN_HEAD), lambda c: (0, 0)),
        ],
        out_specs=pl.BlockSpec((b, n, hd), lambda c: (0, c, 0)),
        out_shape=jax.ShapeDtypeStruct((b, L, hd), BF16),
        scratch_shapes=[pltpu.VMEM((b * DN_HEADS, DN_HEAD, DN_HEAD), F32),
                        pltpu.VMEM((b, CONV_HALO + n, DN_QKV), F32)],
        compiler_params=_params("arbitrary"),
        name="dn_chunk",
    )(proj, proj, proj, proj, small, conv_w, a_log_row, dt_bias_row,
      norm_w.reshape(1, DN_HEAD))


def gated_deltanet_mixer(h, nw, w_all, conv_w, a_log, dt_bias, norm_w):
    lane_pad = jnp.zeros((DN_SMALL - 2 * DN_HEADS,), F32)
    head_pad = jnp.zeros((DN_HEADS,), F32)
    a_log_row = jnp.concatenate([head_pad, a_log.astype(F32), lane_pad]).reshape(1, DN_SMALL)
    dt_bias_row = jnp.concatenate([head_pad, dt_bias.astype(F32), lane_pad]).reshape(1, DN_SMALL)
    proj, small = norm_matmul(h, nw, w_all, BF16, tail=DN_SMALL)
    return dn_chunk(proj, small, conv_w, a_log_row, dt_bias_row, norm_w)


def _t5_bucket(rel):
    nb = N_BUCKETS // 2
    ret = jnp.where(rel > 0, nb, 0)
    n = jnp.abs(rel)
    max_exact = nb // 2
    nf = jnp.maximum(n, 1).astype(F32)
    large = max_exact + (jnp.log(nf / max_exact) / math.log(MAX_DISTANCE / max_exact)
                         * (nb - max_exact)).astype(jnp.int32)
    large = jnp.minimum(large, nb - 1)
    return ret + jnp.where(n < max_exact, n, large)


def _bias_tile_kernel(tab_ref, o_ref):
    h = pl.program_id(0)
    which = pl.program_id(1)
    key = lax.broadcasted_iota(jnp.int32, (QBLOCK, QBLOCK), 0)
    query = lax.broadcasted_iota(jnp.int32, (QBLOCK, QBLOCK), 1)
    bucket = _t5_bucket(key - query - QBLOCK * which)
    acc = jnp.zeros((QBLOCK, QBLOCK), F32)
    for bkt in range(N_BUCKETS):
        acc = jnp.where(bucket == bkt, tab_ref[bkt, h], acc)
    o_ref[0, 0] = (acc - tab_ref[N_BUCKETS // 2 - 1, h]) * LOG2E


def bias_tiles(rel_bias):
    return pl.pallas_call(
        _bias_tile_kernel,
        grid=(DA_HEADS, 2),
        in_specs=[pl.BlockSpec(memory_space=pltpu.SMEM)],
        out_specs=pl.BlockSpec((1, 1, QBLOCK, QBLOCK), lambda h, w: (h, w, 0, 0)),
        out_shape=jax.ShapeDtypeStruct((DA_HEADS, 2, QBLOCK, QBLOCK), F32),
        compiler_params=_params("arbitrary", "arbitrary"),
        name="bias_tiles",
    )(rel_bias.astype(F32))


def _da_kernel(q_ref, qnext_ref, k_ref, v_ref, bias_ref, lamv_ref, subw_ref, o_ref,
               vt_ref, qt_ref, s_ref, p_ref, mblk_ref, m_ref, l_ref, acc_ref,
               *, tq, lambda_init):
    qi = pl.program_id(2)
    nsub = tq // QBLOCK
    hw = 2 * DA_HEAD
    n_blocks = k_ref.shape[1] // tq

    @pl.when(qi == 0)
    def _():
        def prep(t, carry):
            rows = pl.ds(pl.multiple_of(t * tq, tq), tq)
            vt_ref[t] = v_ref[0, rows, :].astype(F32).T.astype(BF16)
            return carry

        lax.fori_loop(0, n_blocks, prep, 0)

    def load_queries(src_ref):
        feat = lax.broadcasted_iota(jnp.int32, (hw, 1), 0)
        q_t = src_ref[0].astype(F32).T
        qt_ref[:, :tq] = jnp.where(feat < DA_HEAD, q_t, 0.0).astype(BF16)
        qt_ref[:, tq:] = jnp.where(feat >= DA_HEAD, q_t, 0.0).astype(BF16)

    m_ref[...] = jnp.full(m_ref.shape, NEG_INF, F32)
    l_ref[...] = jnp.zeros_like(l_ref)
    acc_ref[...] = jnp.zeros_like(acc_ref)

    def sub_rows(j):
        return slice(j * QBLOCK, (j + 1) * QBLOCK)

    def near_terms(s_half, j, diag, sub):
        tiles = [s_half[:, sub_rows(qq)] for qq in range(nsub)]
        if sub and j == nsub - 1:
            tiles[0] = tiles[0] + bias_ref[0, 1]
        if diag:
            key = lax.broadcasted_iota(jnp.int32, (QBLOCK, QBLOCK), 0)
            query = lax.broadcasted_iota(jnp.int32, (QBLOCK, QBLOCK), 1)
            allowed = (lax.shift_right_logical(key, CHUNK_SHIFT)
                       <= lax.shift_right_logical(query, CHUNK_SHIFT))
            for qq in range(j):
                tiles[qq] = jnp.full((QBLOCK, QBLOCK), NEG_INF, F32)
            tiles[j] = jnp.where(allowed, tiles[j] + bias_ref[0, 0], NEG_INF)
            if j + 1 < nsub:
                tiles[j + 1] = tiles[j + 1] + bias_ref[0, 1]
        return jnp.concatenate(tiles, axis=1)

    def scores_part(kb, j0, nj, diag, sub, first):
        rows = pl.ds(pl.multiple_of(kb * tq + j0 * QBLOCK, QBLOCK), nj * QBLOCK)
        s = _dot(k_ref[0, rows, :], qt_ref[...])
        pieces = []
        for jj in range(nj):
            j = j0 + jj
            piece = s[sub_rows(jj), :]
            if diag or (sub and j == nsub - 1):
                piece = jnp.concatenate([near_terms(piece[:, :tq], j, diag, sub),
                                         near_terms(piece[:, tq:], j, diag, sub)], axis=1)
            if first and j == 0:
                valid = lax.broadcasted_iota(jnp.int32, (QBLOCK, 1), 0) >= FRONT_PAD
                piece = jnp.where(valid, piece, NEG_INF)
            pieces.append(piece)
        s = pieces[0] if nj == 1 else jnp.concatenate(pieces, axis=0)
        s_ref[j0 * QBLOCK:(j0 + nj) * QBLOCK, :] = s
        return jnp.max(s.reshape(nj * QBLOCK // SUBLANES, SUBLANES, 2 * tq), axis=0)

    parts_after = {min(j0 + 1, nsub - 1): (j0, min(2, nsub - j0)) for j0 in range(0, nsub, 2)}

    def stage(kb, nxt=None, diag=False, sub=False, first=False):
        if kb is not None:
            m_old = m_ref[...]
            m_new = jnp.maximum(m_old, mblk_ref[...])
        lsum = None
        running = None
        for j in range(nsub):
            if kb is not None:
                p = jnp.exp2(s_ref[sub_rows(j), :] - m_new)
                lj = jnp.sum(p.reshape(QBLOCK // SUBLANES, SUBLANES, 2 * tq), axis=0)
                lsum = lj if lsum is None else lsum + lj
                p_ref[sub_rows(j), :] = p.astype(BF16)
            if nxt is not None and j in parts_after:
                mj = scores_part(nxt, *parts_after[j], diag, sub, first)
                running = mj if running is None else jnp.maximum(running, mj)
        if nxt is not None:
            mblk_ref[...] = jnp.max(running, axis=0, keepdims=True)
        if kb is not None:
            alpha = jnp.exp2(m_old - m_new)
            l_ref[...] = alpha * l_ref[...] + jnp.sum(lsum, axis=0, keepdims=True)
            acc_ref[...] = alpha * acc_ref[...] + _dot(vt_ref[kb], p_ref[...])
            m_ref[...] = m_new

    def region(pred, *args, **kwargs):
        @pl.when(pred)
        def _():
            stage(*args, **kwargs)

    last = pl.num_programs(2) - 1

    @pl.when(qi == 0)
    def _():
        load_queries(q_ref)

    region(qi < 1, None, 0, diag=True, first=True)

    region(qi == 1, 0, 1, diag=True)

    @pl.when(qi > 1)
    def _():
        def body(kb, carry):
            stage(kb, kb + 1)
            return carry

        lax.fori_loop(0, qi - 2, body, 0)

    region(qi >= 2, qi - 2, qi - 1, sub=True)
    region(qi - 2 >= 0, qi - 1, qi, diag=True)

    @pl.when(qi < last)
    def _():
        load_queries(qnext_ref)

    region((qi + 1 == 1) & (qi < last), qi, 0, sub=True, first=True)
    region((qi >= 1) & (qi < last), qi, 0, first=True)
    region(qi == last, qi)

    lamv = lamv_ref[...]
    lam = (jnp.exp(jnp.sum(lamv[0:1] * lamv[1:2], axis=-1, keepdims=True))
           - jnp.exp(jnp.sum(lamv[2:3] * lamv[3:4], axis=-1, keepdims=True)) + lambda_init)
    on = acc_ref[...] * (1.0 / l_ref[...])
    o_t = on[:, :tq] - lam * on[:, tq:]
    o = _rmsnorm_rows(o_t.T, subw_ref[...]) * (1.0 - lambda_init)
    o_ref[0] = o.astype(o_ref.dtype)


def diff_attention_core(proj, bias, lamv, subln_w, lambda_init):
    b, L, _ = proj.shape
    tq = _pick(L, (640, 128))
    hw = 2 * DA_HEAD
    return pl.pallas_call(
        functools.partial(_da_kernel, tq=tq, lambda_init=lambda_init),
        grid=(b, DA_HEADS, L // tq),
        in_specs=[
            pl.BlockSpec((1, tq, hw), lambda bi, h, i: (bi, i, h)),
            pl.BlockSpec((1, tq, hw), lambda bi, h, i: (bi, jnp.minimum(i + 1, L // tq - 1), h)),
            pl.BlockSpec((1, L, hw), lambda bi, h, i: (bi, 0, DA_HEADS + h)),
            pl.BlockSpec((1, L, hw), lambda bi, h, i: (bi, 0, 2 * DA_HEADS + h)),
            pl.BlockSpec((1, 2, QBLOCK, QBLOCK), lambda bi, h, i: (h, 0, 0, 0)),
            pl.BlockSpec((4, DA_HEAD), lambda bi, h, i: (0, 0)),
            pl.BlockSpec((1, hw), lambda bi, h, i: (0, 0)),
        ],
        out_specs=pl.BlockSpec((1, tq, hw), lambda bi, h, i: (bi, i, h)),
        out_shape=jax.ShapeDtypeStruct((b, L, DA_HEADS * hw), BF16),
        scratch_shapes=[pltpu.VMEM((L // tq, hw, tq), BF16), pltpu.VMEM((hw, 2 * tq), BF16),
                        pltpu.VMEM((tq, 2 * tq), F32), pltpu.VMEM((tq, 2 * tq), BF16),
                        pltpu.VMEM((1, 2 * tq), F32), pltpu.VMEM((1, 2 * tq), F32),
                        pltpu.VMEM((1, 2 * tq), F32), pltpu.VMEM((hw, 2 * tq), F32)],
        compiler_params=_params("arbitrary", "arbitrary", "arbitrary"),
        name="diff_attention",
    )(proj, proj, proj, proj, bias, lamv, subln_w.reshape(1, hw))


def diff_attention_mixer(h, nw, w_in, lam_q1, lam_k1, lam_q2, lam_k2, subln_w,
                         rel_bias, lambda_init):
    qk = DA_HEADS * 2 * DA_HEAD
    col_scale = jnp.concatenate([jnp.full((qk,), DA_Q_SCALE, F32),
                                 jnp.ones((w_in[0].shape[2] - qk,), F32)])
    proj = norm_matmul(h, nw, w_in, BF16, col_scale)
    bias = bias_tiles(rel_bias)
    lamv = jnp.stack([lam_q1, lam_k1, lam_q2, lam_k2]).astype(F32)
    return diff_attention_core(proj, bias, lamv, subln_w, lambda_init)


def _lru_kernel(gate_ref, x_ref, halo_ref, cw_ref, cb_ref, wr_ref, br_ref, wi_ref, bi_ref,
                lam_ref, o_ref, xs_ref, a_ref, b_ref, h_ref):
    i = pl.program_id(1)
    tl = x_ref.shape[1]

    @pl.when(i == 0)
    def _():
        h_ref[...] = jnp.zeros_like(h_ref)

    halo = halo_ref[0, BF16_ROWS - CONV_HALO:, :].astype(F32)
    xr = _causal_conv_rows(xs_ref, x_ref[0].astype(F32), halo, i == 0, cw_ref[...], LRU_CONV)
    xr = jnp.where(_keep_rows(i, tl), xr + cb_ref[...], 0.0)
    neg_sp = -LRU_C * jnp.logaddexp(-lam_ref[...], 0.0)
    for g in range(LRU_BLOCKS):
        cols = slice(g * LRU_BLOCK, (g + 1) * LRU_BLOCK)
        xg = xr[:, cols]
        x16 = xg.astype(BF16)
        r = jax.nn.sigmoid(_dot(x16, wr_ref[g]) + br_ref[:, cols])
        ig = jax.nn.sigmoid(_dot(x16, wi_ref[g]) + bi_ref[:, cols])
        log_a = r * neg_sp[:, cols]
        a = jnp.exp(log_a)
        inp = jnp.sqrt(jnp.maximum(-jnp.tanh(log_a) * (a * a + 1.0), 0.0)) * (ig * xg)
        for half in range(LRU_BLOCK // LANES):
            lanes = slice(half * LANES, (half + 1) * LANES)
            a_ref[g * (LRU_BLOCK // LANES) + half] = a[:, lanes]
            b_ref[g * (LRU_BLOCK // LANES) + half] = inp[:, lanes]

    groups = tl // SCAN_GROUP
    for cb in range(LRU_WIDTH // LANES):
        a_cum = a_ref[cb, pl.ds(0, groups, stride=SCAN_GROUP), :]
        b_loc = b_ref[cb, pl.ds(0, groups, stride=SCAN_GROUP), :]
        for r in range(1, SCAN_GROUP):
            rows_r = pl.ds(r, groups, stride=SCAN_GROUP)
            a_r = a_ref[cb, rows_r, :]
            b_loc = a_r * b_loc + b_ref[cb, rows_r, :]
            a_cum = a_cum * a_r
            a_ref[cb, rows_r, :] = a_cum
            b_ref[cb, rows_r, :] = b_loc

    def body(t, hprev):
        rows = pl.ds(pl.multiple_of(t * SCAN_GROUP, SCAN_GROUP), SCAN_GROUP)
        hs = b_ref[:, rows, :] + a_ref[:, rows, :] * hprev
        b_ref[:, rows, :] = hs
        return hs[:, SCAN_GROUP - 1:SCAN_GROUP, :]

    h_ref[...] = lax.fori_loop(0, groups, body, h_ref[...])
    gate = gate_ref[0].astype(F32)
    gelu = 0.5 * gate * (1.0 + jnp.tanh(math.sqrt(2.0 / math.pi)
                                        * (gate + 0.044715 * (gate * gate * gate))))
    hs_all = jnp.concatenate([b_ref[cb] for cb in range(LRU_WIDTH // LANES)], axis=1)
    o_ref[0] = (hs_all * gelu).astype(o_ref.dtype)


def lru_core(proj, conv_w, conv_b, w_r, b_r, w_i, b_i, lam):
    b, L, _ = proj.shape
    tl = _pick(L, (640, 320, 128))
    wd = LRU_WIDTH
    row = lambda a: a.astype(F32).reshape(1, wd)
    return pl.pallas_call(
        _lru_kernel,
        grid=(b, L // tl),
        in_specs=[
            pl.BlockSpec((1, tl, wd), lambda bi, i: (bi, i, 0)),
            pl.BlockSpec((1, tl, wd), lambda bi, i: (bi, i, 1)),
            pl.BlockSpec((1, BF16_ROWS, wd),
                         lambda bi, i: (bi, jnp.maximum(i * (tl // BF16_ROWS) - 1, 0), 1)),
            pl.BlockSpec((LRU_CONV, wd), lambda bi, i: (0, 0)),
            pl.BlockSpec((1, wd), lambda bi, i: (0, 0)),
            pl.BlockSpec((LRU_BLOCKS, LRU_BLOCK, LRU_BLOCK), lambda bi, i: (0, 0, 0)),
            pl.BlockSpec((1, wd), lambda bi, i: (0, 0)),
            pl.BlockSpec((LRU_BLOCKS, LRU_BLOCK, LRU_BLOCK), lambda bi, i: (0, 0, 0)),
            pl.BlockSpec((1, wd), lambda bi, i: (0, 0)),
            pl.BlockSpec((1, wd), lambda bi, i: (0, 0)),
        ],
        out_specs=pl.BlockSpec((1, tl, wd), lambda bi, i: (bi, i, 0)),
        out_shape=jax.ShapeDtypeStruct((b, L, wd), BF16),
        scratch_shapes=[pltpu.VMEM((tl + CONV_HALO, wd), F32),
                        pltpu.VMEM((wd // LANES, tl, LANES), F32),
                        pltpu.VMEM((wd // LANES, tl, LANES), F32),
                        pltpu.VMEM((wd // LANES, 1, LANES), F32)],
        compiler_params=_params("arbitrary", "arbitrary"),
        name="rglru",
    )(proj, proj, proj, conv_w, row(conv_b), w_r.astype(BF16), row(b_r), w_i.astype(BF16),
      row(b_i), row(lam))


def rglru_mixer(h, nw, w_in, conv_w, conv_b, w_r, b_r, w_i, b_i, lam):
    proj = norm_matmul(h, nw, w_in, BF16)
    return lru_core(proj, conv_w, conv_b, w_r, b_r, w_i, b_i, lam)


def kernel(x, meta_tokens, rel_bias, norm_mix_w, norm_mlp_w, final_norm_w, dn_w_in, dn_conv_w, dn_a_log, dn_dt_bias, dn_norm_w, dn_w_out, da_w_in, da_lam_q1, da_lam_k1, da_lam_q2, da_lam_k2, da_subln_w, da_w_out, lru_w_in, lru_conv_w, lru_conv_b, lru_w_rgate, lru_b_rgate, lru_w_igate, lru_b_igate, lru_lambda, lru_w_out, mlp_w1, mlp_w2):
    b = x.shape[0]
    depth = norm_mix_w.shape[0]
    h = jnp.concatenate([
        jnp.zeros((b, FRONT_PAD, D_MODEL), x.dtype),
        jnp.broadcast_to(meta_tokens[None].astype(x.dtype), (b, N_META, D_MODEL)),
        x,
    ], axis=1)
    dn_pad = jnp.zeros(dn_w_in.shape[:2] + (DN_SMALL - 2 * DN_HEADS,), dn_w_in.dtype)
    dn_w_all = jnp.concatenate([dn_w_in, dn_pad], axis=2).astype(BF16)
    dn_w_out, da_w_in, da_w_out, lru_w_in, lru_w_out, mlp_w1, mlp_w2 = (
        w.astype(BF16) for w in (dn_w_out, da_w_in, da_w_out, lru_w_in, lru_w_out, mlp_w1, mlp_w2))
    for layer in range(depth):
        kind = layer % N_MIXERS
        slot = layer // N_MIXERS
        if kind == 0:
            y = gated_deltanet_mixer(h, norm_mix_w[layer], (dn_w_all, slot), dn_conv_w[slot],
                                     dn_a_log[slot], dn_dt_bias[slot], dn_norm_w[slot])
            w_out = dn_w_out
        elif kind == 1:
            lambda_init = 0.8 - 0.6 * math.exp(-0.3 * layer)
            y = diff_attention_mixer(h, norm_mix_w[layer], (da_w_in, slot), da_lam_q1[slot],
                                     da_lam_k1[slot], da_lam_q2[slot], da_lam_k2[slot],
                                     da_subln_w[slot], rel_bias, lambda_init)
            w_out = da_w_out
        else:
            y = rglru_mixer(h, norm_mix_w[layer], (lru_w_in, slot), lru_conv_w[slot],
                            lru_conv_b[slot], lru_w_rgate[slot], lru_b_rgate[slot],
                            lru_w_igate[slot], lru_b_igate[slot], lru_lambda[slot])
            w_out = lru_w_out
        final_w = final_norm_w if layer == depth - 1 else None
        h = block_tail(y, (w_out, slot), h, norm_mlp_w[layer], (mlp_w1, layer), (mlp_w2, layer),
                       final_w)
    return h
```

```python
import functools
import math

import jax
import jax.numpy as jnp
from jax import lax
from jax.experimental import pallas as pl
from jax.experimental.pallas import tpu as pltpu

F32 = jnp.float32
BF16 = jnp.bfloat16

D_MODEL = 1024
N_META = 16
QBLOCK = 128
FRONT_PAD = QBLOCK - N_META
N_MIXERS = 3
EPS = 1e-6
CHUNK = 64
CHUNK_SHIFT = 6
SUBLANES = 8
LANES = 128
BF16_ROWS = 2 * SUBLANES
CONV_HALO = SUBLANES

DN_HEADS = 8
DN_HEAD = 128
DN_CONV = 4
DN_QKV = 3 * DN_HEADS * DN_HEAD
DN_SMALL = 128
DN_CHUNK = 128
INV_BASE_LOG = 4

DA_HEADS = 8
DA_HEAD = 64
N_BUCKETS = 32
MAX_DISTANCE = 128
NEG_INF = -1e30
LOG2E = math.log2(math.e)
DA_Q_SCALE = DA_HEAD ** -0.5 * LOG2E

LRU_WIDTH = 1024
LRU_BLOCKS = 4
LRU_BLOCK = LRU_WIDTH // LRU_BLOCKS
LRU_CONV = 4
LRU_C = 8.0
SCAN_GROUP = SUBLANES

V7X_VMEM_LIMIT_BYTES = 56 * 1024 * 1024


def _params(*semantics):
    return pltpu.CompilerParams(dimension_semantics=semantics,
                                vmem_limit_bytes=V7X_VMEM_LIMIT_BYTES)


def _pick(n, candidates):
    for c in candidates:
        if n % c == 0:
            return c
    raise ValueError(f"no tile for {n} in {candidates}")


def _dot(a, b):
    return jnp.dot(a, b, preferred_element_type=F32)


def _dot_nt(a, b):
    return lax.dot_general(a, b, (((1,), (1,)), ((), ())), preferred_element_type=F32)


def _split2(a):
    hi = a.astype(BF16)
    lo = (a - hi.astype(F32)).astype(BF16)
    return hi, lo


def _rmsnorm_rows(x, w):
    return x * lax.rsqrt(jnp.mean(x * x, axis=-1, keepdims=True) + EPS) * w


def _keep_rows(tile_index, rows):
    pos = tile_index * rows + lax.broadcasted_iota(jnp.int32, (rows, 1), 0)
    return pos >= FRONT_PAD


def _norm_matmul_kernel(h_ref, nw_ref, w_ref, *rest, scaled, tail):
    rest = list(rest)
    cs_ref = rest.pop(0) if scaled else None
    o_ref = rest.pop(0)
    t_ref = rest.pop(0) if tail else None
    u_ref = rest.pop(0)
    j = pl.program_id(2)

    @pl.when(j == 0)
    def _():
        u_ref[...] = _rmsnorm_rows(h_ref[0], nw_ref[...]).astype(BF16)

    y = _dot(u_ref[...], w_ref[...])
    if scaled:
        y = y * cs_ref[...]
    o_ref[0] = y.astype(o_ref.dtype)
    if tail:
        @pl.when(j == pl.num_programs(2) - 1)
        def _():
            t_ref[0] = y[:, y.shape[1] - tail:]


def norm_matmul(h, nw, w, out_dtype, col_scale=None, tail=0):
    b, L, d = h.shape
    w, slot = w
    n = w.shape[2]
    tm = _pick(L, (2080, 640, 320, 128))
    tn = _pick(n, (1408, 1024, 512, 128))
    scaled = col_scale is not None
    in_specs = [
        pl.BlockSpec((1, tm, d), lambda bi, i, j: (bi, i, 0)),
        pl.BlockSpec((1, d), lambda bi, i, j: (0, 0)),
        pl.BlockSpec((None, d, tn), lambda bi, i, j: (slot, 0, j)),
    ]
    args = [h, nw.reshape(1, d), w]
    if scaled:
        in_specs.append(pl.BlockSpec((1, tn), lambda bi, i, j: (0, j)))
        args.append(col_scale.astype(F32).reshape(1, n))
    out_specs = pl.BlockSpec((1, tm, tn), lambda bi, i, j: (bi, i, j))
    out_shape = jax.ShapeDtypeStruct((b, L, n), out_dtype)
    if tail:
        out_specs = [out_specs, pl.BlockSpec((1, tm, tail), lambda bi, i, j: (bi, i, 0))]
        out_shape = [out_shape, jax.ShapeDtypeStruct((b, L, tail), F32)]
    return pl.pallas_call(
        functools.partial(_norm_matmul_kernel, scaled=scaled, tail=tail),
        grid=(b, L // tm, n // tn),
        in_specs=in_specs,
        out_specs=out_specs,
        out_shape=out_shape,
        scratch_shapes=[pltpu.VMEM((tm, d), BF16)],
        compiler_params=_params("arbitrary", "arbitrary", "arbitrary"),
        name="norm_matmul",
    )(*args)


def _tail_kernel(y_ref, wo_ref, h_ref, nw_ref, w1_ref, w2_ref, fw_ref, o_ref,
                 u_ref, hs_ref, acc_ref, *, final_norm, chunks):
    f = pl.program_id(2)
    tm = h_ref.shape[1]

    def kept(x):
        if final_norm:
            return x
        return jnp.where(_keep_rows(pl.program_id(1), tm), x, 0.0)

    @pl.when(f == 0)
    def _():
        keep_all = None if final_norm else _keep_rows(pl.program_id(1), tm)
        step = tm // chunks
        for c in range(chunks):
            rows = slice(c * step, (c + 1) * step)
            mix = _dot(y_ref[0, rows, :], wo_ref[...])
            if keep_all is not None:
                mix = jnp.where(keep_all[rows], mix, 0.0)
            h1 = h_ref[0, rows, :] + mix
            hs_ref[rows, :] = h1
            u_ref[rows, :] = _rmsnorm_rows(h1, nw_ref[...]).astype(BF16)
        acc_ref[...] = jnp.zeros_like(acc_ref)

    a = _dot(u_ref[...], w1_ref[...])
    a = jnp.square(jnp.maximum(a, 0.0)).astype(BF16)
    acc_ref[...] += _dot(a, w2_ref[...])

    @pl.when(f == pl.num_programs(2) - 1)
    def _():
        hn = hs_ref[...] + kept(acc_ref[...])
        if final_norm:
            hn = _rmsnorm_rows(hn, fw_ref[...])
        o_ref[0] = hn


def block_tail(y, w_out, h, nw, w1, w2, final_w=None):
    b, L, d = h.shape
    (w_out, slot), (w1, layer), (w2, _) = w_out, w1, w2
    k = y.shape[-1]
    ff = w1.shape[2]
    tf = _pick(ff, (1024, 512, 128))
    final_norm = final_w is not None
    if final_norm:
        first_row = FRONT_PAD + N_META
        rows = L - first_row
        tm = _pick(rows, (1024, 512, 128))

        def row_spec(width):
            return pl.BlockSpec(
                (pl.Element(1), pl.Element(tm), pl.Element(width)),
                lambda bi, i, f: (bi, pl.multiple_of(first_row + i * tm, QBLOCK), 0))
    else:
        rows = L
        tm = _pick(L, (1040, 640, 320, 128))

        def row_spec(width):
            return pl.BlockSpec((1, tm, width), lambda bi, i, f: (bi, i, 0))
    fw = (final_w if final_norm else nw).reshape(1, d)
    chunks = _pick(tm // BF16_ROWS, (5, 4, 2, 1))
    return pl.pallas_call(
        functools.partial(_tail_kernel, final_norm=final_norm, chunks=chunks),
        grid=(b, rows // tm, ff // tf),
        in_specs=[
            row_spec(k),
            pl.BlockSpec((None, k, d), lambda bi, i, f: (slot, 0, 0)),
            row_spec(d),
            pl.BlockSpec((1, d), lambda bi, i, f: (0, 0)),
            pl.BlockSpec((None, d, tf), lambda bi, i, f: (layer, 0, f)),
            pl.BlockSpec((None, tf, d), lambda bi, i, f: (layer, f, 0)),
            pl.BlockSpec((1, d), lambda bi, i, f: (0, 0)),
        ],
        out_specs=pl.BlockSpec((1, tm, d), lambda bi, i, f: (bi, i, 0)),
        out_shape=jax.ShapeDtypeStruct((b, rows, d), F32),
        scratch_shapes=[pltpu.VMEM((tm, d), BF16), pltpu.VMEM((tm, d), F32),
                        pltpu.VMEM((tm, d), F32)],
        compiler_params=_params("arbitrary", "arbitrary", "arbitrary"),
        name="block_tail",
    )(y, w_out, h, nw.reshape(1, d), w1, w2, fw)


def _conv4_rows(x, w):
    x1 = pltpu.roll(x, 1, 0)
    near = x * w[3:4, :] + x1 * w[2:3, :]
    far = x * w[1:2, :] + x1 * w[0:1, :]
    return near + pltpu.roll(far, 2, 0)


def _causal_conv_rows(xs_ref, cur, halo, first_tile, w, width):
    assert width == 4
    xs_ref[0:CONV_HALO, :] = jnp.where(first_tile, 0.0, halo)
    xs_ref[CONV_HALO:, :] = cur
    return _conv4_rows(xs_ref[...], w)[CONV_HALO:, :]


def _approx_unit_lower_inverses(a_list):
    n = a_list[0].shape[0]
    row = lax.broadcasted_iota(jnp.int32, (n, n), 0)
    col = lax.broadcasted_iota(jnp.int32, (n, n), 1)

    def same_block(log_size):
        return lax.shift_right_logical(row, log_size) == lax.shift_right_logical(col, log_size)

    log_size = INV_BASE_LOG
    in_diag = same_block(log_size)
    eye = (row == col).astype(F32)
    ad = [jnp.where(in_diag, a, 0.0) for a in a_list]
    t = [eye - x for x in ad]
    bk = [x.astype(BF16) for x in ad]
    for _ in range(log_size - 1):
        bk = [_dot(x, x).astype(BF16) for x in bk]
        t = [ti + _dot(ti.astype(BF16), x) for ti, x in zip(t, bk)]
    while (1 << log_size) < n:
        sel = same_block(log_size + 1) & jnp.logical_not(same_block(log_size))
        off = [jnp.where(sel, a, 0.0).astype(BF16) for a in a_list]
        t16 = [ti.astype(BF16) for ti in t]
        left = [_dot(ti, o).astype(BF16) for ti, o in zip(t16, off)]
        t = [ti - _dot(x, ti16) for ti, x, ti16 in zip(t, left, t16)]
        log_size += 1
    return t


def _dn_chunk_kernel(q_ref, k_ref, v_ref, z_ref, s_ref, cw_ref, alog_ref, dtb_ref, nw_ref,
                     o_ref, state_ref, xs_ref):
    c = pl.program_id(0)
    n = DN_CHUNK
    nb = q_ref.shape[0]
    hd = DN_HEADS * DN_HEAD
    streams = [(b, h) for b in range(nb) for h in range(DN_HEADS)]

    @pl.when(c == 0)
    def _():
        state_ref[...] = jnp.zeros_like(state_ref)
        xs_ref[:, 0:CONV_HALO, :] = jnp.zeros((nb, CONV_HALO, 3 * hd), F32)

    for b in range(nb):
        xs_ref[b, CONV_HALO:, 0:hd] = q_ref[b].astype(F32)
        xs_ref[b, CONV_HALO:, hd:2 * hd] = k_ref[b].astype(F32)
        xs_ref[b, CONV_HALO:, 2 * hd:] = v_ref[b].astype(F32)

    def conv_act(b, g):
        lanes = slice(g * DN_HEAD, (g + 1) * DN_HEAD)
        x = xs_ref[b, :, lanes]
        acc = _conv4_rows(x, cw_ref[:, lanes])[CONV_HALO:, :]
        return acc * jax.nn.sigmoid(acc)

    def unit_rows(x):
        return x * lax.rsqrt(jnp.sum(x * x, axis=-1, keepdims=True) + EPS)

    keep = _keep_rows(c, n)
    row = lax.broadcasted_iota(jnp.int32, (n, n), 0)
    col = lax.broadcasted_iota(jnp.int32, (n, n), 1)
    incl = row >= col
    strict = row > col
    tri = incl.astype(BF16)

    beta, gc, gc_t = [], [], []
    for b in range(nb):
        small = s_ref[b]
        beta.append(jax.nn.sigmoid(small))
        sp = jnp.logaddexp(small + dtb_ref[...], 0.0)
        g = jnp.where(keep, -jnp.exp(alog_ref[...]) * sp, 0.0)
        g_hi = g.astype(BF16)
        r1 = g - g_hi.astype(F32)
        g_mid = r1.astype(BF16)
        g_lo = (r1 - g_mid.astype(F32)).astype(BF16)
        gcb = _dot(tri, g_hi) + (_dot(tri, g_mid) + _dot(tri, g_lo))
        gc.append(gcb)
        gc_t.append(gcb.T)

    def cols(h):
        return slice(h * DN_HEAD, (h + 1) * DN_HEAD)

    bcol = [beta[b][:, h:h + 1] for b, h in streams]
    gcol = [gc[b][:, DN_HEADS + h:DN_HEADS + h + 1] for b, h in streams]
    grow = [gc_t[b][DN_HEADS + h:DN_HEADS + h + 1, :] for b, h in streams]
    decay = [jnp.exp(jnp.where(incl, gi - gj, -jnp.inf)) for gi, gj in zip(gcol, grow)]
    k = [jnp.where(keep, unit_rows(conv_act(b, DN_HEADS + h)), 0.0) for b, h in streams]
    k16 = [x.astype(BF16) for x in k]
    kb = [x * bc for x, bc in zip(k, bcol)]
    a = [jnp.where(strict, _dot_nt(x.astype(BF16), y) * dc, 0.0)
         for x, y, dc in zip(kb, k16, decay)]
    t16 = [x.astype(BF16) for x in _approx_unit_lower_inverses(a)]
    a_split = [_split2(x) for x in a]

    egc = [jnp.exp(x) for x in gcol]
    s = [state_ref[i] for i in range(len(streams))]
    s16 = [x.astype(BF16) for x in s]
    rhs = [jnp.where(keep, conv_act(b, 2 * DN_HEADS + h), 0.0) * bc
           - _dot((kbi * e).astype(BF16), si)
           for (b, h), bc, kbi, e, si in zip(streams, bcol, kb, egc, s16)]
    x0 = [_dot(ti, r.astype(BF16)) for ti, r in zip(t16, rhs)]
    resid = []
    for (ah, al), x, r in zip(a_split, x0, rhs):
        xh, xl = _split2(x)
        resid.append(r - x - (_dot(ah, xh) + (_dot(ah, xl) + _dot(al, xh))))
    v_new = [x + _dot(ti, r.astype(BF16)) for x, ti, r in zip(x0, t16, resid)]
    v16 = [x.astype(BF16) for x in v_new]

    q = [unit_rows(conv_act(b, h)) * (DN_HEAD ** -0.5) for b, h in streams]
    attn = [(_dot_nt(x.astype(BF16), y) * dc).astype(BF16) for x, y, dc in zip(q, k16, decay)]
    o = [_dot((x * e).astype(BF16), si) + _dot(at, vi)
         for x, e, si, at, vi in zip(q, egc, s16, attn, v16)]
    g_last = [x[n - 1:n, :] for x in gcol]
    kdec = [(x * jnp.exp(gl - gi)).T.astype(BF16) for x, gl, gi in zip(k, g_last, gcol)]
    for i, (si, gl, kd, vi) in enumerate(zip(s, g_last, kdec, v16)):
        state_ref[i] = si * jnp.exp(gl) + _dot(kd, vi)
    for (b, h), oi in zip(streams, o):
        zh = z_ref[b, :, cols(h)].astype(F32)
        y = _rmsnorm_rows(oi, nw_ref[...]) * (zh * jax.nn.sigmoid(zh))
        o_ref[b, :, cols(h)] = y.astype(o_ref.dtype)
    xs_ref[:, 0:CONV_HALO, :] = xs_ref[:, n:n + CONV_HALO, :]


def dn_chunk(proj, small, conv_w, a_log_row, dt_bias_row, norm_w):
    b, L, _ = proj.shape
    n = DN_CHUNK
    hd = DN_HEADS * DN_HEAD
    return pl.pallas_call(
        _dn_chunk_kernel,
        grid=(L // n,),
        in_specs=[
            pl.BlockSpec((b, n, hd), lambda c: (0, c, 0)),
            pl.BlockSpec((b, n, hd), lambda c: (0, c, 1)),
            pl.BlockSpec((b, n, hd), lambda c: (0, c, 2)),
            pl.BlockSpec((b, n, hd), lambda c: (0, c, 3)),
            pl.BlockSpec((b, n, DN_SMALL), lambda c: (0, c, 0)),
            pl.BlockSpec((DN_CONV, DN_QKV), lambda c: (0, 0)),
            pl.BlockSpec((1, DN_SMALL), lambda c: (0, 0)),
            pl.BlockSpec((1, DN_SMALL), lambda c: (0, 0)),
            pl.BlockSpec((1, DN_HEAD), lambda c: (0, 0)),
        ],
        out_specs=pl.BlockSpec((b, n, hd), lambda c: (0, c, 0)),
        out_shape=jax.ShapeDtypeStruct((b, L, hd), BF16),
        scratch_shapes=[pltpu.VMEM((b * DN_HEADS, DN_HEAD, DN_HEAD), F32),
                        pltpu.VMEM((b, CONV_HALO + n, DN_QKV), F32)],
        compiler_params=_params("arbitrary"),
        name="dn_chunk",
    )(proj, proj, proj, proj, small, conv_w, a_log_row, dt_bias_row,
      norm_w.reshape(1, DN_HEAD))


def gated_deltanet_mixer(h, nw, w_all, conv_w, a_log, dt_bias, norm_w):
    lane_pad = jnp.zeros((DN_SMALL - 2 * DN_HEADS,), F32)
    head_pad = jnp.zeros((DN_HEADS,), F32)
    a_log_row = jnp.concatenate([head_pad, a_log.astype(F32), lane_pad]).reshape(1, DN_SMALL)
    dt_bias_row = jnp.concatenate([head_pad, dt_bias.astype(F32), lane_pad]).reshape(1, DN_SMALL)
    proj, small = norm_matmul(h, nw, w_all, BF16, tail=DN_SMALL)
    return dn_chunk(proj, small, conv_w, a_log_row, dt_bias_row, norm_w)


def _t5_bucket(rel):
    nb = N_BUCKETS // 2
    ret = jnp.where(rel > 0, nb, 0)
    n = jnp.abs(rel)
    max_exact = nb // 2
    nf = jnp.maximum(n, 1).astype(F32)
    large = max_exact + (jnp.log(nf / max_exact) / math.log(MAX_DISTANCE / max_exact)
                         * (nb - max_exact)).astype(jnp.int32)
    large = jnp.minimum(large, nb - 1)
    return ret + jnp.where(n < max_exact, n, large)


def _bias_tile_kernel(tab_ref, o_ref):
    h = pl.program_id(0)
    which = pl.program_id(1)
    key = lax.broadcasted_iota(jnp.int32, (QBLOCK, QBLOCK), 0)
    query = lax.broadcasted_iota(jnp.int32, (QBLOCK, QBLOCK), 1)
    bucket = _t5_bucket(key - query - QBLOCK * which)
    acc = jnp.zeros((QBLOCK, QBLOCK), F32)
    for bkt in range(N_BUCKETS):
        acc = jnp.where(bucket == bkt, tab_ref[bkt, h], acc)
    o_ref[0, 0] = (acc - tab_ref[N_BUCKETS // 2 - 1, h]) * LOG2E


def bias_tiles(rel_bias):
    return pl.pallas_call(
        _bias_tile_kernel,
        grid=(DA_HEADS, 2),
        in_specs=[pl.BlockSpec(memory_space=pltpu.SMEM)],
        out_specs=pl.BlockSpec((1, 1, QBLOCK, QBLOCK), lambda h, w: (h, w, 0, 0)),
        out_shape=jax.ShapeDtypeStruct((DA_HEADS, 2, QBLOCK, QBLOCK), F32),
        compiler_params=_params("arbitrary", "arbitrary"),
        name="bias_tiles",
    )(rel_bias.astype(F32))


def _da_kernel(q_ref, qnext_ref, k_ref, v_ref, bias_ref, lamv_ref, subw_ref, o_ref,
               vt_ref, qt_ref, s_ref, p_ref, acc_ref, mblk_ref, m_ref, l_ref,
               *, tq, lambda_init):
    qi = pl.program_id(2)
    nsub = tq // QBLOCK
    hw = 2 * DA_HEAD
    n_blocks = k_ref.shape[1] // tq

    @pl.when(qi == 0)
    def _():
        def prep(t, carry):
            rows = pl.ds(pl.multiple_of(t * tq, tq), tq)
            vt_ref[t] = v_ref[0, rows, :].astype(F32).T.astype(BF16)
            return carry

        lax.fori_loop(0, n_blocks, prep, 0)

    def load_queries(src_ref):
        feat = lax.broadcasted_iota(jnp.int32, (hw, 1), 0)
        q_t = src_ref[0].astype(F32).T
        qt_ref[:, :tq] = jnp.where(feat < DA_HEAD, q_t, 0.0).astype(BF16)
        qt_ref[:, tq:] = jnp.where(feat >= DA_HEAD, q_t, 0.0).astype(BF16)

    m_ref[...] = jnp.full(m_ref.shape, NEG_INF, F32)
    l_ref[...] = jnp.zeros_like(l_ref)
    acc_ref[...] = jnp.zeros_like(acc_ref)

    def sub_rows(j):
        return slice(j * QBLOCK, (j + 1) * QBLOCK)

    def near_terms(s_half, j, diag, sub):
        tiles = [s_half[:, sub_rows(qq)] for qq in range(nsub)]
        if sub and j == nsub - 1:
            tiles[0] = tiles[0] + bias_ref[0, 1]
        if diag:
            key = lax.broadcasted_iota(jnp.int32, (QBLOCK, QBLOCK), 0)
            query = lax.broadcasted_iota(jnp.int32, (QBLOCK, QBLOCK), 1)
            allowed = (lax.shift_right_logical(key, CHUNK_SHIFT)
                       <= lax.shift_right_logical(query, CHUNK_SHIFT))
            for qq in range(j):
                tiles[qq] = jnp.full((QBLOCK, QBLOCK), NEG_INF, F32)
            tiles[j] = jnp.where(allowed, tiles[j] + bias_ref[0, 0], NEG_INF)
            if j + 1 < nsub:
                tiles[j + 1] = tiles[j + 1] + bias_ref[0, 1]
        return jnp.concatenate(tiles, axis=1)

    def scores_part(kb, j0, nj, diag, sub, first):
        rows = pl.ds(pl.multiple_of(kb * tq + j0 * QBLOCK, QBLOCK), nj * QBLOCK)
        s = _dot(k_ref[0, rows, :], qt_ref[...])
        pieces = []
        for jj in range(nj):
            j = j0 + jj
            piece = s[sub_rows(jj), :]
            if diag or (sub and j == nsub - 1):
                piece = jnp.concatenate([near_terms(piece[:, :tq], j, diag, sub),
                                         near_terms(piece[:, tq:], j, diag, sub)], axis=1)
            if first and j == 0:
                valid = lax.broadcasted_iota(jnp.int32, (QBLOCK, 1), 0) >= FRONT_PAD
                piece = jnp.where(valid, piece, NEG_INF)
            pieces.append(piece)
        s = pieces[0] if nj == 1 else jnp.concatenate(pieces, axis=0)
        s_ref[j0 * QBLOCK:(j0 + nj) * QBLOCK, :] = s
        return jnp.max(s.reshape(nj * QBLOCK // SUBLANES, SUBLANES, 2 * tq), axis=0)

    parts_after = {min(j0 + 1, nsub - 1): (j0, min(2, nsub - j0)) for j0 in range(0, nsub, 2)}

    def stage(kb, nxt=None, diag=False, sub=False, first=False):
        if kb is not None:
            m_old = m_ref[...]
            m_new = jnp.maximum(m_old, mblk_ref[...])
        lsum = None
        running = None
        for j in range(nsub):
            if kb is not None:
                p = jnp.exp2(s_ref[sub_rows(j), :] - m_new)
                lj = jnp.sum(p.reshape(QBLOCK // SUBLANES, SUBLANES, 2 * tq), axis=0)
                lsum = lj if lsum is None else lsum + lj
                p_ref[sub_rows(j), :] = p.astype(BF16)
            if nxt is not None and j in parts_after:
                mj = scores_part(nxt, *parts_after[j], diag, sub, first)
                running = mj if running is None else jnp.maximum(running, mj)
        if nxt is not None:
            mblk_ref[...] = jnp.max(running, axis=0, keepdims=True)
        if kb is not None:
            alpha = jnp.exp2(m_old - m_new)
            l_ref[...] = alpha * l_ref[...] + jnp.sum(lsum, axis=0, keepdims=True)
            acc_ref[...] = alpha * acc_ref[...] + _dot(vt_ref[kb], p_ref[...])
            m_ref[...] = m_new

    def region(pred, *args, **kwargs):
        @pl.when(pred)
        def _():
            stage(*args, **kwargs)

    last = pl.num_programs(2) - 1

    @pl.when(qi == 0)
    def _():
        load_queries(q_ref)

    region(qi < 1, None, 0, diag=True, first=True)

    region(qi == 1, 0, 1, diag=True)

    @pl.when(qi > 1)
    def _():
        def body(kb, carry):
            stage(kb, kb + 1)
            return carry

        lax.fori_loop(0, qi - 2, body, 0)

    region(qi >= 2, qi - 2, qi - 1, sub=True)
    region(qi - 2 >= 0, qi - 1, qi, diag=True)

    @pl.when(qi < last)
    def _():
        load_queries(qnext_ref)

    region((qi + 1 == 1) & (qi < last), qi, 0, sub=True, first=True)
    region((qi >= 1) & (qi < last), qi, 0, first=True)
    region(qi == last, qi)

    lamv = lamv_ref[...]
    lam = (jnp.exp(jnp.sum(lamv[0:1] * lamv[1:2], axis=-1, keepdims=True))
           - jnp.exp(jnp.sum(lamv[2:3] * lamv[3:4], axis=-1, keepdims=True)) + lambda_init)
    on = acc_ref[...] * (1.0 / l_ref[...])
    o_t = on[:, :tq] - lam * on[:, tq:]
    o = _rmsnorm_rows(o_t.T, subw_ref[...]) * (1.0 - lambda_init)
    o_ref[0] = o.astype(o_ref.dtype)


def diff_attention_core(proj, bias, lamv, subln_w, lambda_init):
    b, L, _ = proj.shape
    tq = _pick(L, (640, 128))
    hw = 2 * DA_HEAD
    return pl.pallas_call(
        functools.partial(_da_kernel, tq=tq, lambda_init=lambda_init),
        grid=(b, DA_HEADS, L // tq),
        in_specs=[
            pl.BlockSpec((1, tq, hw), lambda bi, h, i: (bi, i, h)),
            pl.BlockSpec((1, tq, hw), lambda bi, h, i: (bi, jnp.minimum(i + 1, L // tq - 1), h)),
            pl.BlockSpec((1, L, hw), lambda bi, h, i: (bi, 0, DA_HEADS + h)),
            pl.BlockSpec((1, L, hw), lambda bi, h, i: (bi, 0, 2 * DA_HEADS + h)),
            pl.BlockSpec((1, 2, QBLOCK, QBLOCK), lambda bi, h, i: (h, 0, 0, 0)),
            pl.BlockSpec((4, DA_HEAD), lambda bi, h, i: (0, 0)),
            pl.BlockSpec((1, hw), lambda bi, h, i: (0, 0)),
        ],
        out_specs=pl.BlockSpec((1, tq, hw), lambda bi, h, i: (bi, i, h)),
        out_shape=jax.ShapeDtypeStruct((b, L, DA_HEADS * hw), BF16),
        scratch_shapes=[pltpu.VMEM((L // tq, hw, tq), BF16), pltpu.VMEM((hw, 2 * tq), BF16),
                        pltpu.VMEM((tq, 2 * tq), F32), pltpu.VMEM((tq, 2 * tq), BF16),
                        pltpu.VMEM((hw, 2 * tq), F32),
                        pltpu.VMEM((1, 2 * tq), F32), pltpu.VMEM((1, 2 * tq), F32),
                        pltpu.VMEM((1, 2 * tq), F32)],
        compiler_params=_params("arbitrary", "arbitrary", "arbitrary"),
        name="diff_attention",
    )(proj, proj, proj, proj, bias, lamv, subln_w.reshape(1, hw))


def diff_attention_mixer(h, nw, w_in, lam_q1, lam_k1, lam_q2, lam_k2, subln_w,
                         rel_bias, lambda_init):
    qk = DA_HEADS * 2 * DA_HEAD
    col_scale = jnp.concatenate([jnp.full((qk,), DA_Q_SCALE, F32),
                                 jnp.ones((w_in[0].shape[2] - qk,), F32)])
    proj = norm_matmul(h, nw, w_in, BF16, col_scale)
    bias = bias_tiles(rel_bias)
    lamv = jnp.stack([lam_q1, lam_k1, lam_q2, lam_k2]).astype(F32)
    return diff_attention_core(proj, bias, lamv, subln_w, lambda_init)


def _lru_kernel(gate_ref, x_ref, halo_ref, cw_ref, cb_ref, wr_ref, br_ref, wi_ref, bi_ref,
                lam_ref, o_ref, xs_ref, a_ref, b_ref, h_ref):
    i = pl.program_id(1)
    tl = x_ref.shape[1]

    @pl.when(i == 0)
    def _():
        h_ref[...] = jnp.zeros_like(h_ref)

    halo = halo_ref[0, BF16_ROWS - CONV_HALO:, :].astype(F32)
    xr = _causal_conv_rows(xs_ref, x_ref[0].astype(F32), halo, i == 0, cw_ref[...], LRU_CONV)
    xr = jnp.where(_keep_rows(i, tl), xr + cb_ref[...], 0.0)
    neg_sp = -LRU_C * jnp.logaddexp(-lam_ref[...], 0.0)
    for g in range(LRU_BLOCKS):
        cols = slice(g * LRU_BLOCK, (g + 1) * LRU_BLOCK)
        xg = xr[:, cols]
        x16 = xg.astype(BF16)
        r = jax.nn.sigmoid(_dot(x16, wr_ref[g]) + br_ref[:, cols])
        ig = jax.nn.sigmoid(_dot(x16, wi_ref[g]) + bi_ref[:, cols])
        log_a = r * neg_sp[:, cols]
        a = jnp.exp(log_a)
        inp = jnp.sqrt(jnp.maximum(-jnp.tanh(log_a) * (a * a + 1.0), 0.0)) * (ig * xg)
        for half in range(LRU_BLOCK // LANES):
            lanes = slice(half * LANES, (half + 1) * LANES)
            a_ref[g * (LRU_BLOCK // LANES) + half] = a[:, lanes]
            b_ref[g * (LRU_BLOCK // LANES) + half] = inp[:, lanes]

    groups = tl // SCAN_GROUP
    for cb in range(LRU_WIDTH // LANES):
        a_cum = a_ref[cb, pl.ds(0, groups, stride=SCAN_GROUP), :]
        b_loc = b_ref[cb, pl.ds(0, groups, stride=SCAN_GROUP), :]
        for r in range(1, SCAN_GROUP):
            rows_r = pl.ds(r, groups, stride=SCAN_GROUP)
            a_r = a_ref[cb, rows_r, :]
            b_loc = a_r * b_loc + b_ref[cb, rows_r, :]
            a_cum = a_cum * a_r
            a_ref[cb, rows_r, :] = a_cum
            b_ref[cb, rows_r, :] = b_loc

    def body(t, hprev):
        rows = pl.ds(pl.multiple_of(t * SCAN_GROUP, SCAN_GROUP), SCAN_GROUP)
        hs = b_ref[:, rows, :] + a_ref[:, rows, :] * hprev
        b_ref[:, rows, :] = hs
        return hs[:, SCAN_GROUP - 1:SCAN_GROUP, :]

    h_ref[...] = lax.fori_loop(0, groups, body, h_ref[...])
    gate = gate_ref[0].astype(F32)
    gelu = 0.5 * gate * (1.0 + jnp.tanh(math.sqrt(2.0 / math.pi)
                                        * (gate + 0.044715 * (gate * gate * gate))))
    hs_all = jnp.concatenate([b_ref[cb] for cb in range(LRU_WIDTH // LANES)], axis=1)
    o_ref[0] = (hs_all * gelu).astype(o_ref.dtype)


def lru_core(proj, conv_w, conv_b, w_r, b_r, w_i, b_i, lam):
    b, L, _ = proj.shape
    tl = _pick(L, (640, 320, 128))
    wd = LRU_WIDTH
    row = lambda a: a.astype(F32).reshape(1, wd)
    return pl.pallas_call(
        _lru_kernel,
        grid=(b, L // tl),
        in_specs=[
            pl.BlockSpec((1, tl, wd), lambda bi, i: (bi, i, 0)),
            pl.BlockSpec((1, tl, wd), lambda bi, i: (bi, i, 1)),
            pl.BlockSpec((1, BF16_ROWS, wd),
                         lambda bi, i: (bi, jnp.maximum(i * (tl // BF16_ROWS) - 1, 0), 1)),
            pl.BlockSpec((LRU_CONV, wd), lambda bi, i: (0, 0)),
            pl.BlockSpec((1, wd), lambda bi, i: (0, 0)),
            pl.BlockSpec((LRU_BLOCKS, LRU_BLOCK, LRU_BLOCK), lambda bi, i: (0, 0, 0)),
            pl.BlockSpec((1, wd), lambda bi, i: (0, 0)),
            pl.BlockSpec((LRU_BLOCKS, LRU_BLOCK, LRU_BLOCK), lambda bi, i: (0, 0, 0)),
            pl.BlockSpec((1, wd), lambda bi, i: (0, 0)),
            pl.BlockSpec((1, wd), lambda bi, i: (0, 0)),
        ],
        out_specs=pl.BlockSpec((1, tl, wd), lambda bi, i: (bi, i, 0)),
        out_shape=jax.ShapeDtypeStruct((b, L, wd), BF16),
        scratch_shapes=[pltpu.VMEM((tl + CONV_HALO, wd), F32),
                        pltpu.VMEM((wd // LANES, tl, LANES), F32),
                        pltpu.VMEM((wd // LANES, tl, LANES), F32),
                        pltpu.VMEM((wd // LANES, 1, LANES), F32)],
        compiler_params=_params("arbitrary", "arbitrary"),
        name="rglru",
    )(proj, proj, proj, conv_w, row(conv_b), w_r.astype(BF16), row(b_r), w_i.astype(BF16),
      row(b_i), row(lam))


def rglru_mixer(h, nw, w_in, conv_w, conv_b, w_r, b_r, w_i, b_i, lam):
    proj = norm_matmul(h, nw, w_in, BF16)
    return lru_core(proj, conv_w, conv_b, w_r, b_r, w_i, b_i, lam)


def kernel(x, meta_tokens, rel_bias, norm_mix_w, norm_mlp_w, final_norm_w, dn_w_in, dn_conv_w, dn_a_log, dn_dt_bias, dn_norm_w, dn_w_out, da_w_in, da_lam_q1, da_lam_k1, da_lam_q2, da_lam_k2, da_subln_w, da_w_out, lru_w_in, lru_conv_w, lru_conv_b, lru_w_rgate, lru_b_rgate, lru_w_igate, lru_b_igate, lru_lambda, lru_w_out, mlp_w1, mlp_w2):
    b = x.shape[0]
    depth = norm_mix_w.shape[0]
    h = jnp.concatenate([
        jnp.zeros((b, FRONT_PAD, D_MODEL), x.dtype),
        jnp.broadcast_to(meta_tokens[None].astype(x.dtype), (b, N_META, D_MODEL)),
        x,
    ], axis=1)
    dn_pad = jnp.zeros(dn_w_in.shape[:2] + (DN_SMALL - 2 * DN_HEADS,), dn_w_in.dtype)
    dn_w_all = jnp.concatenate([dn_w_in, dn_pad], axis=2).astype(BF16)
    dn_w_out, da_w_in, da_w_out, lru_w_in, lru_w_out, mlp_w1, mlp_w2 = (
        w.astype(BF16) for w in (dn_w_out, da_w_in, da_w_out, lru_w_in, lru_w_out, mlp_w1, mlp_w2))
    for layer in range(depth):
        kind = layer % N_MIXERS
        slot = layer // N_MIXERS
        if kind == 0:
            y = gated_deltanet_mixer(h, norm_mix_w[layer], (dn_w_all, slot), dn_conv_w[slot],
                                     dn_a_log[slot], dn_dt_bias[slot], dn_norm_w[slot])
            w_out = dn_w_out
        elif kind == 1:
            lambda_init = 0.8 - 0.6 * math.exp(-0.3 * layer)
            y = diff_attention_mixer(h, norm_mix_w[layer], (da_w_in, slot), da_lam_q1[slot],
                                     da_lam_k1[slot], da_lam_q2[slot], da_lam_k2[slot],
                                     da_subln_w[slot], rel_bias, lambda_init)
            w_out = da_w_out
        else:
            y = rglru_mixer(h, norm_mix_w[layer], (lru_w_in, slot), lru_conv_w[slot],
                            lru_conv_b[slot], lru_w_rgate[slot], lru_b_rgate[slot],
                            lru_w_igate[slot], lru_b_igate[slot], lru_lambda[slot])
            w_out = lru_w_out
        final_w = final_norm_w if layer == depth - 1 else None
        h = block_tail(y, (w_out, slot), h, norm_mlp_w[layer], (mlp_w1, layer), (mlp_w2, layer),
                       final_w)
    return h
```
